```python
import jax
import jax.numpy as jnp
from jax import lax
import numpy as np

D_MODEL = 1024
BATCH = 8
SEQ = 2048
DEPTH = 1
DEC_BATCH = 32
DEC_SEQ = 4
PAST_LEN = 16384
PAGE_SIZE = 128

N_ATT_HEADS = 8
HEAD_DIM = 64
ATT_WIDTH = N_ATT_HEADS * HEAD_DIM
N_SGU_GROUPS = 8
SGU_DIM = 64
SGU_WIDTH = N_SGU_GROUPS * SGU_DIM
MIX_WIDTH = ATT_WIDTH + SGU_WIDTH
CHUNK = 128
Q_BLOCK = 128
D_FF = 2816
CONV_W = 3
N_MOD = 6
EPS = 1e-6
ATT_SCALE = HEAD_DIM ** -0.5
FORGET_BIAS = 4.0
Q0 = 0
K0 = Q0 + ATT_WIDTH
V0 = K0 + ATT_WIDTH
F0 = V0 + ATT_WIDTH
U0 = F0 + N_ATT_HEADS
S0 = U0 + SGU_WIDTH
IN_COLS = S0 + SGU_WIDTH

kernel_name = 'hymba_fox_sgu_convffn_adaln_step'

F32 = jnp.float32


def rms_norm(x, g):
    xf = x.astype(F32)
    y = xf * lax.rsqrt(jnp.mean(xf * xf, axis=-1, keepdims=True) + EPS)
    return (y * g.astype(F32)).astype(x.dtype)


def group_layer_norm(x, g, b):
    xf = x.astype(F32)
    mu = jnp.mean(xf, axis=-1, keepdims=True)
    var = jnp.mean(jnp.square(xf - mu), axis=-1, keepdims=True)
    return ((xf - mu) * lax.rsqrt(var + EPS) * g.astype(F32) + b.astype(F32)).astype(x.dtype)


def ada_modulation(c, w_ada, b_ada):
    m = jax.nn.silu(c) @ w_ada + b_ada
    return jnp.split(m, N_MOD, axis=-1)


def modulate(h, shift, scale):
    return h * (1.0 + scale[:, None, :]) + shift[:, None, :]


def mixer_inputs(h, w_in, b_f, ln_g, ln_b):
    B, T, _ = h.shape
    z = h @ w_in
    q = z[..., Q0:K0].reshape(B, T, N_ATT_HEADS, HEAD_DIM)
    k = z[..., K0:V0].reshape(B, T, N_ATT_HEADS, HEAD_DIM)
    v = z[..., V0:F0].reshape(B, T, N_ATT_HEADS, HEAD_DIM)
    logf = jax.nn.log_sigmoid(z[..., F0:U0].astype(F32) + b_f.astype(F32))
    u = jax.nn.gelu(z[..., U0:S0]).reshape(B, T, N_SGU_GROUPS, SGU_DIM)
    sv = jax.nn.gelu(z[..., S0:IN_COLS]).reshape(B, T, N_SGU_GROUPS, SGU_DIM)
    sv = group_layer_norm(sv, ln_g, ln_b)
    return q, k, v, logf, u, sv


def fox_prompt(q, k, v, logf):
    B, S, H, hd = q.shape
    nb = S // Q_BLOCK
    cum = jnp.cumsum(logf, axis=1)
    cum_k = cum.transpose(0, 2, 1)[:, :, None, :]
    qb = q.reshape(B, nb, Q_BLOCK, H, hd).transpose(1, 0, 2, 3, 4)
    cb = cum.reshape(B, nb, Q_BLOCK, H).transpose(1, 0, 3, 2)
    kpos = jnp.arange(S)

    def block(args):
        i, qi, ci = args
        s = jnp.einsum('bqhd,bkhd->bhqk', qi, k).astype(F32) * ATT_SCALE
        s = s + ci[..., None] - cum_k
        qpos = i * Q_BLOCK + jnp.arange(Q_BLOCK)
        causal = kpos[None, :] <= qpos[:, None]
        s = jnp.where(causal, s, -jnp.inf)
        p = jax.nn.softmax(s, axis=-1).astype(v.dtype)
        return jnp.einsum('bhqk,bkhd->bqhd', p, v)

    out = lax.map(block, (jnp.arange(nb), qb, cb))
    return out.transpose(1, 0, 2, 3, 4).reshape(B, S, H * hd)


def fox_sample(q, k_new, v_new, logf_new, k_past, v_past, logf_past):
    Bd, T, H, hd = q.shape
    P = k_past.shape[1]
    lp = logf_past.astype(F32)
    suffix = lax.cumsum(lp, axis=1, reverse=True) - lp
    cq = jnp.cumsum(logf_new.astype(F32), axis=1).transpose(0, 2, 1)
    s_past = (jnp.einsum('bthd,bshd->bhts', q, k_past).astype(F32) * ATT_SCALE
              + cq[..., None] + suffix.transpose(0, 2, 1)[:, :, None, :])
    s_new = (jnp.einsum('bthd,bshd->bhts', q, k_new).astype(F32) * ATT_SCALE
             + cq[..., None] - cq[:, :, None, :])
    tpos = jnp.arange(T)
    s_new = jnp.where(tpos[None, :] <= tpos[:, None], s_new, -jnp.inf)
    p = jax.nn.softmax(jnp.concatenate([s_past, s_new], axis=-1), axis=-1).astype(v_new.dtype)
    out = (jnp.einsum('bhts,bshd->bthd', p[..., :P], v_past)
           + jnp.einsum('bhts,bshd->bthd', p[..., P:], v_new))
    return out.reshape(Bd, T, H * hd)


def sgu_mix(u, sv, w_s, b_s, n):
    pos = jnp.arange(n)
    causal = pos[:, None] >= pos[None, :]
    w = jnp.where(causal[None], w_s[:, :n, :n], 0.0)
    mixed = jnp.einsum('gts,bcsgd->bctgd', w, sv) + b_s[:, :n].T[None, None, :, :, None]
    return u * mixed


def conv_ffn(h, prev, w_up, w_conv, b_conv, w_down):
    T = h.shape[1]
    up = h @ w_up
    full = jnp.concatenate([prev.astype(up.dtype), up], axis=1)
    conv = sum((full[:, j:j + T] * w_conv[j] for j in range(CONV_W)), b_conv)
    g, val = jnp.split(conv, 2, axis=-1)
    out = (jax.nn.silu(g) * val) @ w_down
    return out, full[:, -(CONV_W - 1):]


def setup_inputs(seed: int = 0) -> dict:
    key = jax.random.key(seed)
    ks = jax.random.split(key, 32)
    n_pages = PAST_LEN // PAGE_SIZE
    n_used = DEC_BATCH * n_pages
    n_phys = n_used + (n_used + 3) // 4

    def nrm(k, shape, s):
        return s * jax.random.normal(k, shape, F32)

    x_prompt = nrm(ks[0], (BATCH, SEQ, D_MODEL), 1.0)
    x_sample = nrm(ks[1], (DEC_BATCH, DEC_SEQ, D_MODEL), 1.0)
    c_prompt = nrm(ks[2], (BATCH, D_MODEL), 1.0)
    c_sample = nrm(ks[3], (DEC_BATCH, D_MODEL), 1.0)
    cache_k = nrm(ks[4], (DEPTH, n_phys, PAGE_SIZE, N_ATT_HEADS, HEAD_DIM), 1.0)
    cache_v = nrm(ks[5], (DEPTH, n_phys, PAGE_SIZE, N_ATT_HEADS, HEAD_DIM), 1.0)
    cache_logf = jax.nn.log_sigmoid(FORGET_BIAS + jax.random.normal(ks[6], (DEPTH, n_phys, PAGE_SIZE, N_ATT_HEADS), F32))
    state_conv = nrm(ks[7], (DEPTH, DEC_BATCH, CONV_W - 1, 2 * D_FF), 1.0)
    perm = jax.random.permutation(ks[8], n_phys)
    page_table = perm[:n_used].reshape(DEC_BATCH, n_pages).astype(jnp.int32)

    w_ada = nrm(ks[9], (DEPTH, D_MODEL, N_MOD * D_MODEL), 0.5 * D_MODEL ** -0.5)
    b_ada = nrm(ks[10], (DEPTH, N_MOD * D_MODEL), 0.02)
    norm1_g = 1.0 + nrm(ks[11], (DEPTH, D_MODEL), 0.02)
    w_in = nrm(ks[12], (DEPTH, D_MODEL, IN_COLS), D_MODEL ** -0.5)
    b_f = FORGET_BIAS + nrm(ks[13], (DEPTH, N_ATT_HEADS), 0.5)
    ln_v_g = 1.0 + nrm(ks[14], (DEPTH, N_SGU_GROUPS, SGU_DIM), 0.02)
    ln_v_b = nrm(ks[15], (DEPTH, N_SGU_GROUPS, SGU_DIM), 0.02)
    w_s = nrm(ks[16], (DEPTH, N_SGU_GROUPS, CHUNK, CHUNK), CHUNK ** -0.5)
    b_s = 1.0 + nrm(ks[17], (DEPTH, N_SGU_GROUPS, CHUNK), 0.02)
    w_o = nrm(ks[18], (DEPTH, MIX_WIDTH, D_MODEL), MIX_WIDTH ** -0.5)
    norm2_g = 1.0 + nrm(ks[19], (DEPTH, D_MODEL), 0.02)
    w_up = nrm(ks[20], (DEPTH, D_MODEL, 2 * D_FF), D_MODEL ** -0.5)
    w_conv = nrm(ks[21], (DEPTH, CONV_W, 2 * D_FF), CONV_W ** -0.5)
    b_conv = nrm(ks[22], (DEPTH, 2 * D_FF), 0.02)
    w_down = nrm(ks[23], (DEPTH, D_FF, D_MODEL), D_FF ** -0.5)
    final_g = 1.0 + nrm(ks[24], (D_MODEL,), 0.02)
    return {'x_prompt': x_prompt, 'x_sample': x_sample, 'c_prompt': c_prompt, 'c_sample': c_sample,
            'cache_k': cache_k, 'cache_v': cache_v, 'cache_logf': cache_logf, 'state_conv': state_conv,
            'page_table': page_table,
            'w_ada': w_ada, 'b_ada': b_ada, 'norm1_g': norm1_g, 'w_in': w_in, 'b_f': b_f,
            'ln_v_g': ln_v_g, 'ln_v_b': ln_v_b, 'w_s': w_s, 'b_s': b_s, 'w_o': w_o,
            'norm2_g': norm2_g, 'w_up': w_up, 'w_conv': w_conv, 'b_conv': b_conv, 'w_down': w_down,
            'final_g': final_g}


def reference(x_prompt, x_sample, c_prompt, c_sample, cache_k, cache_v, cache_logf, state_conv,
              page_table, w_ada, b_ada, norm1_g, w_in, b_f, ln_v_g, ln_v_b, w_s, b_s, w_o,
              norm2_g, w_up, w_conv, b_conv, w_down, final_g):
    n_pages = PAST_LEN // PAGE_SIZE
    Bp, S, _ = x_prompt.shape
    Bs, T, _ = x_sample.shape
    xp, xs = x_prompt, x_sample
    kp_l, vp_l, lfp_l, convp_l = [], [], [], []
    ks_l, vs_l, lfs_l, svs_l, convs_l = [], [], [], [], []
    for l in range(DEPTH):
        sh1p, sc1p, g1p, sh2p, sc2p, g2p = ada_modulation(c_prompt, w_ada[l], b_ada[l])
        sh1s, sc1s, g1s, sh2s, sc2s, g2s = ada_modulation(c_sample, w_ada[l], b_ada[l])

        hp = modulate(rms_norm(xp, norm1_g[l]), sh1p, sc1p)
        q, k, v, lf, u, sv = mixer_inputs(hp, w_in[l], b_f[l], ln_v_g[l], ln_v_b[l])
        att = fox_prompt(q, k, v, lf)
        nc = S // CHUNK
        sgu = sgu_mix(u.reshape(Bp, nc, CHUNK, N_SGU_GROUPS, SGU_DIM),
                      sv.reshape(Bp, nc, CHUNK, N_SGU_GROUPS, SGU_DIM),
                      w_s[l], b_s[l], CHUNK).reshape(Bp, S, SGU_WIDTH)
        xp = xp + g1p[:, None, :] * (jnp.concatenate([att, sgu], axis=-1) @ w_o[l])
        kp_l.append(k); vp_l.append(v); lfp_l.append(lf)

        hp2 = modulate(rms_norm(xp, norm2_g[l]), sh2p, sc2p)
        prev0 = jnp.zeros((Bp, CONV_W - 1, 2 * D_FF), hp2.dtype)
        fp, conv_p = conv_ffn(hp2, prev0, w_up[l], w_conv[l], b_conv[l], w_down[l])
        xp = xp + g2p[:, None, :] * fp
        convp_l.append(conv_p)

        hs = modulate(rms_norm(xs, norm1_g[l]), sh1s, sc1s)
        q, k, v, lf, u, sv = mixer_inputs(hs, w_in[l], b_f[l], ln_v_g[l], ln_v_b[l])
        k_past = cache_k[l][page_table].reshape(Bs, n_pages * PAGE_SIZE, N_ATT_HEADS, HEAD_DIM)
        v_past = cache_v[l][page_table].reshape(Bs, n_pages * PAGE_SIZE, N_ATT_HEADS, HEAD_DIM)
        lf_past = cache_logf[l][page_table].reshape(Bs, n_pages * PAGE_SIZE, N_ATT_HEADS)
        att = fox_sample(q, k, v, lf, k_past, v_past, lf_past)
        sgu = sgu_mix(u[:, None], sv[:, None], w_s[l], b_s[l], T).reshape(Bs, T, SGU_WIDTH)
        xs = xs + g1s[:, None, :] * (jnp.concatenate([att, sgu], axis=-1) @ w_o[l])
        ks_l.append(k); vs_l.append(v); lfs_l.append(lf); svs_l.append(sv)

        hs2 = modulate(rms_norm(xs, norm2_g[l]), sh2s, sc2s)
        fs, conv_s = conv_ffn(hs2, state_conv[l], w_up[l], w_conv[l], b_conv[l], w_down[l])
        xs = xs + g2s[:, None, :] * fs
        convs_l.append(conv_s)

    y_prompt = rms_norm(xp, final_g)
    y_sample = rms_norm(xs, final_g)
    return (y_prompt, y_sample,
            jnp.stack(kp_l), jnp.stack(vp_l), jnp.stack(lfp_l), jnp.stack(convp_l),
            jnp.stack(ks_l), jnp.stack(vs_l), jnp.stack(lfs_l), jnp.stack(svs_l), jnp.stack(convs_l))
```

```python
import functools

import jax
import jax.numpy as jnp
from jax import lax
from jax.experimental import pallas as pl
from jax.experimental.pallas import tpu as pltpu

F32 = jnp.float32
BF16 = jnp.bfloat16

N_HEADS = 8
HEAD_DIM = 64
ATT_W = N_HEADS * HEAD_DIM
N_GROUPS = 8
SGU_DIM = 64
SGU_W = N_GROUPS * SGU_DIM
CHUNK = 128
PAGE = 128
N_MOD = 6
EPS = 1e-6
ATT_SCALE = HEAD_DIM ** -0.5
LANES = 128
PAIR_W = 2 * HEAD_DIM
VMEM_LIMIT = 56 * 1024 * 1024

TM_IN = 512
TQ = 256
TM_FFN = 256
FC = 256
PAGES_PER_GROUP = 8


def _cparams(sem):
    return pltpu.CompilerParams(dimension_semantics=sem, vmem_limit_bytes=VMEM_LIMIT)


def _const_spec(shape):
    nd = len(shape)
    return pl.BlockSpec(shape, lambda *_: (0,) * nd)


def _split3(x):
    hi = x.astype(BF16)
    r1 = x - hi.astype(F32)
    mid = r1.astype(BF16)
    lo = (r1 - mid.astype(F32)).astype(BF16)
    return hi, mid, lo


def _dot3(x, w):
    hi, mid, lo = _split3(x)
    d = functools.partial(jnp.dot, preferred_element_type=F32)
    return d(hi, w) + d(mid, w) + d(lo, w)


def _rms_mod(x, g, shift, scale):
    y = x * lax.rsqrt(jnp.mean(x * x, axis=-1, keepdims=True) + EPS)
    return (y * g) * (1.0 + scale) + shift


def _log_sigmoid(z):
    return jnp.minimum(z, 0.0) - jnp.log1p(jnp.exp(-jnp.abs(z)))


def _group_ln(sv, gmat, ln_g, ln_b):
    mu = jnp.dot(sv.astype(BF16), gmat, preferred_element_type=F32)
    d = sv - mu
    var = jnp.dot((d * d).astype(BF16), gmat, preferred_element_type=F32)
    return d * lax.rsqrt(var + EPS) * ln_g + ln_b


def _sgu_pairs(wm, svn_b, u, bias):
    lane = lax.broadcasted_iota(jnp.int32, (CHUNK, LANES), 1)
    outs = []
    for p in range(N_GROUPS // 2):
        rhs = svn_b[:, p * LANES:(p + 1) * LANES]
        lo = jnp.dot(wm[2 * p], rhs, preferred_element_type=F32)
        hi = jnp.dot(wm[2 * p + 1], rhs, preferred_element_type=F32)
        mixed = jnp.where(lane < SGU_DIM, lo, hi) + bias[:, p * LANES:(p + 1) * LANES]
        outs.append(u[:, p * LANES:(p + 1) * LANES] * mixed)
    return jnp.concatenate(outs, axis=1)


def _ada_kernel(c_ref, w_ref, b_ref, o_ref):
    a = jax.nn.silu(c_ref[...])
    o_ref[...] = jnp.dot(a.astype(BF16), w_ref[...].astype(BF16),
                         preferred_element_type=F32) + b_ref[...]


def _ada(c_all, w_ada, b_ada):
    n, d = c_all.shape
    cols = w_ada.shape[1]
    bn = d
    return pl.pallas_call(
        _ada_kernel,
        grid=(cols // bn,),
        in_specs=[pl.BlockSpec((n, d), lambda j: (0, 0)),
                  pl.BlockSpec((d, bn), lambda j: (0, j)),
                  pl.BlockSpec((1, bn), lambda j: (0, j))],
        out_specs=pl.BlockSpec((n, bn), lambda j: (0, j)),
        out_shape=jax.ShapeDtypeStruct((n, cols), F32),
        compiler_params=_cparams(("arbitrary",)),
        name="ada_ln",
    )(c_all, w_ada, b_ada)


def _in_prompt_kernel(x_ref, mod_ref, g1_ref, wq_ref, wkvf_ref, wus_ref, bf_ref, lng_ref, lnb_ref,
                      gmat_ref, ws_ref, bs_ref, tri_ref,
                      q_ref, kt_ref, vt_ref, ktb_ref, vtb_ref, lft_ref, cumt_ref, cum_ref, sgu_ref,
                      carry_ref):
    s = pl.program_id(1)
    tm = x_ref.shape[1]

    @pl.when(s == 0)
    def _():
        carry_ref[...] = jnp.zeros_like(carry_ref)

    h = _rms_mod(x_ref[0], g1_ref[...], mod_ref[0, 0:1, :], mod_ref[0, 1:2, :]).astype(BF16)

    q = jnp.dot(h, wq_ref[...], preferred_element_type=F32)
    q_ref[0] = (q * ATT_SCALE).astype(BF16)

    kvf = lax.dot_general(wkvf_ref[...], h, (((1,), (1,)), ((), ())), preferred_element_type=F32)
    kt = kvf[0:ATT_W]
    vt = kvf[ATT_W:2 * ATT_W]
    kt_ref[0] = kt
    vt_ref[0] = vt
    ktb_ref[0, 0] = kt.astype(BF16)
    vtb_ref[0, 0] = vt.astype(BF16)
    lft = _log_sigmoid(kvf[2 * ATT_W:2 * ATT_W + N_HEADS] + bf_ref[...])
    lft_ref[0] = lft

    r = _dot3(lft, tri_ref[...])
    carry = carry_ref[...]
    cumt = r[:, 0:tm] + jnp.concatenate([carry] * (tm // LANES), axis=1)
    cumt_ref[0, 0] = cumt
    carry_ref[...] = carry + r[:, tm:tm + LANES]
    pad = jnp.zeros((LANES - N_HEADS, LANES), F32)
    for c in range(tm // LANES):
        blk = jnp.concatenate([cumt[:, c * LANES:(c + 1) * LANES], pad], axis=0)
        cum_ref[0, c * LANES:(c + 1) * LANES, :] = blk.T[:, 0:N_HEADS]

    us = jnp.dot(h, wus_ref[...], preferred_element_type=F32)
    u = jax.nn.gelu(us[:, 0:SGU_W])
    sv = jax.nn.gelu(us[:, SGU_W:2 * SGU_W])
    svn = _group_ln(sv, gmat_ref[...], lng_ref[...], lnb_ref[...]).astype(BF16)

    row = lax.broadcasted_iota(jnp.int32, (CHUNK, CHUNK), 0)
    col = lax.broadcasted_iota(jnp.int32, (CHUNK, CHUNK), 1)
    wm = [jnp.where(row >= col, ws_ref[g], 0.0).astype(BF16) for g in range(N_GROUPS)]
    bias = bs_ref[...]
    for c in range(tm // CHUNK):
        sl = slice(c * CHUNK, (c + 1) * CHUNK)
        sgu_ref[0, sl, :] = _sgu_pairs(wm, svn[sl], u[sl], bias).astype(BF16)


def _in_prompt(x, mod3, g1, wq, wkvf, wus, bf_col, lng, lnb, gmat, ws_b, bs_full, tri):
    b, s, d = x.shape
    tm = TM_IN
    ns = s // tm
    grid = (b, ns)
    row_blk = lambda w: pl.BlockSpec((1, tm, w), lambda i, j: (i, j, 0))
    colT_blk = lambda r: pl.BlockSpec((1, r, tm), lambda i, j: (i, 0, j))
    tiledT_blk = lambda r: pl.BlockSpec((1, 1, r, tm), lambda i, j: (i, j, 0, 0))
    out_shape = (
        jax.ShapeDtypeStruct((b, s, ATT_W), BF16),
        jax.ShapeDtypeStruct((b, ATT_W, s), F32),
        jax.ShapeDtypeStruct((b, ATT_W, s), F32),
        jax.ShapeDtypeStruct((b, ns, ATT_W, tm), BF16),
        jax.ShapeDtypeStruct((b, ns, ATT_W, tm), BF16),
        jax.ShapeDtypeStruct((b, N_HEADS, s), F32),
        jax.ShapeDtypeStruct((b, ns, N_HEADS, tm), F32),
        jax.ShapeDtypeStruct((b, s, N_HEADS), F32),
        jax.ShapeDtypeStruct((b, s, SGU_W), BF16),
    )
    out_specs = (row_blk(ATT_W), colT_blk(ATT_W), colT_blk(ATT_W), tiledT_blk(ATT_W), tiledT_blk(ATT_W),
                 colT_blk(N_HEADS), tiledT_blk(N_HEADS), row_blk(N_HEADS), row_blk(SGU_W))
    in_specs = [row_blk(d),
                pl.BlockSpec((1, N_MOD, d), lambda i, j: (i, 0, 0)),
                _const_spec(g1.shape), _const_spec(wq.shape), _const_spec(wkvf.shape), _const_spec(wus.shape),
                _const_spec(bf_col.shape), _const_spec(lng.shape), _const_spec(lnb.shape),
                _const_spec(gmat.shape), _const_spec(ws_b.shape), _const_spec(bs_full.shape),
                _const_spec(tri.shape)]
    return pl.pallas_call(
        _in_prompt_kernel, grid=grid, in_specs=in_specs, out_specs=out_specs, out_shape=out_shape,
        scratch_shapes=[pltpu.VMEM((N_HEADS, LANES), F32)],
        compiler_params=_cparams(("arbitrary", "arbitrary")),
        name="prompt_in_proj",
    )(x, mod3, g1, wq, wkvf, wus, bf_col, lng, lnb, gmat, ws_b, bs_full, tri)


def _attn_kernel(q_ref, kt_ref, vt_ref, cumt_ref, cum_ref, o_ref):
    qi = pl.program_id(1)
    tq = q_ref.shape[1]
    tk = kt_ref.shape[3]
    qs = qi * tq
    nkv = (qs + tq + tk - 1) // tk
    lane = lax.broadcasted_iota(jnp.int32, (tq, LANES), 1)
    qpos = qs + lax.broadcasted_iota(jnp.int32, (tq, tk), 0)
    kcol = lax.broadcasted_iota(jnp.int32, (tq, tk), 1)
    nt = (((1,), (1,)), ((), ()))

    for p in range(N_HEADS // 2):
        qp = q_ref[0, :, p * PAIR_W:(p + 1) * PAIR_W]
        outs = []
        for half in range(2):
            hh = 2 * p + half
            own = (lane >= half * HEAD_DIM) & (lane < (half + 1) * HEAD_DIM)
            qm = jnp.where(own, qp.astype(F32), 0.0).astype(BF16)
            cq = cum_ref[0, :, hh:hh + 1]

            def body(j, carry, p=p, hh=hh, qm=qm, cq=cq):
                m, l, acc = carry
                kt = kt_ref[0, j, p * PAIR_W:(p + 1) * PAIR_W, :]
                sc = jnp.dot(qm, kt, preferred_element_type=F32)
                ck = cumt_ref[0, j, hh:hh + 1, :]
                sc = sc + cq - ck
                sc = jnp.where(j * tk + kcol <= qpos, sc, -jnp.inf)
                m_new = jnp.maximum(m, jnp.max(sc, axis=1, keepdims=True))
                alpha = jnp.exp(m - m_new)
                pr = jnp.exp(sc - m_new)
                l = alpha * l + jnp.sum(pr, axis=1, keepdims=True)
                vt = vt_ref[0, j, p * PAIR_W:(p + 1) * PAIR_W, :]
                pv = lax.dot_general(pr.astype(BF16), vt, nt, preferred_element_type=F32)
                return m_new, l, alpha * acc + pv

            init = (jnp.full((tq, 1), -jnp.inf, F32), jnp.zeros((tq, 1), F32),
                    jnp.zeros((tq, PAIR_W), F32))
            m, l, acc = lax.fori_loop(0, nkv, body, init)
            outs.append(acc / l)
        o_ref[0, :, p * PAIR_W:(p + 1) * PAIR_W] = jnp.where(lane < HEAD_DIM, outs[0], outs[1]).astype(BF16)


def _attn_prompt(q, ktb, vtb, cumt, cum):
    b, s, _ = q.shape
    nk, tk = ktb.shape[1], ktb.shape[3]
    tq = TQ
    return pl.pallas_call(
        _attn_kernel,
        grid=(b, s // tq),
        in_specs=[pl.BlockSpec((1, tq, ATT_W), lambda i, j: (i, j, 0)),
                  pl.BlockSpec((1, nk, ATT_W, tk), lambda i, j: (i, 0, 0, 0)),
                  pl.BlockSpec((1, nk, ATT_W, tk), lambda i, j: (i, 0, 0, 0)),
                  pl.BlockSpec((1, nk, N_HEADS, tk), lambda i, j: (i, 0, 0, 0)),
                  pl.BlockSpec((1, tq, N_HEADS), lambda i, j: (i, j, 0))],
        out_specs=pl.BlockSpec((1, tq, ATT_W), lambda i, j: (i, j, 0)),
        out_shape=jax.ShapeDtypeStruct((b, s, ATT_W), BF16),
        compiler_params=_cparams(("arbitrary", "arbitrary")),
        name="prompt_attention",
    )(q, ktb, vtb, cumt, cum)


def _ffn_prompt_kernel(x_ref, att_ref, sgu_ref, mod_ref, wo_ref, g2_ref, wug_ref, wuv_ref, wcg_ref, wcv_ref,
                       bcg_ref, bcv_ref, wd_ref, gf_ref,
                       y_ref, conv_ref,
                       fg_ref, fv_ref, act_ref):
    s = pl.program_id(1)
    tm = x_ref.shape[1]
    dff = wd_ref.shape[0]

    @pl.when(s == 0)
    def _():
        fg_ref[0:8, :] = jnp.zeros((8, dff), F32)
        fv_ref[0:8, :] = jnp.zeros((8, dff), F32)

    @pl.when(s > 0)
    def _():
        fg_ref[0:8, :] = fg_ref[tm:tm + 8, :]
        fv_ref[0:8, :] = fv_ref[tm:tm + 8, :]

    mix = (jnp.dot(att_ref[0], wo_ref[0:ATT_W, :], preferred_element_type=F32)
           + jnp.dot(sgu_ref[0], wo_ref[ATT_W:ATT_W + SGU_W, :], preferred_element_type=F32))
    x1 = x_ref[0] + mod_ref[0, 2:3, :] * mix
    h2 = _rms_mod(x1, g2_ref[...], mod_ref[0, 3:4, :], mod_ref[0, 4:5, :]).astype(BF16)

    for c in range(dff // FC):
        cs = slice(c * FC, (c + 1) * FC)
        fg_ref[8:8 + tm, cs] = jnp.dot(h2, wug_ref[:, cs], preferred_element_type=F32)
        fv_ref[8:8 + tm, cs] = jnp.dot(h2, wuv_ref[:, cs], preferred_element_type=F32)
        cg = bcg_ref[:, cs]
        cv = bcv_ref[:, cs]
        for j in range(3):
            cg = cg + fg_ref[6 + j:6 + j + tm, cs] * wcg_ref[j:j + 1, cs]
            cv = cv + fv_ref[6 + j:6 + j + tm, cs] * wcv_ref[j:j + 1, cs]
        act_ref[:, cs] = (jax.nn.silu(cg) * cv).astype(BF16)

    f = jnp.dot(act_ref[...], wd_ref[...], preferred_element_type=F32)
    x2 = x1 + mod_ref[0, 5:6, :] * f
    y_ref[0] = x2 * lax.rsqrt(jnp.mean(x2 * x2, axis=-1, keepdims=True) + EPS) * gf_ref[...]

    @pl.when(s == pl.num_programs(1) - 1)
    def _():
        conv_ref[0, :, 0:dff] = fg_ref[tm:tm + 8, :]
        conv_ref[0, :, dff:2 * dff] = fv_ref[tm:tm + 8, :]


def _ffn_prompt(x, att, sgu, mod3, wo, g2, wug, wuv, wcg, wcv, bcg, bcv, wd, gf):
    b, s, d = x.shape
    tm = TM_FFN
    dff = wd.shape[0]
    row_blk = lambda w: pl.BlockSpec((1, tm, w), lambda i, j: (i, j, 0))
    consts = (wo, g2, wug, wuv, wcg, wcv, bcg, bcv, wd, gf)
    return pl.pallas_call(
        _ffn_prompt_kernel,
        grid=(b, s // tm),
        in_specs=[row_blk(d), row_blk(ATT_W), row_blk(SGU_W),
                  pl.BlockSpec((1, N_MOD, d), lambda i, j: (i, 0, 0))]
                 + [_const_spec(a.shape) for a in consts],
        out_specs=(row_blk(d), pl.BlockSpec((1, 8, 2 * dff), lambda i, j: (i, 0, 0))),
        out_shape=(jax.ShapeDtypeStruct((b, s, d), F32), jax.ShapeDtypeStruct((b, 8, 2 * dff), F32)),
        scratch_shapes=[pltpu.VMEM((tm + 8, dff), F32), pltpu.VMEM((tm + 8, dff), F32),
                        pltpu.VMEM((tm, dff), BF16)],
        compiler_params=_cparams(("arbitrary", "arbitrary")),
        name="prompt_ffn",
    )(x, att, sgu, mod3, *consts)


def _in_sample_kernel(x_ref, mod_ref, g1_ref, wq_ref, wkv_ref, wf_ref, wus_ref, bf_ref, lng_ref, lnb_ref,
                      gmat_ref, wst_ref, bs_ref, same_ref,
                      q_ref, k_ref, v_ref, lf_ref, cq_ref, svn_ref, sgu_ref):
    n = x_ref.shape[0]
    h = _rms_mod(x_ref[...], g1_ref[...], mod_ref[0], mod_ref[1]).astype(BF16)
    q_ref[...] = jnp.dot(h, wq_ref[...], preferred_element_type=F32) * ATT_SCALE
    kv = jnp.dot(h, wkv_ref[...], preferred_element_type=F32)
    k_ref[...] = kv[:, 0:ATT_W]
    v_ref[...] = kv[:, ATT_W:2 * ATT_W]
    lf = _log_sigmoid(jnp.dot(h, wf_ref[...], preferred_element_type=F32) + bf_ref[...])
    lf_ref[...] = lf

    same = same_ref[...]
    hi, mid, lo = _split3(lf)
    tri = same.astype(BF16)
    d = functools.partial(jnp.dot, preferred_element_type=F32)
    cq_ref[...] = d(tri, hi) + d(tri, mid) + d(tri, lo)

    us = jnp.dot(h, wus_ref[...], preferred_element_type=F32)
    u = jax.nn.gelu(us[:, 0:SGU_W])
    sv = jax.nn.gelu(us[:, SGU_W:2 * SGU_W])
    svn = _group_ln(sv, gmat_ref[...], lng_ref[...], lnb_ref[...])
    svn_ref[...] = svn
    wm = [jnp.where(same > 0, wst_ref[g], 0.0).astype(BF16) for g in range(N_GROUPS)]
    sgu_ref[...] = _sgu_pairs(wm, svn.astype(BF16), u, bs_ref[...]).astype(BF16)


def _in_sample(x2, mod_rows, g1, wq, wkv, wf, wus, bf_row, lng, lnb, gmat, wst, bs_full, same):
    n, d = x2.shape
    args = (x2, mod_rows, g1, wq, wkv, wf, wus, bf_row, lng, lnb, gmat, wst, bs_full, same)
    out_shape = (jax.ShapeDtypeStruct((n, ATT_W), F32), jax.ShapeDtypeStruct((n, ATT_W), F32),
                 jax.ShapeDtypeStruct((n, ATT_W), F32), jax.ShapeDtypeStruct((n, LANES), F32),
                 jax.ShapeDtypeStruct((n, LANES), F32), jax.ShapeDtypeStruct((n, SGU_W), F32),
                 jax.ShapeDtypeStruct((n, SGU_W), BF16))
    return pl.pallas_call(
        _in_sample_kernel, grid=(1,),
        in_specs=[_const_spec(a.shape) for a in args],
        out_specs=tuple(_const_spec(o.shape) for o in out_shape),
        out_shape=out_shape,
        compiler_params=_cparams(("arbitrary",)),
        name="sample_in_proj",
    )(*args)


def _decode_kernel(pt_ref, q_ref, kn_ref, vn_ref, cqcol_ref, cqmat_ref, uo_ref, kc_hbm, vc_hbm, lc_hbm,
                   o_ref, kbuf, vbuf, lbuf, sem):
    b = pl.program_id(0)
    nb = pl.num_programs(0)
    t_new = q_ref.shape[1]
    rows = t_new * N_HEADS
    g_pages = kbuf.shape[1]
    n_groups = pt_ref.shape[1] // g_pages
    total = nb * n_groups
    nt = (((1,), (1,)), ((), ()))

    def copies(n, slot):
        bb = n // n_groups
        lg = n_groups - 1 - (n % n_groups)
        out = []
        for i in range(g_pages):
            phys = pt_ref[bb, lg * g_pages + i]
            out.append(pltpu.make_async_copy(kc_hbm.at[phys], kbuf.at[slot, i], sem.at[0, slot]))
            out.append(pltpu.make_async_copy(vc_hbm.at[phys], vbuf.at[slot, i], sem.at[1, slot]))
            out.append(pltpu.make_async_copy(lc_hbm.at[phys], lbuf.at[slot, i], sem.at[2, slot]))
        return out

    @pl.when(b == 0)
    def _():
        for c in copies(0, 0):
            c.start()

    sub = lax.broadcasted_iota(jnp.int32, (N_HEADS, ATT_W), 0)
    lane_head = lax.broadcasted_iota(jnp.int32, (N_HEADS, ATT_W), 1) // HEAD_DIM
    own = sub == lane_head
    q = q_ref[0]
    qbd = jnp.concatenate(
        [jnp.where(own, jnp.broadcast_to(q[t:t + 1, :], (N_HEADS, ATT_W)), 0.0) for t in range(t_new)], axis=0)
    qbd_b = qbd.astype(BF16)
    cqcol = cqcol_ref[0]
    rt = lax.broadcasted_iota(jnp.int32, (rows, 1), 0) // N_HEADS

    kn = kn_ref[0]
    vn = vn_ref[0]
    s_new = []
    for t2 in range(t_new):
        sc = jnp.sum(qbd * kn[t2:t2 + 1, :], axis=1, keepdims=True) + cqcol - cqmat_ref[0, :, t2:t2 + 1]
        s_new.append(jnp.where(rt >= t2, sc, -jnp.inf))
    m = s_new[0]
    for t2 in range(1, t_new):
        m = jnp.maximum(m, s_new[t2])
    l = jnp.zeros((rows, 1), F32)
    acc = jnp.zeros((rows, ATT_W), F32)
    for t2 in range(t_new):
        pr = jnp.exp(s_new[t2] - m)
        l = l + pr
        acc = acc + pr * vn[t2:t2 + 1, :]

    def group_body(g, carry):
        m, l, acc, c_run = carry
        n = b * n_groups + g
        slot = n % 2

        @pl.when(n + 1 < total)
        def _():
            for c in copies(n + 1, 1 - slot):
                c.start()

        for c in copies(n, slot):
            c.wait()

        lp = lbuf[slot].reshape(g_pages * N_HEADS, PAGE)
        r = _dot3(lp, uo_ref[...])
        sufs = [None] * g_pages
        for i in reversed(range(g_pages)):
            sufs[i] = r[i * N_HEADS:(i + 1) * N_HEADS, 0:PAGE] + c_run
            c_run = c_run + r[i * N_HEADS:(i + 1) * N_HEADS, PAGE:2 * PAGE]

        scs = []
        for i in range(0, g_pages, 2):
            kt2 = jnp.concatenate([kbuf[slot, i].astype(BF16), kbuf[slot, i + 1].astype(BF16)], axis=1)
            sc = jnp.dot(qbd_b, kt2, preferred_element_type=F32)
            bias = jnp.concatenate([jnp.concatenate([sufs[i]] * t_new, axis=0),
                                    jnp.concatenate([sufs[i + 1]] * t_new, axis=0)], axis=1)
            scs.append(sc + bias + cqcol)
        m_new = m
        for sc in scs:
            m_new = jnp.maximum(m_new, jnp.max(sc, axis=1, keepdims=True))
        alpha = jnp.exp(m - m_new)
        l = alpha * l
        acc = alpha * acc
        for idx, sc in enumerate(scs):
            i = 2 * idx
            pr = jnp.exp(sc - m_new)
            l = l + jnp.sum(pr, axis=1, keepdims=True)
            vt2 = jnp.concatenate([vbuf[slot, i].astype(BF16), vbuf[slot, i + 1].astype(BF16)], axis=1)
            acc = acc + lax.dot_general(pr.astype(BF16), vt2, nt, preferred_element_type=F32)
        return m_new, l, acc, c_run

    m, l, acc, _ = lax.fori_loop(0, n_groups, group_body, (m, l, acc, jnp.zeros((N_HEADS, PAGE), F32)))

    o = acc / l
    for t in range(t_new):
        blk = jnp.where(own, o[t * N_HEADS:(t + 1) * N_HEADS, :], 0.0)
        o_ref[0, t:t + 1, :] = jnp.sum(blk, axis=0, keepdims=True)


def _decode_attn(page_table, q3, k3, v3, cqcol, cqmat, uo, kc, vc, lc):
    nb, t_new, _ = q3.shape
    g = PAGES_PER_GROUP
    rows = t_new * N_HEADS
    grid_spec = pltpu.PrefetchScalarGridSpec(
        num_scalar_prefetch=1,
        grid=(nb,),
        in_specs=[pl.BlockSpec((1, t_new, ATT_W), lambda i, pt: (i, 0, 0)),
                  pl.BlockSpec((1, t_new, ATT_W), lambda i, pt: (i, 0, 0)),
                  pl.BlockSpec((1, t_new, ATT_W), lambda i, pt: (i, 0, 0)),
                  pl.BlockSpec((1, rows, 1), lambda i, pt: (i, 0, 0)),
                  pl.BlockSpec((1, rows, t_new), lambda i, pt: (i, 0, 0)),
                  pl.BlockSpec(uo.shape, lambda i, pt: (0, 0)),
                  pl.BlockSpec(memory_space=pl.ANY),
                  pl.BlockSpec(memory_space=pl.ANY),
                  pl.BlockSpec(memory_space=pl.ANY)],
        out_specs=pl.BlockSpec((1, t_new, ATT_W), lambda i, pt: (i, 0, 0)),
        scratch_shapes=[pltpu.VMEM((2, g, ATT_W, PAGE), F32),
                        pltpu.VMEM((2, g, ATT_W, PAGE), F32),
                        pltpu.VMEM((2, g, N_HEADS, PAGE), F32),
                        pltpu.SemaphoreType.DMA((3, 2))],
    )
    return pl.pallas_call(
        _decode_kernel, grid_spec=grid_spec,
        out_shape=jax.ShapeDtypeStruct((nb, t_new, ATT_W), F32),
        compiler_params=_cparams(("arbitrary",)),
        name="decode_attention",
    )(page_table, q3, k3, v3, cqcol, cqmat, uo, kc, vc, lc)


def _ffn_sample_kernel(x_ref, att_ref, sgu_ref, mod_ref, wo_ref, g2_ref, wug_ref, wuv_ref, wcg_ref, wcv_ref,
                       bcg_ref, bcv_ref, p1g_ref, p1v_ref, p2g_ref, p2v_ref, wd_ref, gf_ref,
                       y_ref, upg_ref, upv_ref,
                       x1_ref, h2_ref, acc_ref):
    j = pl.program_id(0)
    n = x_ref.shape[0]
    t_new = 4

    @pl.when(j == 0)
    def _():
        mix = (jnp.dot(att_ref[...].astype(BF16), wo_ref[0:ATT_W, :], preferred_element_type=F32)
               + jnp.dot(sgu_ref[...], wo_ref[ATT_W:ATT_W + SGU_W, :], preferred_element_type=F32))
        x1 = x_ref[...] + mod_ref[2] * mix
        x1_ref[...] = x1
        h2_ref[...] = _rms_mod(x1, g2_ref[...], mod_ref[3], mod_ref[4]).astype(BF16)
        acc_ref[...] = jnp.zeros_like(acc_ref)

    h2 = h2_ref[...]
    tpos = lax.broadcasted_iota(jnp.int32, (n, FC), 0) % t_new

    def conv(up, wc_ref, bc_ref, p1_ref, p2_ref):
        s1 = jnp.where(tpos >= 1, pltpu.roll(up, 1, axis=0), p1_ref[...])
        s2 = jnp.where(tpos >= 2, pltpu.roll(up, 2, axis=0), p2_ref[...])
        return bc_ref[...] + s2 * wc_ref[0:1, :] + s1 * wc_ref[1:2, :] + up * wc_ref[2:3, :]

    upg = jnp.dot(h2, wug_ref[...], preferred_element_type=F32)
    upv = jnp.dot(h2, wuv_ref[...], preferred_element_type=F32)
    upg_ref[...] = upg
    upv_ref[...] = upv
    cg = conv(upg, wcg_ref, bcg_ref, p1g_ref, p2g_ref)
    cv = conv(upv, wcv_ref, bcv_ref, p1v_ref, p2v_ref)
    act = (jax.nn.silu(cg) * cv).astype(BF16)
    acc_ref[...] += jnp.dot(act, wd_ref[...], preferred_element_type=F32)

    @pl.when(j == pl.num_programs(0) - 1)
    def _():
        x2 = x1_ref[...] + mod_ref[5] * acc_ref[...]
        y_ref[...] = x2 * lax.rsqrt(jnp.mean(x2 * x2, axis=-1, keepdims=True) + EPS) * gf_ref[...]


def _ffn_sample(x2, att, sgu, mod_rows, wo, g2, wug, wuv, wcg, wcv, bcg, bcv, p1g, p1v, p2g, p2v, wd, gf):
    n, d = x2.shape
    dff = wd.shape[0]
    nf = dff // FC
    full = lambda a: _const_spec(a.shape)
    colc = lambda r: pl.BlockSpec((r, FC), lambda j: (0, j))
    in_specs = [full(x2), full(att), full(sgu), full(mod_rows), full(wo), full(g2),
                colc(d), colc(d), colc(3), colc(3), colc(1), colc(1),
                colc(n), colc(n), colc(n), colc(n),
                pl.BlockSpec((FC, d), lambda j: (j, 0)), full(gf)]
    return pl.pallas_call(
        _ffn_sample_kernel, grid=(nf,),
        in_specs=in_specs,
        out_specs=(pl.BlockSpec((n, d), lambda j: (0, 0)), colc(n), colc(n)),
        out_shape=(jax.ShapeDtypeStruct((n, d), F32), jax.ShapeDtypeStruct((n, dff), F32),
                   jax.ShapeDtypeStruct((n, dff), F32)),
        scratch_shapes=[pltpu.VMEM((n, d), F32), pltpu.VMEM((n, d), BF16), pltpu.VMEM((n, d), F32)],
        compiler_params=_cparams(("arbitrary",)),
        name="sample_ffn",
    )(x2, att, sgu, mod_rows, wo, g2, wug, wuv, wcg, wcv, bcg, bcv, p1g, p1v, p2g, p2v, wd, gf)


def kernel(x_prompt, x_sample, c_prompt, c_sample, cache_k, cache_v, cache_logf, state_conv, page_table,
           w_ada, b_ada, norm1_g, w_in, b_f, ln_v_g, ln_v_b, w_s, b_s, w_o, norm2_g, w_up, w_conv, b_conv,
           w_down, final_g):
    bp, s, d = x_prompt.shape
    bs, t_new, _ = x_sample.shape
    n_s = bs * t_new
    dff = w_down.shape[1]
    n_phys = cache_k.shape[1]
    assert w_ada.shape[0] == 1, "single layer"
    assert s % TM_IN == 0 and s % TQ == 0 and s % TM_FFN == 0 and TM_IN % TQ == 0
    assert dff % FC == 0 and n_s == CHUNK and t_new == 4
    assert page_table.shape[1] % PAGES_PER_GROUP == 0 and PAGES_PER_GROUP % 2 == 0

    wi = w_in[0]
    k0, v0, f0 = ATT_W, 2 * ATT_W, 3 * ATT_W
    u0 = f0 + N_HEADS
    wq = wi[:, 0:k0].astype(BF16)
    wkv = wi[:, k0:f0].astype(BF16)
    wf = wi[:, f0:u0]
    wus = wi[:, u0:].astype(BF16)
    wkvf_t = jnp.concatenate([wi[:, k0:f0], wf], axis=1).T.astype(BF16)
    wf_pad = jnp.pad(wf, ((0, 0), (0, LANES - N_HEADS))).astype(BF16)
    bf_col = b_f[0].reshape(N_HEADS, 1)
    bf_row = jnp.pad(b_f[0].reshape(1, N_HEADS), ((0, 0), (0, LANES - N_HEADS)))
    lng = ln_v_g[0].reshape(1, SGU_W)
    lnb = ln_v_b[0].reshape(1, SGU_W)
    gidx = jnp.arange(SGU_W) // SGU_DIM
    gmat = jnp.where(gidx[:, None] == gidx[None, :], 1.0 / SGU_DIM, 0.0).astype(BF16)
    ws_b = w_s[0]
    bs_full = jnp.repeat(b_s[0].T, SGU_DIM, axis=1)
    wst = jnp.tile(w_s[0][:, :t_new, :t_new], (1, bs, bs))
    bs_full_s = jnp.tile(jnp.repeat(b_s[0][:, :t_new].T, SGU_DIM, axis=1), (bs, 1))
    r_idx = jnp.arange(n_s)
    same = ((r_idx[:, None] // t_new == r_idx[None, :] // t_new)
            & (r_idx[:, None] >= r_idx[None, :])).astype(F32)
    pos = jnp.arange(TM_IN)
    tri = jnp.concatenate([(pos[:, None] <= pos[None, :]).astype(BF16),
                           jnp.ones((TM_IN, LANES), BF16)], axis=1)
    pp = jnp.arange(PAGE)
    uo = jnp.concatenate([(pp[:, None] > pp[None, :]).astype(BF16),
                          jnp.ones((PAGE, PAGE), BF16)], axis=1)
    wo = w_o[0].astype(BF16)
    wug = w_up[0][:, :dff].astype(BF16)
    wuv = w_up[0][:, dff:].astype(BF16)
    wcg, wcv = w_conv[0][:, :dff], w_conv[0][:, dff:]
    bcg, bcv = b_conv[:, :dff], b_conv[:, dff:]
    wd = w_down[0].astype(BF16)
    g1 = norm1_g
    g2 = norm2_g
    gf = final_g.reshape(1, d)

    mod = _ada(jnp.concatenate([c_prompt, c_sample], axis=0), w_ada[0], b_ada)
    mod_p = mod[:bp].reshape(bp, N_MOD, d)
    mod_s = jnp.repeat(mod[bp:].reshape(bs, N_MOD, d), t_new, axis=0).transpose(1, 0, 2)

    q, kt, vt, ktb, vtb, lft, cumt, cum, sgu = _in_prompt(
        x_prompt, mod_p, g1, wq, wkvf_t, wus, bf_col, lng, lnb, gmat, ws_b, bs_full, tri)
    att = _attn_prompt(q, ktb, vtb, cumt, cum)
    y_prompt, conv_p = _ffn_prompt(x_prompt, att, sgu, mod_p, wo, g2, wug, wuv, wcg, wcv, bcg, bcv, wd, gf)

    new_k_p = kt.reshape(1, bp, N_HEADS, HEAD_DIM, s).transpose(0, 1, 4, 2, 3)
    new_v_p = vt.reshape(1, bp, N_HEADS, HEAD_DIM, s).transpose(0, 1, 4, 2, 3)
    new_lf_p = lft.transpose(0, 2, 1)[None]
    new_conv_p = conv_p[:, 6:8, :][None]

    x2 = x_sample.reshape(n_s, d)
    q_s, k_s, v_s, lf_s, cq_s, svn_s, sgu_s = _in_sample(
        x2, mod_s, g1, wq, wkv, wf_pad, wus, bf_row, lng, lnb, gmat, wst, bs_full_s, same)
    cq3 = cq_s[:, :N_HEADS].reshape(bs, t_new, N_HEADS)
    cqcol = cq3.reshape(bs, t_new * N_HEADS, 1)
    cqmat = jnp.tile(cq3.transpose(0, 2, 1), (1, t_new, 1))
    kc = cache_k[0].transpose(0, 2, 3, 1).reshape(n_phys, ATT_W, PAGE)
    vc = cache_v[0].transpose(0, 2, 3, 1).reshape(n_phys, ATT_W, PAGE)
    lc = cache_logf[0].transpose(0, 2, 1)
    att_s = _decode_attn(page_table, q_s.reshape(bs, t_new, ATT_W), k_s.reshape(bs, t_new, ATT_W),
                         v_s.reshape(bs, t_new, ATT_W), cqcol, cqmat, uo, kc, vc, lc)

    st = state_conv[0]
    zero = jnp.zeros((bs, 1, 2 * dff), F32)
    p1 = jnp.concatenate([st[:, 1:2], zero, zero, zero], axis=1).reshape(n_s, 2 * dff)
    p2 = jnp.concatenate([st[:, 0:1], st[:, 1:2], zero, zero], axis=1).reshape(n_s, 2 * dff)
    y_s, upg, upv = _ffn_sample(x2, att_s.reshape(n_s, ATT_W), sgu_s, mod_s, wo, g2, wug, wuv, wcg, wcv,
                                bcg, bcv, p1[:, :dff], p1[:, dff:], p2[:, :dff], p2[:, dff:], wd, gf)
    up = jnp.concatenate([upg, upv], axis=1).reshape(bs, t_new, 2 * dff)

    return (y_prompt, y_s.reshape(bs, t_new, d),
            new_k_p, new_v_p, new_lf_p, new_conv_p,
            k_s.reshape(1, bs, t_new, N_HEADS, HEAD_DIM), v_s.reshape(1, bs, t_new, N_HEADS, HEAD_DIM),
            lf_s[:, :N_HEADS].reshape(1, bs, t_new, N_HEADS),
            svn_s.reshape(1, bs, t_new, N_GROUPS, SGU_DIM),
            up[:, t_new - 2:, :][None])
```

```python
import functools

import jax
import jax.numpy as jnp
from jax import lax
from jax.experimental import pallas as pl
from jax.experimental.pallas import tpu as pltpu

F32 = jnp.float32
BF16 = jnp.bfloat16

N_HEADS = 8
HEAD_DIM = 64
ATT_W = N_HEADS * HEAD_DIM
N_GROUPS = 8
SGU_DIM = 64
SGU_W = N_GROUPS * SGU_DIM
CHUNK = 128
PAGE = 128
N_MOD = 6
EPS = 1e-6
ATT_SCALE = HEAD_DIM ** -0.5
LOG2E = 1.4426950408889634
LANES = 128
PAIR_W = 2 * HEAD_DIM
VMEM_LIMIT = 56 * 1024 * 1024

TM_IN = 512
TQ = 256
ATTN_SCORE_SLOTS = 4
TM_FFN = 512
FC = 256
PAGES_PER_GROUP = 8
DECODE_SLOTS = 3


def _cparams(sem):
    return pltpu.CompilerParams(dimension_semantics=sem, vmem_limit_bytes=VMEM_LIMIT)


def _const_spec(shape):
    nd = len(shape)
    return pl.BlockSpec(shape, lambda *_: (0,) * nd, pipeline_mode=pl.Buffered(1))


def _split3(x):
    hi = x.astype(BF16)
    r1 = x - hi.astype(F32)
    mid = r1.astype(BF16)
    lo = (r1 - mid.astype(F32)).astype(BF16)
    return hi, mid, lo


def _dot3(x, w):
    hi, mid, lo = _split3(x)
    d = functools.partial(jnp.dot, preferred_element_type=F32)
    return d(hi, w) + d(mid, w) + d(lo, w)


def _rms_mod(x, g, shift, scale):
    y = x * lax.rsqrt(jnp.mean(x * x, axis=-1, keepdims=True) + EPS)
    return (y * g) * (1.0 + scale) + shift


def _log_sigmoid(z):
    return jnp.minimum(z, 0.0) - jnp.log1p(jnp.exp(-jnp.abs(z)))


def _group_ln(sv, gmat, ln_g, ln_b):
    mu = jnp.dot(sv.astype(BF16), gmat, preferred_element_type=F32)
    d = sv - mu
    var = jnp.dot((d * d).astype(BF16), gmat, preferred_element_type=F32)
    return d * lax.rsqrt(var + EPS) * ln_g + ln_b


def _sgu_pairs(wm, svn_b, u, bias):
    lane = lax.broadcasted_iota(jnp.int32, (CHUNK, LANES), 1)
    outs = []
    for p in range(N_GROUPS // 2):
        rhs = svn_b[:, p * LANES:(p + 1) * LANES]
        lo = jnp.dot(wm[2 * p], rhs, preferred_element_type=F32)
        hi = jnp.dot(wm[2 * p + 1], rhs, preferred_element_type=F32)
        mixed = jnp.where(lane < SGU_DIM, lo, hi) + bias[:, p * LANES:(p + 1) * LANES]
        outs.append(u[:, p * LANES:(p + 1) * LANES] * mixed)
    return jnp.concatenate(outs, axis=1)


def _ada_kernel(c_ref, w_ref, b_ref, o_ref):
    a = jax.nn.silu(c_ref[...])
    o_ref[...] = jnp.dot(a.astype(BF16), w_ref[...].astype(BF16),
                         preferred_element_type=F32) + b_ref[...]


def _ada(c_all, w_ada, b_ada):
    n, d = c_all.shape
    cols = w_ada.shape[1]
    bn = d
    return pl.pallas_call(
        _ada_kernel,
        grid=(cols // bn,),
        in_specs=[pl.BlockSpec((n, d), lambda j: (0, 0)),
                  pl.BlockSpec((d, bn), lambda j: (0, j)),
                  pl.BlockSpec((1, bn), lambda j: (0, j))],
        out_specs=pl.BlockSpec((n, bn), lambda j: (0, j)),
        out_shape=jax.ShapeDtypeStruct((n, cols), F32),
        compiler_params=_cparams(("arbitrary",)),
        name="ada_ln",
    )(c_all, w_ada, b_ada)


def _in_prompt_kernel(x_ref, mod_ref, g1_ref, wq_ref, wkvf_ref, wus_ref, bf_ref, lng_ref, lnb_ref,
                      gmat_ref, ws_ref, bs_ref, tri_ref,
                      q_ref, kt_ref, vt_ref, ktb_ref, vtb_ref, lft_ref, cumt_ref, cum_ref, sgu_ref,
                      carry_ref):
    s = pl.program_id(1)
    tm = x_ref.shape[1]

    @pl.when(s == 0)
    def _():
        carry_ref[...] = jnp.zeros_like(carry_ref)

    h = _rms_mod(x_ref[0], g1_ref[...], mod_ref[0, 0:1, :], mod_ref[0, 1:2, :]).astype(BF16)

    q = jnp.dot(h, wq_ref[...], preferred_element_type=F32)
    q_ref[0] = (q * (ATT_SCALE * LOG2E)).astype(BF16)

    kvf = lax.dot_general(wkvf_ref[...], h, (((1,), (1,)), ((), ())), preferred_element_type=F32)
    kt = kvf[0:ATT_W]
    vt = kvf[ATT_W:2 * ATT_W]
    kt_ref[0] = kt
    vt_ref[0] = vt
    ktb_ref[0, 0] = kt.astype(BF16)
    vtb_ref[0, 0] = vt.astype(BF16)
    lft = _log_sigmoid(kvf[2 * ATT_W:2 * ATT_W + N_HEADS] + bf_ref[...])
    lft_ref[0] = lft

    r = _dot3(lft, tri_ref[...])
    carry = carry_ref[...]
    cumt = r[:, 0:tm] + jnp.concatenate([carry] * (tm // LANES), axis=1)
    carry_ref[...] = carry + r[:, tm:tm + LANES]
    cumt = cumt * LOG2E
    cumt_ref[0, 0] = cumt
    pad = jnp.zeros((LANES - N_HEADS, LANES), F32)
    for c in range(tm // LANES):
        blk = jnp.concatenate([cumt[:, c * LANES:(c + 1) * LANES], pad], axis=0)
        cum_ref[0, c * LANES:(c + 1) * LANES, :] = blk.T[:, 0:N_HEADS]

    us = jnp.dot(h, wus_ref[...], preferred_element_type=F32)
    u = jax.nn.gelu(us[:, 0:SGU_W])
    sv = jax.nn.gelu(us[:, SGU_W:2 * SGU_W])
    svn = _group_ln(sv, gmat_ref[...], lng_ref[...], lnb_ref[...]).astype(BF16)

    row = lax.broadcasted_iota(jnp.int32, (CHUNK, CHUNK), 0)
    col = lax.broadcasted_iota(jnp.int32, (CHUNK, CHUNK), 1)
    wm = [jnp.where(row >= col, ws_ref[g], 0.0).astype(BF16) for g in range(N_GROUPS)]
    bias = bs_ref[...]
    for c in range(tm // CHUNK):
        sl = slice(c * CHUNK, (c + 1) * CHUNK)
        sgu_ref[0, sl, :] = _sgu_pairs(wm, svn[sl], u[sl], bias).astype(BF16)


def _in_prompt(x, mod3, g1, wq, wkvf, wus, bf_col, lng, lnb, gmat, ws_b, bs_full, tri):
    b, s, d = x.shape
    tm = TM_IN
    ns = s // tm
    grid = (b, ns)
    row_blk = lambda w: pl.BlockSpec((1, tm, w), lambda i, j: (i, j, 0))
    colT_blk = lambda r: pl.BlockSpec((1, r, tm), lambda i, j: (i, 0, j))
    tiledT_blk = lambda r: pl.BlockSpec((1, 1, r, tm), lambda i, j: (i, j, 0, 0))
    out_shape = (
        jax.ShapeDtypeStruct((b, s, ATT_W), BF16),
        jax.ShapeDtypeStruct((b, ATT_W, s), F32),
        jax.ShapeDtypeStruct((b, ATT_W, s), F32),
        jax.ShapeDtypeStruct((b, ns, ATT_W, tm), BF16),
        jax.ShapeDtypeStruct((b, ns, ATT_W, tm), BF16),
        jax.ShapeDtypeStruct((b, N_HEADS, s), F32),
        jax.ShapeDtypeStruct((b, ns, N_HEADS, tm), F32),
        jax.ShapeDtypeStruct((b, s, N_HEADS), F32),
        jax.ShapeDtypeStruct((b, s, SGU_W), BF16),
    )
    out_specs = (row_blk(ATT_W), colT_blk(ATT_W), colT_blk(ATT_W), tiledT_blk(ATT_W), tiledT_blk(ATT_W),
                 colT_blk(N_HEADS), tiledT_blk(N_HEADS), row_blk(N_HEADS), row_blk(SGU_W))
    in_specs = [row_blk(d),
                pl.BlockSpec((1, N_MOD, d), lambda i, j: (i, 0, 0)),
                _const_spec(g1.shape), _const_spec(wq.shape), _const_spec(wkvf.shape), _const_spec(wus.shape),
                _const_spec(bf_col.shape), _const_spec(lng.shape), _const_spec(lnb.shape),
                _const_spec(gmat.shape), _const_spec(ws_b.shape), _const_spec(bs_full.shape),
                _const_spec(tri.shape)]
    return pl.pallas_call(
        _in_prompt_kernel, grid=grid, in_specs=in_specs, out_specs=out_specs, out_shape=out_shape,
        scratch_shapes=[pltpu.VMEM((N_HEADS, LANES), F32)],
        compiler_params=_cparams(("arbitrary", "arbitrary")),
        name="prompt_in_proj",
    )(x, mod3, g1, wq, wkvf, wus, bf_col, lng, lnb, gmat, ws_b, bs_full, tri)


def _attn_kernel(q_ref, kt_ref, vt_ref, cumt_ref, cum_ref, o_ref, qm_ref, s_ref, cq_ref, m_ref, acc_ref):
    qi = pl.program_id(1)
    tq = q_ref.shape[1]
    tk = kt_ref.shape[3]
    qs = qi * tq
    j_diag = (qs + tq - 1) // tk
    n_slots = s_ref.shape[0]
    ahead = n_slots - 1
    lane = lax.broadcasted_iota(jnp.int32, (tq, LANES), 1)
    qpos = qs + lax.broadcasted_iota(jnp.int32, (tq, tk), 0)
    kcol = lax.broadcasted_iota(jnp.int32, (tq, tk), 1)
    ones_half = jnp.ones((HEAD_DIM, tk), BF16)
    nt = (((1,), (1,)), ((), ()))

    for p in range(N_HEADS // 2):
        qp = q_ref[0, :, p * PAIR_W:(p + 1) * PAIR_W].astype(F32)
        qm_ref[2 * p] = jnp.where(lane < HEAD_DIM, qp, 0.0).astype(BF16)
        qm_ref[2 * p + 1] = jnp.where(lane >= HEAD_DIM, qp, 0.0).astype(BF16)
    for h in range(N_HEADS):
        cq_ref[h] = jnp.broadcast_to(cum_ref[0, :, h:h + 1], (tq, LANES))
    m_ref[...] = jnp.full(m_ref.shape, -jnp.inf, F32)
    acc_ref[...] = jnp.zeros(acc_ref.shape, F32)

    def scores(j, h):
        p = h // 2
        kt = kt_ref[0, j, p * PAIR_W:(p + 1) * PAIR_W, :]
        s_ref[h % n_slots] = jnp.dot(qm_ref[h], kt, preferred_element_type=F32)

    def softmax_pv(j, h, masked):
        p, half = divmod(h, 2)
        t = s_ref[h % n_slots] - cumt_ref[0, j, h:h + 1, :]
        if masked:
            t = jnp.where(j * tk + kcol <= qpos, t, -jnp.inf)
        cq = cq_ref[h]
        m = m_ref[h]
        m_new = jnp.maximum(m, jnp.max(t, axis=1, keepdims=True) + cq)
        alpha = jnp.exp2(m - m_new)
        d = cq - m_new
        pr = jnp.exp2(t + jnp.concatenate([d] * (tk // LANES), axis=1)).astype(BF16)
        vt = vt_ref[0, j, p * PAIR_W:(p + 1) * PAIR_W, :]
        if half == 0:
            vaug = jnp.concatenate([vt[0:HEAD_DIM], ones_half], axis=0)
        else:
            vaug = jnp.concatenate([ones_half, vt[HEAD_DIM:PAIR_W]], axis=0)
        acc_ref[h] = alpha * acc_ref[h] + lax.dot_general(pr, vaug, nt, preferred_element_type=F32)
        m_ref[h] = m_new

    def key_tile(j, masked):
        for h in range(N_HEADS):
            softmax_pv(j, h, masked)
            nh = h + ahead
            if nh < N_HEADS:
                scores(j, nh)
            elif not masked:
                scores(j + 1, nh - N_HEADS)

    for h in range(ahead):
        scores(0, h)

    def loop_body(j, carry):
        key_tile(j, False)
        return carry

    lax.fori_loop(0, j_diag, loop_body, 0)
    key_tile(j_diag, True)

    for p in range(N_HEADS // 2):
        a0 = acc_ref[2 * p]
        a1 = acc_ref[2 * p + 1]
        o0 = a0 / pltpu.roll(a0, HEAD_DIM, axis=1)
        o1 = a1 / pltpu.roll(a1, HEAD_DIM, axis=1)
        o_ref[0, :, p * PAIR_W:(p + 1) * PAIR_W] = jnp.where(lane < HEAD_DIM, o0, o1).astype(BF16)


def _attn_prompt(q, ktb, vtb, cumt, cum):
    b, s, _ = q.shape
    nk, tk = ktb.shape[1], ktb.shape[3]
    tq = TQ
    return pl.pallas_call(
        _attn_kernel,
        grid=(b, s // tq),
        in_specs=[pl.BlockSpec((1, tq, ATT_W), lambda i, j: (i, j, 0)),
                  pl.BlockSpec((1, nk, ATT_W, tk), lambda i, j: (i, 0, 0, 0)),
                  pl.BlockSpec((1, nk, ATT_W, tk), lambda i, j: (i, 0, 0, 0)),
                  pl.BlockSpec((1, nk, N_HEADS, tk), lambda i, j: (i, 0, 0, 0)),
                  pl.BlockSpec((1, tq, N_HEADS), lambda i, j: (i, j, 0))],
        out_specs=pl.BlockSpec((1, tq, ATT_W), lambda i, j: (i, j, 0)),
        out_shape=jax.ShapeDtypeStruct((b, s, ATT_W), BF16),
        scratch_shapes=[pltpu.VMEM((N_HEADS, tq, PAIR_W), BF16),
                        pltpu.VMEM((ATTN_SCORE_SLOTS, tq, tk), F32),
                        pltpu.VMEM((N_HEADS, tq, LANES), F32),
                        pltpu.VMEM((N_HEADS, tq, LANES), F32),
                        pltpu.VMEM((N_HEADS, tq, PAIR_W), F32)],
        compiler_params=_cparams(("arbitrary", "arbitrary")),
        name="prompt_attention",
    )(q, ktb, vtb, cumt, cum)


def _ffn_prompt_kernel(x_ref, att_ref, sgu_ref, mod_ref, wo_ref, g2_ref, wug_ref, wuv_ref, wcg_ref, wcv_ref,
                       bcg_ref, bcv_ref, wd_ref, gf_ref,
                       y_ref, conv_ref,
                       fg_ref, fv_ref, act_ref):
    s = pl.program_id(1)
    tm = x_ref.shape[1]
    dff = wd_ref.shape[0]

    @pl.when(s == 0)
    def _():
        fg_ref[0:8, :] = jnp.zeros((8, dff), F32)
        fv_ref[0:8, :] = jnp.zeros((8, dff), F32)

    @pl.when(s > 0)
    def _():
        fg_ref[0:8, :] = fg_ref[tm:tm + 8, :]
        fv_ref[0:8, :] = fv_ref[tm:tm + 8, :]

    mix = (jnp.dot(att_ref[0], wo_ref[0:ATT_W, :], preferred_element_type=F32)
           + jnp.dot(sgu_ref[0], wo_ref[ATT_W:ATT_W + SGU_W, :], preferred_element_type=F32))
    x1 = x_ref[0] + mod_ref[0, 2:3, :] * mix
    h2 = _rms_mod(x1, g2_ref[...], mod_ref[0, 3:4, :], mod_ref[0, 4:5, :]).astype(BF16)

    for c in range(dff // FC):
        cs = slice(c * FC, (c + 1) * FC)
        fg_ref[8:8 + tm, cs] = jnp.dot(h2, wug_ref[:, cs], preferred_element_type=F32)
        fv_ref[8:8 + tm, cs] = jnp.dot(h2, wuv_ref[:, cs], preferred_element_type=F32)
        cg = bcg_ref[:, cs]
        cv = bcv_ref[:, cs]
        for j in range(3):
            cg = cg + fg_ref[6 + j:6 + j + tm, cs] * wcg_ref[j:j + 1, cs]
            cv = cv + fv_ref[6 + j:6 + j + tm, cs] * wcv_ref[j:j + 1, cs]
        act_ref[:, cs] = (jax.nn.silu(cg) * cv).astype(BF16)

    f = jnp.dot(act_ref[...], wd_ref[...], preferred_element_type=F32)
    x2 = x1 + mod_ref[0, 5:6, :] * f
    y_ref[0] = x2 * lax.rsqrt(jnp.mean(x2 * x2, axis=-1, keepdims=True) + EPS) * gf_ref[...]

    @pl.when(s == pl.num_programs(1) - 1)
    def _():
        conv_ref[0, :, 0:dff] = fg_ref[tm:tm + 8, :]
        conv_ref[0, :, dff:2 * dff] = fv_ref[tm:tm + 8, :]


def _ffn_prompt(x, att, sgu, mod3, wo, g2, wug, wuv, wcg, wcv, bcg, bcv, wd, gf):
    b, s, d = x.shape
    tm = TM_FFN
    dff = wd.shape[0]
    row_blk = lambda w: pl.BlockSpec((1, tm, w), lambda i, j: (i, j, 0))
    consts = (wo, g2, wug, wuv, wcg, wcv, bcg, bcv, wd, gf)
    return pl.pallas_call(
        _ffn_prompt_kernel,
        grid=(b, s // tm),
        in_specs=[row_blk(d), row_blk(ATT_W), row_blk(SGU_W),
                  pl.BlockSpec((1, N_MOD, d), lambda i, j: (i, 0, 0))]
                 + [_const_spec(a.shape) for a in consts],
        out_specs=(row_blk(d), pl.BlockSpec((1, 8, 2 * dff), lambda i, j: (i, 0, 0))),
        out_shape=(jax.ShapeDtypeStruct((b, s, d), F32), jax.ShapeDtypeStruct((b, 8, 2 * dff), F32)),
        scratch_shapes=[pltpu.VMEM((tm + 8, dff), F32), pltpu.VMEM((tm + 8, dff), F32),
                        pltpu.VMEM((tm, dff), BF16)],
        compiler_params=_cparams(("arbitrary", "arbitrary")),
        name="prompt_ffn",
    )(x, att, sgu, mod3, *consts)


def _in_sample_kernel(x_ref, mod_ref, g1_ref, wq_ref, wkv_ref, wf_ref, wus_ref, bf_ref, lng_ref, lnb_ref,
                      gmat_ref, wst_ref, bs_ref, same_ref,
                      q_ref, k_ref, v_ref, lf_ref, cq_ref, svn_ref, sgu_ref):
    n = x_ref.shape[0]
    h = _rms_mod(x_ref[...], g1_ref[...], mod_ref[0], mod_ref[1]).astype(BF16)
    q_ref[...] = jnp.dot(h, wq_ref[...], preferred_element_type=F32) * ATT_SCALE
    kv = jnp.dot(h, wkv_ref[...], preferred_element_type=F32)
    k_ref[...] = kv[:, 0:ATT_W]
    v_ref[...] = kv[:, ATT_W:2 * ATT_W]
    lf = _log_sigmoid(jnp.dot(h, wf_ref[...], preferred_element_type=F32) + bf_ref[...])
    lf_ref[...] = lf

    same = same_ref[...]
    hi, mid, lo = _split3(lf)
    tri = same.astype(BF16)
    d = functools.partial(jnp.dot, preferred_element_type=F32)
    cq_ref[...] = d(tri, hi) + d(tri, mid) + d(tri, lo)

    us = jnp.dot(h, wus_ref[...], preferred_element_type=F32)
    u = jax.nn.gelu(us[:, 0:SGU_W])
    sv = jax.nn.gelu(us[:, SGU_W:2 * SGU_W])
    svn = _group_ln(sv, gmat_ref[...], lng_ref[...], lnb_ref[...])
    svn_ref[...] = svn
    wm = [jnp.where(same > 0, wst_ref[g], 0.0).astype(BF16) for g in range(N_GROUPS)]
    sgu_ref[...] = _sgu_pairs(wm, svn.astype(BF16), u, bs_ref[...]).astype(BF16)


def _in_sample(x2, mod_rows, g1, wq, wkv, wf, wus, bf_row, lng, lnb, gmat, wst, bs_full, same):
    n, d = x2.shape
    args = (x2, mod_rows, g1, wq, wkv, wf, wus, bf_row, lng, lnb, gmat, wst, bs_full, same)
    out_shape = (jax.ShapeDtypeStruct((n, ATT_W), F32), jax.ShapeDtypeStruct((n, ATT_W), F32),
                 jax.ShapeDtypeStruct((n, ATT_W), F32), jax.ShapeDtypeStruct((n, LANES), F32),
                 jax.ShapeDtypeStruct((n, LANES), F32), jax.ShapeDtypeStruct((n, SGU_W), F32),
                 jax.ShapeDtypeStruct((n, SGU_W), BF16))
    return pl.pallas_call(
        _in_sample_kernel, grid=(1,),
        in_specs=[_const_spec(a.shape) for a in args],
        out_specs=tuple(pl.BlockSpec(o.shape, lambda i: (0, 0)) for o in out_shape),
        out_shape=out_shape,
        compiler_params=_cparams(("arbitrary",)),
        name="sample_in_proj",
    )(*args)


def _decode_kernel(pt_ref, q_ref, kn_ref, vn_ref, cqcol_ref, cqmat_ref, uo_ref, kc_hbm, vc_hbm, lc_hbm,
                   o_ref, kbuf, vbuf, lbuf, sem):
    b = pl.program_id(0)
    nb = pl.num_programs(0)
    t_new = q_ref.shape[1]
    rows = t_new * N_HEADS
    g_pages = kbuf.shape[1]
    n_groups = pt_ref.shape[1] // g_pages
    total = nb * n_groups
    nt = (((1,), (1,)), ((), ()))

    def copies(n, slot):
        bb = n // n_groups
        lg = n_groups - 1 - (n % n_groups)
        out = []
        for i in range(g_pages):
            phys = pt_ref[bb, lg * g_pages + i]
            out.append(pltpu.make_async_copy(kc_hbm.at[phys], kbuf.at[slot, i], sem.at[0, slot]))
            out.append(pltpu.make_async_copy(vc_hbm.at[phys], vbuf.at[slot, i], sem.at[1, slot]))
            out.append(pltpu.make_async_copy(lc_hbm.at[phys], lbuf.at[slot, i], sem.at[2, slot]))
        return out

    n_slots = kbuf.shape[0]
    ahead = n_slots - 1

    @pl.when(b == 0)
    def _():
        for n0 in range(ahead):
            for c in copies(n0, n0):
                c.start()

    sub = lax.broadcasted_iota(jnp.int32, (N_HEADS, ATT_W), 0)
    lane_head = lax.broadcasted_iota(jnp.int32, (N_HEADS, ATT_W), 1) // HEAD_DIM
    own = sub == lane_head
    q = q_ref[0]
    qbd = jnp.concatenate(
        [jnp.where(own, jnp.broadcast_to(q[t:t + 1, :], (N_HEADS, ATT_W)), 0.0) for t in range(t_new)], axis=0)
    qbd_b = qbd.astype(BF16)
    cqcol = cqcol_ref[0]
    rt = lax.broadcasted_iota(jnp.int32, (rows, 1), 0) // N_HEADS

    kn = kn_ref[0]
    vn = vn_ref[0]
    s_new = []
    for t2 in range(t_new):
        sc = jnp.sum(qbd * kn[t2:t2 + 1, :], axis=1, keepdims=True) + cqcol - cqmat_ref[0, :, t2:t2 + 1]
        s_new.append(jnp.where(rt >= t2, sc, -jnp.inf))
    m = s_new[0]
    for t2 in range(1, t_new):
        m = jnp.maximum(m, s_new[t2])
    l = jnp.zeros((rows, 1), F32)
    acc = jnp.zeros((rows, ATT_W), F32)
    for t2 in range(t_new):
        pr = jnp.exp(s_new[t2] - m)
        l = l + pr
        acc = acc + pr * vn[t2:t2 + 1, :]

    def group_body(g, carry):
        m, l, acc, c_run = carry
        n = b * n_groups + g
        slot = n % n_slots

        @pl.when(n + ahead < total)
        def _():
            for c in copies(n + ahead, (n + ahead) % n_slots):
                c.start()

        for c in copies(n, slot):
            c.wait()

        lp = lbuf[slot].reshape(g_pages * N_HEADS, PAGE)
        r = _dot3(lp, uo_ref[...])
        sufs = [None] * g_pages
        for i in reversed(range(g_pages)):
            sufs[i] = r[i * N_HEADS:(i + 1) * N_HEADS, 0:PAGE] + c_run
            c_run = c_run + r[i * N_HEADS:(i + 1) * N_HEADS, PAGE:2 * PAGE]

        scs = []
        for i in range(0, g_pages, 2):
            kt2 = jnp.concatenate([kbuf[slot, i].astype(BF16), kbuf[slot, i + 1].astype(BF16)], axis=1)
            sc = jnp.dot(qbd_b, kt2, preferred_element_type=F32)
            bias = jnp.concatenate([jnp.concatenate([sufs[i]] * t_new, axis=0),
                                    jnp.concatenate([sufs[i + 1]] * t_new, axis=0)], axis=1)
            scs.append(sc + bias + cqcol)
        m_new = m
        for sc in scs:
            m_new = jnp.maximum(m_new, jnp.max(sc, axis=1, keepdims=True))
        alpha = jnp.exp(m - m_new)
        l = alpha * l
        acc = alpha * acc
        for idx, sc in enumerate(scs):
            i = 2 * idx
            pr = jnp.exp(sc - m_new)
            l = l + jnp.sum(pr, axis=1, keepdims=True)
            vt2 = jnp.concatenate([vbuf[slot, i].astype(BF16), vbuf[slot, i + 1].astype(BF16)], axis=1)
            acc = acc + lax.dot_general(pr.astype(BF16), vt2, nt, preferred_element_type=F32)
        return m_new, l, acc, c_run

    m, l, acc, _ = lax.fori_loop(0, n_groups, group_body, (m, l, acc, jnp.zeros((N_HEADS, PAGE), F32)))

    o = acc / l
    for t in range(t_new):
        blk = jnp.where(own, o[t * N_HEADS:(t + 1) * N_HEADS, :], 0.0)
        o_ref[0, t:t + 1, :] = jnp.sum(blk, axis=0, keepdims=True)


def _decode_attn(page_table, q3, k3, v3, cqcol, cqmat, uo, kc, vc, lc):
    nb, t_new, _ = q3.shape
    g = PAGES_PER_GROUP
    rows = t_new * N_HEADS
    grid_spec = pltpu.PrefetchScalarGridSpec(
        num_scalar_prefetch=1,
        grid=(nb,),
        in_specs=[pl.BlockSpec((1, t_new, ATT_W), lambda i, pt: (i, 0, 0)),
                  pl.BlockSpec((1, t_new, ATT_W), lambda i, pt: (i, 0, 0)),
                  pl.BlockSpec((1, t_new, ATT_W), lambda i, pt: (i, 0, 0)),
                  pl.BlockSpec((1, rows, 1), lambda i, pt: (i, 0, 0)),
                  pl.BlockSpec((1, rows, t_new), lambda i, pt: (i, 0, 0)),
                  pl.BlockSpec(uo.shape, lambda i, pt: (0, 0)),
                  pl.BlockSpec(memory_space=pl.ANY),
                  pl.BlockSpec(memory_space=pl.ANY),
                  pl.BlockSpec(memory_space=pl.ANY)],
        out_specs=pl.BlockSpec((1, t_new, ATT_W), lambda i, pt: (i, 0, 0)),
        scratch_shapes=[pltpu.VMEM((DECODE_SLOTS, g, ATT_W, PAGE), F32),
                        pltpu.VMEM((DECODE_SLOTS, g, ATT_W, PAGE), F32),
                        pltpu.VMEM((DECODE_SLOTS, g, N_HEADS, PAGE), F32),
                        pltpu.SemaphoreType.DMA((3, DECODE_SLOTS))],
    )
    return pl.pallas_call(
        _decode_kernel, grid_spec=grid_spec,
        out_shape=jax.ShapeDtypeStruct((nb, t_new, ATT_W), F32),
        compiler_params=_cparams(("arbitrary",)),
        name="decode_attention",
    )(page_table, q3, k3, v3, cqcol, cqmat, uo, kc, vc, lc)


def _ffn_sample_kernel(x_ref, att_ref, sgu_ref, mod_ref, wo_ref, g2_ref, wug_ref, wuv_ref, wcg_ref, wcv_ref,
                       bcg_ref, bcv_ref, p1g_ref, p1v_ref, p2g_ref, p2v_ref, wd_ref, gf_ref,
                       y_ref, upg_ref, upv_ref,
                       x1_ref, h2_ref, acc_ref):
    j = pl.program_id(0)
    n = x_ref.shape[0]
    t_new = 4

    @pl.when(j == 0)
    def _():
        mix = (jnp.dot(att_ref[...].astype(BF16), wo_ref[0:ATT_W, :], preferred_element_type=F32)
               + jnp.dot(sgu_ref[...], wo_ref[ATT_W:ATT_W + SGU_W, :], preferred_element_type=F32))
        x1 = x_ref[...] + mod_ref[2] * mix
        x1_ref[...] = x1
        h2_ref[...] = _rms_mod(x1, g2_ref[...], mod_ref[3], mod_ref[4]).astype(BF16)
        acc_ref[...] = jnp.zeros_like(acc_ref)

    h2 = h2_ref[...]
    tpos = lax.broadcasted_iota(jnp.int32, (n, FC), 0) % t_new

    def conv(up, wc_ref, bc_ref, p1_ref, p2_ref):
        s1 = jnp.where(tpos >= 1, pltpu.roll(up, 1, axis=0), p1_ref[...])
        s2 = jnp.where(tpos >= 2, pltpu.roll(up, 2, axis=0), p2_ref[...])
        return bc_ref[...] + s2 * wc_ref[0:1, :] + s1 * wc_ref[1:2, :] + up * wc_ref[2:3, :]

    upg = jnp.dot(h2, wug_ref[...], preferred_element_type=F32)
    upv = jnp.dot(h2, wuv_ref[...], preferred_element_type=F32)
    upg_ref[...] = upg
    upv_ref[...] = upv
    cg = conv(upg, wcg_ref, bcg_ref, p1g_ref, p2g_ref)
    cv = conv(upv, wcv_ref, bcv_ref, p1v_ref, p2v_ref)
    act = (jax.nn.silu(cg) * cv).astype(BF16)
    acc_ref[...] += jnp.dot(act, wd_ref[...], preferred_element_type=F32)

    @pl.when(j == pl.num_programs(0) - 1)
    def _():
        x2 = x1_ref[...] + mod_ref[5] * acc_ref[...]
        y_ref[...] = x2 * lax.rsqrt(jnp.mean(x2 * x2, axis=-1, keepdims=True) + EPS) * gf_ref[...]


def _ffn_sample(x2, att, sgu, mod_rows, wo, g2, wug, wuv, wcg, wcv, bcg, bcv, p1g, p1v, p2g, p2v, wd, gf):
    n, d = x2.shape
    dff = wd.shape[0]
    nf = dff // FC
    full = lambda a: _const_spec(a.shape)
    colc = lambda r: pl.BlockSpec((r, FC), lambda j: (0, j))
    in_specs = [full(x2), full(att), full(sgu), full(mod_rows), full(wo), full(g2),
                colc(d), colc(d), colc(3), colc(3), colc(1), colc(1),
                colc(n), colc(n), colc(n), colc(n),
                pl.BlockSpec((FC, d), lambda j: (j, 0)), full(gf)]
    return pl.pallas_call(
        _ffn_sample_kernel, grid=(nf,),
        in_specs=in_specs,
        out_specs=(pl.BlockSpec((n, d), lambda j: (0, 0)), colc(n), colc(n)),
        out_shape=(jax.ShapeDtypeStruct((n, d), F32), jax.ShapeDtypeStruct((n, dff), F32),
                   jax.ShapeDtypeStruct((n, dff), F32)),
        scratch_shapes=[pltpu.VMEM((n, d), F32), pltpu.VMEM((n, d), BF16), pltpu.VMEM((n, d), F32)],
        compiler_params=_cparams(("arbitrary",)),
        name="sample_ffn",
    )(x2, att, sgu, mod_rows, wo, g2, wug, wuv, wcg, wcv, bcg, bcv, p1g, p1v, p2g, p2v, wd, gf)


def kernel(x_prompt, x_sample, c_prompt, c_sample, cache_k, cache_v, cache_logf, state_conv, page_table,
           w_ada, b_ada, norm1_g, w_in, b_f, ln_v_g, ln_v_b, w_s, b_s, w_o, norm2_g, w_up, w_conv, b_conv,
           w_down, final_g):
    bp, s, d = x_prompt.shape
    bs, t_new, _ = x_sample.shape
    n_s = bs * t_new
    dff = w_down.shape[1]
    n_phys = cache_k.shape[1]
    assert w_ada.shape[0] == 1, "single layer"
    assert s % TM_IN == 0 and s % TQ == 0 and s % TM_FFN == 0 and TM_IN % TQ == 0
    assert dff % FC == 0 and n_s == CHUNK and t_new == 4
    assert page_table.shape[1] % PAGES_PER_GROUP == 0 and PAGES_PER_GROUP % 2 == 0

    wi = w_in[0]
    k0, v0, f0 = ATT_W, 2 * ATT_W, 3 * ATT_W
    u0 = f0 + N_HEADS
    wq = wi[:, 0:k0].astype(BF16)
    wkv = wi[:, k0:f0].astype(BF16)
    wf = wi[:, f0:u0]
    wus = wi[:, u0:].astype(BF16)
    wkvf_t = jnp.concatenate([wi[:, k0:f0], wf], axis=1).T.astype(BF16)
    wf_pad = jnp.pad(wf, ((0, 0), (0, LANES - N_HEADS))).astype(BF16)
    bf_col = b_f[0].reshape(N_HEADS, 1)
    bf_row = jnp.pad(b_f[0].reshape(1, N_HEADS), ((0, 0), (0, LANES - N_HEADS)))
    lng = ln_v_g[0].reshape(1, SGU_W)
    lnb = ln_v_b[0].reshape(1, SGU_W)
    gidx = jnp.arange(SGU_W) // SGU_DIM
    gmat = jnp.where(gidx[:, None] == gidx[None, :], 1.0 / SGU_DIM, 0.0).astype(BF16)
    ws_b = w_s[0]
    bs_full = jnp.repeat(b_s[0].T, SGU_DIM, axis=1)
    wst = jnp.tile(w_s[0][:, :t_new, :t_new], (1, bs, bs))
    bs_full_s = jnp.tile(jnp.repeat(b_s[0][:, :t_new].T, SGU_DIM, axis=1), (bs, 1))
    r_idx = jnp.arange(n_s)
    same = ((r_idx[:, None] // t_new == r_idx[None, :] // t_new)
            & (r_idx[:, None] >= r_idx[None, :])).astype(F32)
    pos = jnp.arange(TM_IN)
    tri = jnp.concatenate([(pos[:, None] <= pos[None, :]).astype(BF16),
                           jnp.ones((TM_IN, LANES), BF16)], axis=1)
    pp = jnp.arange(PAGE)
    uo = jnp.concatenate([(pp[:, None] > pp[None, :]).astype(BF16),
                          jnp.ones((PAGE, PAGE), BF16)], axis=1)
    wo = w_o[0].astype(BF16)
    wug = w_up[0][:, :dff].astype(BF16)
    wuv = w_up[0][:, dff:].astype(BF16)
    wcg, wcv = w_conv[0][:, :dff], w_conv[0][:, dff:]
    bcg, bcv = b_conv[:, :dff], b_conv[:, dff:]
    wd = w_down[0].astype(BF16)
    g1 = norm1_g
    g2 = norm2_g
    gf = final_g.reshape(1, d)

    mod = _ada(jnp.concatenate([c_prompt, c_sample], axis=0), w_ada[0], b_ada)
    mod_p = mod[:bp].reshape(bp, N_MOD, d)
    mod_s = jnp.repeat(mod[bp:].reshape(bs, N_MOD, d), t_new, axis=0).transpose(1, 0, 2)

    q, kt, vt, ktb, vtb, lft, cumt, cum, sgu = _in_prompt(
        x_prompt, mod_p, g1, wq, wkvf_t, wus, bf_col, lng, lnb, gmat, ws_b, bs_full, tri)
    att = _attn_prompt(q, ktb, vtb, cumt, cum)
    y_prompt, conv_p = _ffn_prompt(x_prompt, att, sgu, mod_p, wo, g2, wug, wuv, wcg, wcv, bcg, bcv, wd, gf)

    new_k_p = kt.reshape(1, bp, N_HEADS, HEAD_DIM, s).transpose(0, 1, 4, 2, 3)
    new_v_p = vt.reshape(1, bp, N_HEADS, HEAD_DIM, s).transpose(0, 1, 4, 2, 3)
    new_lf_p = lft.transpose(0, 2, 1)[None]
    new_conv_p = conv_p[:, 6:8, :][None]

    x2 = x_sample.reshape(n_s, d)
    q_s, k_s, v_s, lf_s, cq_s, svn_s, sgu_s = _in_sample(
        x2, mod_s, g1, wq, wkv, wf_pad, wus, bf_row, lng, lnb, gmat, wst, bs_full_s, same)
    cq3 = cq_s[:, :N_HEADS].reshape(bs, t_new, N_HEADS)
    cqcol = cq3.reshape(bs, t_new * N_HEADS, 1)
    cqmat = jnp.tile(cq3.transpose(0, 2, 1), (1, t_new, 1))
    kc = cache_k[0].transpose(0, 2, 3, 1).reshape(n_phys, ATT_W, PAGE)
    vc = cache_v[0].transpose(0, 2, 3, 1).reshape(n_phys, ATT_W, PAGE)
    lc = cache_logf[0].transpose(0, 2, 1)
    att_s = _decode_attn(page_table, q_s.reshape(bs, t_new, ATT_W), k_s.reshape(bs, t_new, ATT_W),
                         v_s.reshape(bs, t_new, ATT_W), cqcol, cqmat, uo, kc, vc, lc)

    st = state_conv[0]
    zero = jnp.zeros((bs, 1, 2 * dff), F32)
    p1 = jnp.concatenate([st[:, 1:2], zero, zero, zero], axis=1).reshape(n_s, 2 * dff)
    p2 = jnp.concatenate([st[:, 0:1], st[:, 1:2], zero, zero], axis=1).reshape(n_s, 2 * dff)
    y_s, upg, upv = _ffn_sample(x2, att_s.reshape(n_s, ATT_W), sgu_s, mod_s, wo, g2, wug, wuv, wcg, wcv,
                                bcg, bcv, p1[:, :dff], p1[:, dff:], p2[:, :dff], p2[:, dff:], wd, gf)
    up = jnp.concatenate([upg, upv], axis=1).reshape(bs, t_new, 2 * dff)

    return (y_prompt, y_s.reshape(bs, t_new, d),
            new_k_p, new_v_p, new_lf_p, new_conv_p,
            k_s.reshape(1, bs, t_new, N_HEADS, HEAD_DIM), v_s.reshape(1, bs, t_new, N_HEADS, HEAD_DIM),
            lf_s[:, :N_HEADS].reshape(1, bs, t_new, N_HEADS),
            svn_s.reshape(1, bs, t_new, N_GROUPS, SGU_DIM),
            up[:, t_new - 2:, :][None])
```

```python
import functools

import jax
import jax.numpy as jnp
from jax import lax
from jax.experimental import pallas as pl
from jax.experimental.pallas import tpu as pltpu

F32 = jnp.float32
BF16 = jnp.bfloat16

N_HEADS = 8
HEAD_DIM = 64
ATT_W = N_HEADS * HEAD_DIM
N_GROUPS = 8
SGU_DIM = 64
SGU_W = N_GROUPS * SGU_DIM
CHUNK = 128
PAGE = 128
N_MOD = 6
EPS = 1e-6
ATT_SCALE = HEAD_DIM ** -0.5
LOG2E = 1.4426950408889634
LANES = 128
PAIR_W = 2 * HEAD_DIM
VMEM_LIMIT = 56 * 1024 * 1024

TM_IN = 512
TQ = 256
ATTN_SCORE_SLOTS = 4
TM_FFN = 512
FC = 256
PAGES_PER_GROUP = 6
DECODE_SLOTS = 5


def _cparams(sem):
    return pltpu.CompilerParams(dimension_semantics=sem, vmem_limit_bytes=VMEM_LIMIT)


def _const_spec(shape):
    nd = len(shape)
    return pl.BlockSpec(shape, lambda *_: (0,) * nd, pipeline_mode=pl.Buffered(1))


def _split3(x):
    hi = x.astype(BF16)
    r1 = x - hi.astype(F32)
    mid = r1.astype(BF16)
    lo = (r1 - mid.astype(F32)).astype(BF16)
    return hi, mid, lo


def _dot3(x, w):
    hi, mid, lo = _split3(x)
    d = functools.partial(jnp.dot, preferred_element_type=F32)
    return d(hi, w) + d(mid, w) + d(lo, w)


def _rms_mod(x, g, shift, scale):
    y = x * lax.rsqrt(jnp.mean(x * x, axis=-1, keepdims=True) + EPS)
    return (y * g) * (1.0 + scale) + shift


def _log_sigmoid(z):
    return jnp.minimum(z, 0.0) - jnp.log1p(jnp.exp(-jnp.abs(z)))


def _group_ln(sv, gmat, ln_g, ln_b):
    mu = jnp.dot(sv.astype(BF16), gmat, preferred_element_type=F32)
    d = sv - mu
    var = jnp.dot((d * d).astype(BF16), gmat, preferred_element_type=F32)
    return d * lax.rsqrt(var + EPS) * ln_g + ln_b


def _sgu_pairs(wm, svn_b, u, bias):
    lane = lax.broadcasted_iota(jnp.int32, (CHUNK, LANES), 1)
    outs = []
    for p in range(N_GROUPS // 2):
        rhs = svn_b[:, p * LANES:(p + 1) * LANES]
        lo = jnp.dot(wm[2 * p], rhs, preferred_element_type=F32)
        hi = jnp.dot(wm[2 * p + 1], rhs, preferred_element_type=F32)
        mixed = jnp.where(lane < SGU_DIM, lo, hi) + bias[:, p * LANES:(p + 1) * LANES]
        outs.append(u[:, p * LANES:(p + 1) * LANES] * mixed)
    return jnp.concatenate(outs, axis=1)


def _ada_kernel(c_ref, w_ref, b_ref, o_ref):
    a = jax.nn.silu(c_ref[...])
    o_ref[...] = jnp.dot(a.astype(BF16), w_ref[...].astype(BF16),
                         preferred_element_type=F32) + b_ref[...]


def _ada(c_all, w_ada, b_ada):
    n, d = c_all.shape
    cols = w_ada.shape[1]
    bn = d
    return pl.pallas_call(
        _ada_kernel,
        grid=(cols // bn,),
        in_specs=[pl.BlockSpec((n, d), lambda j: (0, 0)),
                  pl.BlockSpec((d, bn), lambda j: (0, j)),
                  pl.BlockSpec((1, bn), lambda j: (0, j))],
        out_specs=pl.BlockSpec((n, bn), lambda j: (0, j)),
        out_shape=jax.ShapeDtypeStruct((n, cols), F32),
        compiler_params=_cparams(("arbitrary",)),
        name="ada_ln",
    )(c_all, w_ada, b_ada)


def _in_prompt_kernel(x_ref, mod_ref, g1_ref, wq_ref, wkvf_ref, wus_ref, bf_ref, lng_ref, lnb_ref,
                      gmat_ref, ws_ref, bs_ref, tri_ref,
                      q_ref, kt_ref, vt_ref, ktb_ref, vtb_ref, lft_ref, cumt_ref, cum_ref, sgu_ref,
                      carry_ref):
    s = pl.program_id(1)
    tm = x_ref.shape[1]

    @pl.when(s == 0)
    def _():
        carry_ref[...] = jnp.zeros_like(carry_ref)

    h = _rms_mod(x_ref[0], g1_ref[...], mod_ref[0, 0:1, :], mod_ref[0, 1:2, :]).astype(BF16)

    q = jnp.dot(h, wq_ref[...], preferred_element_type=F32)
    q_ref[0] = (q * (ATT_SCALE * LOG2E)).astype(BF16)

    kvf = lax.dot_general(wkvf_ref[...], h, (((1,), (1,)), ((), ())), preferred_element_type=F32)
    kt = kvf[0:ATT_W]
    vt = kvf[ATT_W:2 * ATT_W]
    kt_ref[0] = kt
    vt_ref[0] = vt
    ktb_ref[0, 0] = kt.astype(BF16)
    vtb_ref[0, 0] = vt.astype(BF16)
    lft = _log_sigmoid(kvf[2 * ATT_W:2 * ATT_W + N_HEADS] + bf_ref[...])
    lft_ref[0] = lft

    r = _dot3(lft, tri_ref[...])
    carry = carry_ref[...]
    cumt = r[:, 0:tm] + jnp.concatenate([carry] * (tm // LANES), axis=1)
    carry_ref[...] = carry + r[:, tm:tm + LANES]
    cumt = cumt * LOG2E
    cumt_ref[0, 0] = cumt
    pad = jnp.zeros((LANES - N_HEADS, LANES), F32)
    for c in range(tm // LANES):
        blk = jnp.concatenate([cumt[:, c * LANES:(c + 1) * LANES], pad], axis=0)
        cum_ref[0, c * LANES:(c + 1) * LANES, :] = blk.T[:, 0:N_HEADS]

    us = jnp.dot(h, wus_ref[...], preferred_element_type=F32)
    u = jax.nn.gelu(us[:, 0:SGU_W])
    sv = jax.nn.gelu(us[:, SGU_W:2 * SGU_W])
    svn = _group_ln(sv, gmat_ref[...], lng_ref[...], lnb_ref[...]).astype(BF16)

    row = lax.broadcasted_iota(jnp.int32, (CHUNK, CHUNK), 0)
    col = lax.broadcasted_iota(jnp.int32, (CHUNK, CHUNK), 1)
    wm = [jnp.where(row >= col, ws_ref[g], 0.0).astype(BF16) for g in range(N_GROUPS)]
    bias = bs_ref[...]
    for c in range(tm // CHUNK):
        sl = slice(c * CHUNK, (c + 1) * CHUNK)
        sgu_ref[0, sl, :] = _sgu_pairs(wm, svn[sl], u[sl], bias).astype(BF16)


def _in_prompt(x, mod3, g1, wq, wkvf, wus, bf_col, lng, lnb, gmat, ws_b, bs_full, tri):
    b, s, d = x.shape
    tm = TM_IN
    ns = s // tm
    grid = (b, ns)
    row_blk = lambda w: pl.BlockSpec((1, tm, w), lambda i, j: (i, j, 0))
    colT_blk = lambda r: pl.BlockSpec((1, r, tm), lambda i, j: (i, 0, j))
    tiledT_blk = lambda r: pl.BlockSpec((1, 1, r, tm), lambda i, j: (i, j, 0, 0))
    out_shape = (
        jax.ShapeDtypeStruct((b, s, ATT_W), BF16),
        jax.ShapeDtypeStruct((b, ATT_W, s), F32),
        jax.ShapeDtypeStruct((b, ATT_W, s), F32),
        jax.ShapeDtypeStruct((b, ns, ATT_W, tm), BF16),
        jax.ShapeDtypeStruct((b, ns, ATT_W, tm), BF16),
        jax.ShapeDtypeStruct((b, N_HEADS, s), F32),
        jax.ShapeDtypeStruct((b, ns, N_HEADS, tm), F32),
        jax.ShapeDtypeStruct((b, s, N_HEADS), F32),
        jax.ShapeDtypeStruct((b, s, SGU_W), BF16),
    )
    out_specs = (row_blk(ATT_W), colT_blk(ATT_W), colT_blk(ATT_W), tiledT_blk(ATT_W), tiledT_blk(ATT_W),
                 colT_blk(N_HEADS), tiledT_blk(N_HEADS), row_blk(N_HEADS), row_blk(SGU_W))
    in_specs = [row_blk(d),
                pl.BlockSpec((1, N_MOD, d), lambda i, j: (i, 0, 0)),
                _const_spec(g1.shape), _const_spec(wq.shape), _const_spec(wkvf.shape), _const_spec(wus.shape),
                _const_spec(bf_col.shape), _const_spec(lng.shape), _const_spec(lnb.shape),
                _const_spec(gmat.shape), _const_spec(ws_b.shape), _const_spec(bs_full.shape),
                _const_spec(tri.shape)]
    return pl.pallas_call(
        _in_prompt_kernel, grid=grid, in_specs=in_specs, out_specs=out_specs, out_shape=out_shape,
        scratch_shapes=[pltpu.VMEM((N_HEADS, LANES), F32)],
        compiler_params=_cparams(("arbitrary", "arbitrary")),
        name="prompt_in_proj",
    )(x, mod3, g1, wq, wkvf, wus, bf_col, lng, lnb, gmat, ws_b, bs_full, tri)


def _attn_kernel(q_ref, kt_ref, vt_ref, cumt_ref, cum_ref, o_ref, qm_ref, s_ref, cq_ref, m_ref, acc_ref):
    qi = pl.program_id(1)
    tq = q_ref.shape[1]
    tk = kt_ref.shape[3]
    qs = qi * tq
    j_diag = (qs + tq - 1) // tk
    n_slots = s_ref.shape[0]
    ahead = n_slots - 1
    lane = lax.broadcasted_iota(jnp.int32, (tq, LANES), 1)
    qpos = qs + lax.broadcasted_iota(jnp.int32, (tq, tk), 0)
    kcol = lax.broadcasted_iota(jnp.int32, (tq, tk), 1)
    ones_half = jnp.ones((HEAD_DIM, tk), BF16)
    nt = (((1,), (1,)), ((), ()))

    for p in range(N_HEADS // 2):
        qp = q_ref[0, :, p * PAIR_W:(p + 1) * PAIR_W].astype(F32)
        qm_ref[2 * p] = jnp.where(lane < HEAD_DIM, qp, 0.0).astype(BF16)
        qm_ref[2 * p + 1] = jnp.where(lane >= HEAD_DIM, qp, 0.0).astype(BF16)
    for h in range(N_HEADS):
        cq_ref[h] = jnp.broadcast_to(cum_ref[0, :, h:h + 1], (tq, LANES))
    m_ref[...] = jnp.full(m_ref.shape, -jnp.inf, F32)
    acc_ref[...] = jnp.zeros(acc_ref.shape, F32)

    def scores(j, h):
        p = h // 2
        kt = kt_ref[0, j, p * PAIR_W:(p + 1) * PAIR_W, :]
        s_ref[h % n_slots] = jnp.dot(qm_ref[h], kt, preferred_element_type=F32)

    def softmax_pv(j, h, masked):
        p, half = divmod(h, 2)
        t = s_ref[h % n_slots] - cumt_ref[0, j, h:h + 1, :]
        if masked:
            t = jnp.where(j * tk + kcol <= qpos, t, -jnp.inf)
        cq = cq_ref[h]
        m = m_ref[h]
        m_new = jnp.maximum(m, jnp.max(t, axis=1, keepdims=True) + cq)
        alpha = jnp.exp2(m - m_new)
        d = cq - m_new
        pr = jnp.exp2(t + jnp.concatenate([d] * (tk // LANES), axis=1)).astype(BF16)
        vt = vt_ref[0, j, p * PAIR_W:(p + 1) * PAIR_W, :]
        if half == 0:
            vaug = jnp.concatenate([vt[0:HEAD_DIM], ones_half], axis=0)
        else:
            vaug = jnp.concatenate([ones_half, vt[HEAD_DIM:PAIR_W]], axis=0)
        acc_ref[h] = alpha * acc_ref[h] + lax.dot_general(pr, vaug, nt, preferred_element_type=F32)
        m_ref[h] = m_new

    def key_tile(j, masked):
        for h in range(N_HEADS):
            softmax_pv(j, h, masked)
            nh = h + ahead
            if nh < N_HEADS:
                scores(j, nh)
            elif not masked:
                scores(j + 1, nh - N_HEADS)

    for h in range(ahead):
        scores(0, h)

    def loop_body(j, carry):
        key_tile(j, False)
        return carry

    lax.fori_loop(0, j_diag, loop_body, 0)
    key_tile(j_diag, True)

    for p in range(N_HEADS // 2):
        a0 = acc_ref[2 * p]
        a1 = acc_ref[2 * p + 1]
        o0 = a0 / pltpu.roll(a0, HEAD_DIM, axis=1)
        o1 = a1 / pltpu.roll(a1, HEAD_DIM, axis=1)
        o_ref[0, :, p * PAIR_W:(p + 1) * PAIR_W] = jnp.where(lane < HEAD_DIM, o0, o1).astype(BF16)


def _attn_prompt(q, ktb, vtb, cumt, cum):
    b, s, _ = q.shape
    nk, tk = ktb.shape[1], ktb.shape[3]
    tq = TQ
    return pl.pallas_call(
        _attn_kernel,
        grid=(b, s // tq),
        in_specs=[pl.BlockSpec((1, tq, ATT_W), lambda i, j: (i, j, 0)),
                  pl.BlockSpec((1, nk, ATT_W, tk), lambda i, j: (i, 0, 0, 0)),
                  pl.BlockSpec((1, nk, ATT_W, tk), lambda i, j: (i, 0, 0, 0)),
                  pl.BlockSpec((1, nk, N_HEADS, tk), lambda i, j: (i, 0, 0, 0)),
                  pl.BlockSpec((1, tq, N_HEADS), lambda i, j: (i, j, 0))],
        out_specs=pl.BlockSpec((1, tq, ATT_W), lambda i, j: (i, j, 0)),
        out_shape=jax.ShapeDtypeStruct((b, s, ATT_W), BF16),
        scratch_shapes=[pltpu.VMEM((N_HEADS, tq, PAIR_W), BF16),
                        pltpu.VMEM((ATTN_SCORE_SLOTS, tq, tk), F32),
                        pltpu.VMEM((N_HEADS, tq, LANES), F32),
                        pltpu.VMEM((N_HEADS, tq, LANES), F32),
                        pltpu.VMEM((N_HEADS, tq, PAIR_W), F32)],
        compiler_params=_cparams(("arbitrary", "arbitrary")),
        name="prompt_attention",
    )(q, ktb, vtb, cumt, cum)


def _transpose_rows8(blocks):
    sub = lax.broadcasted_iota(jnp.int32, blocks[0].shape, 0)
    a = list(blocks)
    for s in (4, 2, 1):
        keep = (sub & s) == 0
        b = list(a)
        for i in range(8):
            if i & s == 0:
                lo, hi = a[i], a[i + s]
                b[i] = jnp.where(keep, lo, pltpu.roll(hi, s, axis=0))
                b[i + s] = jnp.where(keep, pltpu.roll(lo, 8 - s, axis=0), hi)
        a = b
    return a


def _interleave_rows(x, nv):
    out = [None] * nv
    for jh in range(nv // 8):
        t = _transpose_rows8([x[r * nv + jh * 8:r * nv + jh * 8 + 8, :] for r in range(8)])
        for jl in range(8):
            out[8 * jh + jl] = t[jl]
    return jnp.concatenate(out, axis=0)


def _deinterleave_rows(y, nv):
    nb = nv // 8
    out = [None] * nv
    for jh in range(nb):
        t = _transpose_rows8([y[(8 * jh + jl) * 8:(8 * jh + jl) * 8 + 8, :] for jl in range(8)])
        for r in range(8):
            out[r * nb + jh] = t[r]
    return jnp.concatenate(out, axis=0)


def _causal_conv3(up, prev_ref, c, wc_ref, bc_ref):
    th = up.shape[0]
    first_row = lax.broadcasted_iota(jnp.int32, (8, up.shape[1]), 0) == 0

    def wrap(prev_blk, cur_blk):
        return jnp.where(first_row, pltpu.roll(prev_blk, 1, axis=0), pltpu.roll(cur_blk, 1, axis=0))

    w1 = wrap(prev_ref[c, 8:16, :], up[th - 8:th])
    w2 = wrap(prev_ref[c, 0:8, :], up[th - 16:th - 8])
    s1 = jnp.concatenate([w1, up[0:th - 8]], axis=0)
    s2 = jnp.concatenate([w2, w1, up[0:th - 16]], axis=0)
    prev_ref[c] = up[th - 16:th]
    return bc_ref[c] + s2 * wc_ref[c, 0:1, :] + s1 * wc_ref[c, 1:2, :] + up * wc_ref[c, 2:3, :]


def _in_sample_kernel(x_ref, mod_ref, g1_ref, wq_ref, wkv_ref, wf_ref, wus_ref, bf_ref, lng_ref, lnb_ref,
                      gmat_ref, wst_ref, bs_ref, same_ref,
                      q_ref, k_ref, v_ref, lf_ref, cq_ref, svn_ref, sgu_ref):
    n = x_ref.shape[0]
    h = _rms_mod(x_ref[...], g1_ref[...], mod_ref[0], mod_ref[1]).astype(BF16)
    q_ref[...] = jnp.dot(h, wq_ref[...], preferred_element_type=F32) * ATT_SCALE
    kv = jnp.dot(h, wkv_ref[...], preferred_element_type=F32)
    k_ref[...] = kv[:, 0:ATT_W]
    v_ref[...] = kv[:, ATT_W:2 * ATT_W]
    lf = _log_sigmoid(jnp.dot(h, wf_ref[...], preferred_element_type=F32) + bf_ref[...])
    lf_ref[...] = lf

    same = same_ref[...]
    hi, mid, lo = _split3(lf)
    tri = same.astype(BF16)
    d = functools.partial(jnp.dot, preferred_element_type=F32)
    cq_ref[...] = d(tri, hi) + d(tri, mid) + d(tri, lo)

    us = jnp.dot(h, wus_ref[...], preferred_element_type=F32)
    u = jax.nn.gelu(us[:, 0:SGU_W])
    sv = jax.nn.gelu(us[:, SGU_W:2 * SGU_W])
    svn = _group_ln(sv, gmat_ref[...], lng_ref[...], lnb_ref[...])
    svn_ref[...] = svn
    wm = [jnp.where(same > 0, wst_ref[g], 0.0).astype(BF16) for g in range(N_GROUPS)]
    sgu_ref[...] = _sgu_pairs(wm, svn.astype(BF16), u, bs_ref[...]).astype(BF16)


def _in_sample(x2, mod_rows, g1, wq, wkv, wf, wus, bf_row, lng, lnb, gmat, wst, bs_full, same):
    n, d = x2.shape
    args = (x2, mod_rows, g1, wq, wkv, wf, wus, bf_row, lng, lnb, gmat, wst, bs_full, same)
    out_shape = (jax.ShapeDtypeStruct((n, ATT_W), F32), jax.ShapeDtypeStruct((n, ATT_W), F32),
                 jax.ShapeDtypeStruct((n, ATT_W), F32), jax.ShapeDtypeStruct((n, LANES), F32),
                 jax.ShapeDtypeStruct((n, LANES), F32), jax.ShapeDtypeStruct((n, SGU_W), F32),
                 jax.ShapeDtypeStruct((n, SGU_W), BF16))
    return pl.pallas_call(
        _in_sample_kernel, grid=(1,),
        in_specs=[_const_spec(a.shape) for a in args],
        out_specs=tuple(pl.BlockSpec(o.shape, lambda i: (0, 0)) for o in out_shape),
        out_shape=out_shape,
        compiler_params=_cparams(("arbitrary",)),
        name="sample_in_proj",
    )(*args)


def _decode_begin(q, kn, vn, cqcol, cqmat_ref, bb):
    t_new = q.shape[0]
    rows = t_new * N_HEADS
    sub = lax.broadcasted_iota(jnp.int32, (N_HEADS, ATT_W), 0)
    lane_head = lax.broadcasted_iota(jnp.int32, (N_HEADS, ATT_W), 1) // HEAD_DIM
    own = sub == lane_head
    qbd = jnp.concatenate(
        [jnp.where(own, jnp.broadcast_to(q[t:t + 1, :], (N_HEADS, ATT_W)), 0.0) for t in range(t_new)], axis=0)
    rt = lax.broadcasted_iota(jnp.int32, (rows, 1), 0) // N_HEADS
    s_new = []
    for t2 in range(t_new):
        sc = jnp.sum(qbd * kn[t2:t2 + 1, :], axis=1, keepdims=True) + cqcol - cqmat_ref[bb, :, t2:t2 + 1]
        s_new.append(jnp.where(rt >= t2, sc, -jnp.inf))
    m = s_new[0]
    for t2 in range(1, t_new):
        m = jnp.maximum(m, s_new[t2])
    l = jnp.zeros((rows, 1), F32)
    acc = jnp.zeros((rows, ATT_W), F32)
    for t2 in range(t_new):
        pr = jnp.exp(s_new[t2] - m)
        l = l + pr
        acc = acc + pr * vn[t2:t2 + 1, :]
    return dict(qbd_b=qbd.astype(BF16), cqcol=cqcol, own=own, t_new=t_new,
                m=m, l=l, acc=acc, c_run=jnp.zeros((N_HEADS, PAGE), F32))


def _decode_scores(st, kbuf, lbuf, slot, uo, valid):
    g_pages = kbuf.shape[1]
    t_new = st["t_new"]
    c_run = st["c_run"]
    lp = lbuf[slot].reshape(g_pages * N_HEADS, PAGE)
    r = _dot3(lp, uo)
    sufs = [None] * g_pages
    for i in reversed(range(g_pages)):
        sufs[i] = jnp.where(valid[i], r[i * N_HEADS:(i + 1) * N_HEADS, 0:PAGE] + c_run, -jnp.inf)
        c_run = c_run + r[i * N_HEADS:(i + 1) * N_HEADS, PAGE:2 * PAGE]
    scs = []
    for i in range(0, g_pages, 2):
        kt2 = jnp.concatenate([kbuf[slot, i].astype(BF16), kbuf[slot, i + 1].astype(BF16)], axis=1)
        sc = jnp.dot(st["qbd_b"], kt2, preferred_element_type=F32)
        bias = jnp.concatenate([jnp.concatenate([sufs[i]] * t_new, axis=0),
                                jnp.concatenate([sufs[i + 1]] * t_new, axis=0)], axis=1)
        scs.append(sc + bias + st["cqcol"])
    return scs, dict(st, c_run=c_run)


def _decode_update(st, scs, vbuf, slot):
    nt = (((1,), (1,)), ((), ()))
    m, l, acc = st["m"], st["l"], st["acc"]
    m_new = m
    for sc in scs:
        m_new = jnp.maximum(m_new, jnp.max(sc, axis=1, keepdims=True))
    alpha = jnp.exp(m - m_new)
    l = alpha * l
    acc = alpha * acc
    for idx, sc in enumerate(scs):
        i = 2 * idx
        pr = jnp.exp(sc - m_new)
        l = l + jnp.sum(pr, axis=1, keepdims=True)
        vt2 = jnp.concatenate([vbuf[slot, i].astype(BF16), vbuf[slot, i + 1].astype(BF16)], axis=1)
        acc = acc + lax.dot_general(pr.astype(BF16), vt2, nt, preferred_element_type=F32)
    return dict(st, m=m_new, l=l, acc=acc)


def _decode_end(st, o_ref, bb):
    o = st["acc"] / st["l"]
    for t in range(st["t_new"]):
        blk = jnp.where(st["own"], o[t * N_HEADS:(t + 1) * N_HEADS, :], 0.0)
        o_ref[bb, t:t + 1, :] = jnp.sum(blk, axis=0, keepdims=True)


def _ffn_decode_kernel(pt_ref, x_ref, att_ref, sgu_ref, mod_ref, wo_ref, g2_ref, wug_ref, wuv_ref, wcg_ref,
                       wcv_ref, bcg_ref, bcv_ref, wd_ref, gf_ref,
                       q_ref, kn_ref, vn_ref, cqcol_ref, cqmat_ref, uo_ref, kc_hbm, vc_hbm, lc_hbm,
                       y_ref, conv_ref, o_ref,
                       x1_ref, h2_ref, f_ref, pg_ref, pv_ref, kbuf, vbuf, lbuf, sem):
    s = pl.program_id(1)
    step = pl.program_id(0) * pl.num_programs(1) + s
    n_steps = pl.num_programs(0) * pl.num_programs(1)
    tm = x_ref.shape[1]
    n_chunks = wd_ref.shape[0]
    th = tm // 2
    nv = th // 8
    halves = (slice(0, th), slice(th, tm))
    n_pages = pt_ref.shape[1]
    g_pages = kbuf.shape[1]
    n_regions = 2 * n_chunks
    n_slots = kbuf.shape[0]
    ahead = n_slots - 2
    total = n_steps * n_regions

    def page_of(r, i):
        return n_pages - g_pages * (r + 1) + i

    def copies(n, slot):
        bb = n // n_regions
        r = n % n_regions
        out = []
        for i in range(g_pages):
            phys = pt_ref[bb, jnp.maximum(page_of(r, i), 0)]
            out.append(pltpu.make_async_copy(kc_hbm.at[phys], kbuf.at[slot, i], sem.at[0, slot]))
            out.append(pltpu.make_async_copy(vc_hbm.at[phys], vbuf.at[slot, i], sem.at[1, slot]))
            out.append(pltpu.make_async_copy(lc_hbm.at[phys], lbuf.at[slot, i], sem.at[2, slot]))
        return out

    @pl.when(step == 0)
    def _():
        for n0 in range(ahead):
            for c in copies(n0, n0):
                c.start()

    @pl.when(s == 0)
    def _():
        pg_ref[...] = jnp.zeros(pg_ref.shape, F32)
        pv_ref[...] = jnp.zeros(pv_ref.shape, F32)

    for i, rows in enumerate(halves):
        mix = (jnp.dot(att_ref[0, rows, :], wo_ref[0:ATT_W, :], preferred_element_type=F32)
               + jnp.dot(sgu_ref[0, rows, :], wo_ref[ATT_W:ATT_W + SGU_W, :], preferred_element_type=F32))
        x1 = _interleave_rows(x_ref[0, rows, :] + mod_ref[0, 2:3, :] * mix, nv)
        x1_ref[rows, :] = x1
        h2_ref[i] = _rms_mod(x1, g2_ref[...], mod_ref[0, 3:4, :], mod_ref[0, 4:5, :]).astype(BF16)
    f_ref[...] = jnp.zeros(f_ref.shape, F32)

    uo = uo_ref[...]
    st0 = _decode_begin(q_ref[0], kn_ref[0], vn_ref[0], cqcol_ref[0], cqmat_ref, 0)

    def trip(c, carry):
        st = dict(st0, m=carry[0], l=carry[1], acc=carry[2], c_run=carry[3])
        regions = [2 * c, 2 * c + 1]
        ns_ = [step * n_regions + r for r in regions]
        slots = [n % n_slots for n in ns_]
        for n in ns_:
            @pl.when(n + ahead < total)
            def _(n=n):
                for cp in copies(n + ahead, (n + ahead) % n_slots):
                    cp.start()
        for n, slot in zip(ns_, slots):
            for cp in copies(n, slot):
                cp.wait()
        ups = [(jnp.dot(h2_ref[i], wug_ref[c], preferred_element_type=F32),
                jnp.dot(h2_ref[i], wuv_ref[c], preferred_element_type=F32)) for i in range(2)]
        scs = []
        for r, slot in zip(regions, slots):
            valid = [page_of(r, k) >= 0 for k in range(g_pages)]
            sc, st = _decode_scores(st, kbuf, lbuf, slot, uo, valid)
            scs.append(sc)

        def down(i):
            cg = _causal_conv3(ups[i][0], pg_ref, c, wcg_ref, bcg_ref)
            cv = _causal_conv3(ups[i][1], pv_ref, c, wcv_ref, bcv_ref)
            act = (jax.nn.silu(cg) * cv).astype(BF16)
            f_ref[i] += jnp.dot(act, wd_ref[c], preferred_element_type=F32)

        down(0)
        for sc, slot in zip(scs, slots):
            st = _decode_update(st, sc, vbuf, slot)
        down(1)
        return st["m"], st["l"], st["acc"], st["c_run"]

    m, l, acc, c_run = lax.fori_loop(0, n_chunks, trip, (st0["m"], st0["l"], st0["acc"], st0["c_run"]))
    _decode_end(dict(st0, m=m, l=l, acc=acc, c_run=c_run), o_ref, 0)

    for i, rows in enumerate(halves):
        x2 = x1_ref[rows, :] + mod_ref[0, 5:6, :] * f_ref[i]
        y = x2 * lax.rsqrt(jnp.mean(x2 * x2, axis=-1, keepdims=True) + EPS) * gf_ref[...]
        y_ref[0, rows, :] = _deinterleave_rows(y, nv)
    fc = pg_ref.shape[2]
    for c in range(n_chunks):
        conv_ref[0, :, c * fc:(c + 1) * fc] = pg_ref[c]
        conv_ref[0, :, (n_chunks + c) * fc:(n_chunks + c + 1) * fc] = pv_ref[c]


def _ffn_prompt_and_decode(page_table, x, att, sgu, mod3, wo, g2, wug, wuv, wcg, wcv, bcg, bcv, wd, gf,
                           q3, k3, v3, cqcol, cqmat, uo, kc, vc, lc):
    b, s, d = x.shape
    tm = TM_FFN
    ns = s // tm
    n_chunks, fc, _ = wd.shape
    dff = n_chunks * fc
    nb, t_new, _ = q3.shape
    g = PAGES_PER_GROUP
    assert nb == b * ns and -(-page_table.shape[1] // g) == 2 * n_chunks and g % 2 == 0
    rows = t_new * N_HEADS
    consts = (wo, g2, wug, wuv, wcg, wcv, bcg, bcv, wd, gf)
    row_blk = lambda w: pl.BlockSpec((1, tm, w), lambda i, j, pt: (i, j, 0))
    smp_blk = lambda r, w: pl.BlockSpec((1, r, w), lambda i, j, pt: (i * ns + j, 0, 0))
    const_blk = lambda a: pl.BlockSpec(a.shape, lambda i, j, pt: (0,) * a.ndim, pipeline_mode=pl.Buffered(1))
    grid_spec = pltpu.PrefetchScalarGridSpec(
        num_scalar_prefetch=1,
        grid=(b, ns),
        in_specs=[row_blk(d), row_blk(ATT_W), row_blk(SGU_W),
                  pl.BlockSpec((1, N_MOD, d), lambda i, j, pt: (i, 0, 0))]
                 + [const_blk(a) for a in consts]
                 + [smp_blk(t_new, ATT_W), smp_blk(t_new, ATT_W), smp_blk(t_new, ATT_W),
                    smp_blk(rows, 1), smp_blk(rows, t_new), const_blk(uo),
                    pl.BlockSpec(memory_space=pl.ANY), pl.BlockSpec(memory_space=pl.ANY),
                    pl.BlockSpec(memory_space=pl.ANY)],
        out_specs=(row_blk(d), pl.BlockSpec((1, 16, 2 * dff), lambda i, j, pt: (i, 0, 0)),
                   smp_blk(t_new, ATT_W)),
        scratch_shapes=[pltpu.VMEM((tm, d), F32),
                        pltpu.VMEM((2, tm // 2, d), BF16),
                        pltpu.VMEM((2, tm // 2, d), F32),
                        pltpu.VMEM((n_chunks, 16, fc), F32),
                        pltpu.VMEM((n_chunks, 16, fc), F32),
                        pltpu.VMEM((DECODE_SLOTS, g, ATT_W, PAGE), F32),
                        pltpu.VMEM((DECODE_SLOTS, g, ATT_W, PAGE), F32),
                        pltpu.VMEM((DECODE_SLOTS, g, N_HEADS, PAGE), F32),
                        pltpu.SemaphoreType.DMA((3, DECODE_SLOTS))],
    )
    return pl.pallas_call(
        _ffn_decode_kernel, grid_spec=grid_spec,
        out_shape=(jax.ShapeDtypeStruct((b, s, d), F32), jax.ShapeDtypeStruct((b, 16, 2 * dff), F32),
                   jax.ShapeDtypeStruct((nb, t_new, ATT_W), F32)),
        compiler_params=_cparams(("arbitrary", "arbitrary")),
        name="prompt_ffn_decode",
    )(page_table, x, att, sgu, mod3, *consts, q3, k3, v3, cqcol, cqmat, uo, kc, vc, lc)


def _ffn_sample_kernel(x_ref, att_ref, sgu_ref, mod_ref, wo_ref, g2_ref, wug_ref, wuv_ref, wcg_ref, wcv_ref,
                       bcg_ref, bcv_ref, p1g_ref, p1v_ref, p2g_ref, p2v_ref, wd_ref, gf_ref,
                       y_ref, upg_ref, upv_ref,
                       x1_ref, h2_ref, acc_ref):
    j = pl.program_id(0)
    n = x_ref.shape[0]
    t_new = 4

    @pl.when(j == 0)
    def _():
        mix = (jnp.dot(att_ref[...].astype(BF16), wo_ref[0:ATT_W, :], preferred_element_type=F32)
               + jnp.dot(sgu_ref[...], wo_ref[ATT_W:ATT_W + SGU_W, :], preferred_element_type=F32))
        x1 = x_ref[...] + mod_ref[2] * mix
        x1_ref[...] = x1
        h2_ref[...] = _rms_mod(x1, g2_ref[...], mod_ref[3], mod_ref[4]).astype(BF16)
        acc_ref[...] = jnp.zeros_like(acc_ref)

    h2 = h2_ref[...]
    tpos = lax.broadcasted_iota(jnp.int32, (n, FC), 0) % t_new

    def conv(up, wc_ref, bc_ref, p1_ref, p2_ref):
        s1 = jnp.where(tpos >= 1, pltpu.roll(up, 1, axis=0), p1_ref[...])
        s2 = jnp.where(tpos >= 2, pltpu.roll(up, 2, axis=0), p2_ref[...])
        return bc_ref[0] + s2 * wc_ref[0, 0:1, :] + s1 * wc_ref[0, 1:2, :] + up * wc_ref[0, 2:3, :]

    upg = jnp.dot(h2, wug_ref[0], preferred_element_type=F32)
    upv = jnp.dot(h2, wuv_ref[0], preferred_element_type=F32)
    upg_ref[...] = upg
    upv_ref[...] = upv
    cg = conv(upg, wcg_ref, bcg_ref, p1g_ref, p2g_ref)
    cv = conv(upv, wcv_ref, bcv_ref, p1v_ref, p2v_ref)
    act = (jax.nn.silu(cg) * cv).astype(BF16)
    acc_ref[...] += jnp.dot(act, wd_ref[0], preferred_element_type=F32)

    @pl.when(j == pl.num_programs(0) - 1)
    def _():
        x2 = x1_ref[...] + mod_ref[5] * acc_ref[...]
        y_ref[...] = x2 * lax.rsqrt(jnp.mean(x2 * x2, axis=-1, keepdims=True) + EPS) * gf_ref[...]


def _ffn_sample(x2, att, sgu, mod_rows, wo, g2, wug, wuv, wcg, wcv, bcg, bcv, p1g, p1v, p2g, p2v, wd, gf):
    n, d = x2.shape
    nf = wd.shape[0]
    dff = nf * FC
    full = lambda a: _const_spec(a.shape)
    colc = lambda r: pl.BlockSpec((r, FC), lambda j: (0, j))
    chunk = lambda a: pl.BlockSpec((1,) + a.shape[1:], lambda j: (j, 0, 0))
    in_specs = [full(x2), full(att), full(sgu), full(mod_rows), full(wo), full(g2),
                chunk(wug), chunk(wuv), chunk(wcg), chunk(wcv), chunk(bcg), chunk(bcv),
                colc(n), colc(n), colc(n), colc(n),
                chunk(wd), full(gf)]
    return pl.pallas_call(
        _ffn_sample_kernel, grid=(nf,),
        in_specs=in_specs,
        out_specs=(pl.BlockSpec((n, d), lambda j: (0, 0)), colc(n), colc(n)),
        out_shape=(jax.ShapeDtypeStruct((n, d), F32), jax.ShapeDtypeStruct((n, dff), F32),
                   jax.ShapeDtypeStruct((n, dff), F32)),
        scratch_shapes=[pltpu.VMEM((n, d), F32), pltpu.VMEM((n, d), BF16), pltpu.VMEM((n, d), F32)],
        compiler_params=_cparams(("arbitrary",)),
        name="sample_ffn",
    )(x2, att, sgu, mod_rows, wo, g2, wug, wuv, wcg, wcv, bcg, bcv, p1g, p1v, p2g, p2v, wd, gf)


def kernel(x_prompt, x_sample, c_prompt, c_sample, cache_k, cache_v, cache_logf, state_conv, page_table,
           w_ada, b_ada, norm1_g, w_in, b_f, ln_v_g, ln_v_b, w_s, b_s, w_o, norm2_g, w_up, w_conv, b_conv,
           w_down, final_g):
    bp, s, d = x_prompt.shape
    bs, t_new, _ = x_sample.shape
    n_s = bs * t_new
    dff = w_down.shape[1]
    n_phys = cache_k.shape[1]
    assert w_ada.shape[0] == 1, "single layer"
    assert s % TM_IN == 0 and s % TQ == 0 and s % TM_FFN == 0 and TM_IN % TQ == 0
    assert dff % FC == 0 and n_s == CHUNK and t_new == 4
    assert bs == bp * (s // TM_FFN), "one decode sample rides along with each FFN grid step"

    wi = w_in[0]
    k0, v0, f0 = ATT_W, 2 * ATT_W, 3 * ATT_W
    u0 = f0 + N_HEADS
    wq = wi[:, 0:k0].astype(BF16)
    wkv = wi[:, k0:f0].astype(BF16)
    wf = wi[:, f0:u0]
    wus = wi[:, u0:].astype(BF16)
    wkvf_t = jnp.concatenate([wi[:, k0:f0], wf], axis=1).T.astype(BF16)
    wf_pad = jnp.pad(wf, ((0, 0), (0, LANES - N_HEADS))).astype(BF16)
    bf_col = b_f[0].reshape(N_HEADS, 1)
    bf_row = jnp.pad(b_f[0].reshape(1, N_HEADS), ((0, 0), (0, LANES - N_HEADS)))
    lng = ln_v_g[0].reshape(1, SGU_W)
    lnb = ln_v_b[0].reshape(1, SGU_W)
    gidx = jnp.arange(SGU_W) // SGU_DIM
    gmat = jnp.where(gidx[:, None] == gidx[None, :], 1.0 / SGU_DIM, 0.0).astype(BF16)
    ws_b = w_s[0]
    bs_full = jnp.repeat(b_s[0].T, SGU_DIM, axis=1)
    wst = jnp.tile(w_s[0][:, :t_new, :t_new], (1, bs, bs))
    bs_full_s = jnp.tile(jnp.repeat(b_s[0][:, :t_new].T, SGU_DIM, axis=1), (bs, 1))
    r_idx = jnp.arange(n_s)
    same = ((r_idx[:, None] // t_new == r_idx[None, :] // t_new)
            & (r_idx[:, None] >= r_idx[None, :])).astype(F32)
    pos = jnp.arange(TM_IN)
    tri = jnp.concatenate([(pos[:, None] <= pos[None, :]).astype(BF16),
                           jnp.ones((TM_IN, LANES), BF16)], axis=1)
    pp = jnp.arange(PAGE)
    uo = jnp.concatenate([(pp[:, None] > pp[None, :]).astype(BF16),
                          jnp.ones((PAGE, PAGE), BF16)], axis=1)
    wo = w_o[0].astype(BF16)
    nfc = dff // FC
    chunked = lambda a: a.reshape(a.shape[0], nfc, FC).transpose(1, 0, 2)
    wug = chunked(w_up[0][:, :dff]).astype(BF16)
    wuv = chunked(w_up[0][:, dff:]).astype(BF16)
    wcg, wcv = chunked(w_conv[0][:, :dff]), chunked(w_conv[0][:, dff:])
    bcg, bcv = chunked(b_conv[:, :dff]), chunked(b_conv[:, dff:])
    wd = w_down[0].astype(BF16).reshape(nfc, FC, d)
    g1 = norm1_g
    g2 = norm2_g
    gf = final_g.reshape(1, d)

    mod = _ada(jnp.concatenate([c_prompt, c_sample], axis=0), w_ada[0], b_ada)
    mod_p = mod[:bp].reshape(bp, N_MOD, d)
    mod_s = jnp.repeat(mod[bp:].reshape(bs, N_MOD, d), t_new, axis=0).transpose(1, 0, 2)

    q, kt, vt, ktb, vtb, lft, cumt, cum, sgu = _in_prompt(
        x_prompt, mod_p, g1, wq, wkvf_t, wus, bf_col, lng, lnb, gmat, ws_b, bs_full, tri)
    att = _attn_prompt(q, ktb, vtb, cumt, cum)
    new_k_p = kt.reshape(1, bp, N_HEADS, HEAD_DIM, s).transpose(0, 1, 4, 2, 3)
    new_v_p = vt.reshape(1, bp, N_HEADS, HEAD_DIM, s).transpose(0, 1, 4, 2, 3)
    new_lf_p = lft.transpose(0, 2, 1)[None]

    x2 = x_sample.reshape(n_s, d)
    q_s, k_s, v_s, lf_s, cq_s, svn_s, sgu_s = _in_sample(
        x2, mod_s, g1, wq, wkv, wf_pad, wus, bf_row, lng, lnb, gmat, wst, bs_full_s, same)
    cq3 = cq_s[:, :N_HEADS].reshape(bs, t_new, N_HEADS)
    cqcol = cq3.reshape(bs, t_new * N_HEADS, 1)
    cqmat = jnp.tile(cq3.transpose(0, 2, 1), (1, t_new, 1))
    kc = cache_k[0].transpose(0, 2, 3, 1).reshape(n_phys, ATT_W, PAGE)
    vc = cache_v[0].transpose(0, 2, 3, 1).reshape(n_phys, ATT_W, PAGE)
    lc = cache_logf[0].transpose(0, 2, 1)
    y_prompt, conv_p, att_s = _ffn_prompt_and_decode(
        page_table, x_prompt, att, sgu, mod_p, wo, g2, wug, wuv, wcg, wcv, bcg, bcv, wd, gf,
        q_s.reshape(bs, t_new, ATT_W), k_s.reshape(bs, t_new, ATT_W), v_s.reshape(bs, t_new, ATT_W),
        cqcol, cqmat, uo, kc, vc, lc)
    new_conv_p = conv_p[:, 7::8, :][None]

    st = state_conv[0]
    zero = jnp.zeros((bs, 1, 2 * dff), F32)
    p1 = jnp.concatenate([st[:, 1:2], zero, zero, zero], axis=1).reshape(n_s, 2 * dff)
    p2 = jnp.concatenate([st[:, 0:1], st[:, 1:2], zero, zero], axis=1).reshape(n_s, 2 * dff)
    y_s, upg, upv = _ffn_sample(x2, att_s.reshape(n_s, ATT_W), sgu_s, mod_s, wo, g2, wug, wuv, wcg, wcv,
                                bcg, bcv, p1[:, :dff], p1[:, dff:], p2[:, :dff], p2[:, dff:], wd, gf)
    up = jnp.concatenate([upg, upv], axis=1).reshape(bs, t_new, 2 * dff)

    return (y_prompt, y_s.reshape(bs, t_new, d),
            new_k_p, new_v_p, new_lf_p, new_conv_p,
            k_s.reshape(1, bs, t_new, N_HEADS, HEAD_DIM), v_s.reshape(1, bs, t_new, N_HEADS, HEAD_DIM),
            lf_s[:, :N_HEADS].reshape(1, bs, t_new, N_HEADS),
            svn_s.reshape(1, bs, t_new, N_GROUPS, SGU_DIM),
            up[:, t_new - 2:, :][None])
```

```python
import functools

import jax
import jax.numpy as jnp
from jax import lax
from jax.experimental import pallas as pl
from jax.experimental.pallas import tpu as pltpu

F32 = jnp.float32
BF16 = jnp.bfloat16

N_HEADS = 8
HEAD_DIM = 64
ATT_W = N_HEADS * HEAD_DIM
N_GROUPS = 8
SGU_DIM = 64
SGU_W = N_GROUPS * SGU_DIM
CHUNK = 128
PAGE = 128
N_MOD = 6
EPS = 1e-6
ATT_SCALE = HEAD_DIM ** -0.5
LOG2E = 1.4426950408889634
LANES = 128
PAIR_W = 2 * HEAD_DIM
VMEM_LIMIT = 58 * 1024 * 1024

TM_IN = 512
TQ = 512
ATTN_SCORE_SLOTS = 4
TM_FFN = 512
FC = 256
PAGES_PER_GROUP = 6
DECODE_SLOTS = 6


def _cparams(sem):
    return pltpu.CompilerParams(dimension_semantics=sem, vmem_limit_bytes=VMEM_LIMIT)


def _const_spec(shape):
    nd = len(shape)
    return pl.BlockSpec(shape, lambda *_: (0,) * nd, pipeline_mode=pl.Buffered(1))


def _split3(x):
    hi = x.astype(BF16)
    r1 = x - hi.astype(F32)
    mid = r1.astype(BF16)
    lo = (r1 - mid.astype(F32)).astype(BF16)
    return hi, mid, lo


def _dot3(x, w):
    hi, mid, lo = _split3(x)
    d = functools.partial(jnp.dot, preferred_element_type=F32)
    return d(hi, w) + d(mid, w) + d(lo, w)


def _rms_mod(x, g, shift, scale):
    y = x * lax.rsqrt(jnp.mean(x * x, axis=-1, keepdims=True) + EPS)
    return (y * g) * (1.0 + scale) + shift


def _log_sigmoid(z):
    return jnp.minimum(z, 0.0) - jnp.log1p(jnp.exp(-jnp.abs(z)))


def _group_ln(sv, gmat, ln_g, ln_b):
    mu = jnp.dot(sv.astype(BF16), gmat, preferred_element_type=F32)
    d = sv - mu
    var = jnp.dot((d * d).astype(BF16), gmat, preferred_element_type=F32)
    return d * lax.rsqrt(var + EPS) * ln_g + ln_b


def _sgu_pairs(wm, svn_b, u, bias):
    lane = lax.broadcasted_iota(jnp.int32, (CHUNK, LANES), 1)
    outs = []
    for p in range(N_GROUPS // 2):
        rhs = svn_b[:, p * LANES:(p + 1) * LANES]
        lo = jnp.dot(wm[2 * p], rhs, preferred_element_type=F32)
        hi = jnp.dot(wm[2 * p + 1], rhs, preferred_element_type=F32)
        mixed = jnp.where(lane < SGU_DIM, lo, hi) + bias[:, p * LANES:(p + 1) * LANES]
        outs.append(u[:, p * LANES:(p + 1) * LANES] * mixed)
    return jnp.concatenate(outs, axis=1)


def _ada_kernel(c_ref, w_ref, b_ref, o_ref):
    a = jax.nn.silu(c_ref[...])
    o_ref[...] = jnp.dot(a.astype(BF16), w_ref[...].astype(BF16),
                         preferred_element_type=F32) + b_ref[...]


def _ada(c_all, w_ada, b_ada):
    n, d = c_all.shape
    cols = w_ada.shape[1]
    bn = d
    return pl.pallas_call(
        _ada_kernel,
        grid=(cols // bn,),
        in_specs=[pl.BlockSpec((n, d), lambda j: (0, 0)),
                  pl.BlockSpec((d, bn), lambda j: (0, j)),
                  pl.BlockSpec((1, bn), lambda j: (0, j))],
        out_specs=pl.BlockSpec((n, bn), lambda j: (0, j)),
        out_shape=jax.ShapeDtypeStruct((n, cols), F32),
        compiler_params=_cparams(("arbitrary",)),
        name="ada_ln",
    )(c_all, w_ada, b_ada)


def _in_prompt_kernel(x_ref, mod_ref, g1_ref, wq_ref, wkvf_ref, wus_ref, bf_ref, lng_ref, lnb_ref,
                      gmat_ref, ws_ref, bs_ref, tri_ref,
                      q_ref, kt_ref, vt_ref, ktb_ref, vtb_ref, lft_ref, cumt_ref, cum_ref, sgu_ref,
                      carry_ref):
    s = pl.program_id(1)
    tm = x_ref.shape[1]

    @pl.when(s == 0)
    def _():
        carry_ref[...] = jnp.zeros_like(carry_ref)

    h = _rms_mod(x_ref[0], g1_ref[...], mod_ref[0, 0:1, :], mod_ref[0, 1:2, :]).astype(BF16)

    q = jnp.dot(h, wq_ref[...], preferred_element_type=F32)
    q_ref[0] = (q * (ATT_SCALE * LOG2E)).astype(BF16)

    kvf = lax.dot_general(wkvf_ref[...], h, (((1,), (1,)), ((), ())), preferred_element_type=F32)
    kt = kvf[0:ATT_W]
    vt = kvf[ATT_W:2 * ATT_W]
    kt_ref[0] = kt
    vt_ref[0] = vt
    ktb_ref[0, 0] = kt.astype(BF16)
    vtb_ref[0, 0] = vt.astype(BF16)
    lft = _log_sigmoid(kvf[2 * ATT_W:2 * ATT_W + N_HEADS] + bf_ref[...])
    lft_ref[0] = lft

    r = _dot3(lft, tri_ref[...])
    carry = carry_ref[...]
    cumt = r[:, 0:tm] + jnp.concatenate([carry] * (tm // LANES), axis=1)
    carry_ref[...] = carry + r[:, tm:tm + LANES]
    cumt = cumt * LOG2E
    cumt_ref[0, 0] = cumt
    pad = jnp.zeros((LANES - N_HEADS, LANES), F32)
    for c in range(tm // LANES):
        blk = jnp.concatenate([cumt[:, c * LANES:(c + 1) * LANES], pad], axis=0)
        cum_ref[0, c * LANES:(c + 1) * LANES, :] = blk.T[:, 0:N_HEADS]

    us = jnp.dot(h, wus_ref[...], preferred_element_type=F32)
    u = jax.nn.gelu(us[:, 0:SGU_W])
    sv = jax.nn.gelu(us[:, SGU_W:2 * SGU_W])
    svn = _group_ln(sv, gmat_ref[...], lng_ref[...], lnb_ref[...]).astype(BF16)

    row = lax.broadcasted_iota(jnp.int32, (CHUNK, CHUNK), 0)
    col = lax.broadcasted_iota(jnp.int32, (CHUNK, CHUNK), 1)
    wm = [jnp.where(row >= col, ws_ref[g], 0.0).astype(BF16) for g in range(N_GROUPS)]
    bias = bs_ref[...]
    for c in range(tm // CHUNK):
        sl = slice(c * CHUNK, (c + 1) * CHUNK)
        sgu_ref[0, sl, :] = _sgu_pairs(wm, svn[sl], u[sl], bias).astype(BF16)


def _in_prompt(x, mod3, g1, wq, wkvf, wus, bf_col, lng, lnb, gmat, ws_b, bs_full, tri):
    b, s, d = x.shape
    tm = TM_IN
    ns = s // tm
    grid = (b, ns)
    row_blk = lambda w: pl.BlockSpec((1, tm, w), lambda i, j: (i, j, 0))
    colT_blk = lambda r: pl.BlockSpec((1, r, tm), lambda i, j: (i, 0, j))
    tiledT_blk = lambda r: pl.BlockSpec((1, 1, r, tm), lambda i, j: (i, j, 0, 0))
    out_shape = (
        jax.ShapeDtypeStruct((b, s, ATT_W), BF16),
        jax.ShapeDtypeStruct((b, ATT_W, s), F32),
        jax.ShapeDtypeStruct((b, ATT_W, s), F32),
        jax.ShapeDtypeStruct((b, ns, ATT_W, tm), BF16),
        jax.ShapeDtypeStruct((b, ns, ATT_W, tm), BF16),
        jax.ShapeDtypeStruct((b, N_HEADS, s), F32),
        jax.ShapeDtypeStruct((b, ns, N_HEADS, tm), F32),
        jax.ShapeDtypeStruct((b, s, N_HEADS), F32),
        jax.ShapeDtypeStruct((b, s, SGU_W), BF16),
    )
    out_specs = (row_blk(ATT_W), colT_blk(ATT_W), colT_blk(ATT_W), tiledT_blk(ATT_W), tiledT_blk(ATT_W),
                 colT_blk(N_HEADS), tiledT_blk(N_HEADS), row_blk(N_HEADS), row_blk(SGU_W))
    in_specs = [row_blk(d),
                pl.BlockSpec((1, N_MOD, d), lambda i, j: (i, 0, 0)),
                _const_spec(g1.shape), _const_spec(wq.shape), _const_spec(wkvf.shape), _const_spec(wus.shape),
                _const_spec(bf_col.shape), _const_spec(lng.shape), _const_spec(lnb.shape),
                _const_spec(gmat.shape), _const_spec(ws_b.shape), _const_spec(bs_full.shape),
                _const_spec(tri.shape)]
    return pl.pallas_call(
        _in_prompt_kernel, grid=grid, in_specs=in_specs, out_specs=out_specs, out_shape=out_shape,
        scratch_shapes=[pltpu.VMEM((N_HEADS, LANES), F32)],
        compiler_params=_cparams(("arbitrary", "arbitrary")),
        name="prompt_in_proj",
    )(x, mod3, g1, wq, wkvf, wus, bf_col, lng, lnb, gmat, ws_b, bs_full, tri)


def _attn_kernel(q_ref, kt_ref, vt_ref, cumt_ref, cum_ref, o_ref, qm_ref, s_ref, cq_ref, m_ref, acc_ref):
    qi = pl.program_id(1)
    tq = q_ref.shape[1]
    tk = kt_ref.shape[3]
    qs = qi * tq
    j_diag = (qs + tq - 1) // tk
    n_slots = s_ref.shape[0]
    ahead = n_slots - 1
    lane = lax.broadcasted_iota(jnp.int32, (tq, LANES), 1)
    qpos = qs + lax.broadcasted_iota(jnp.int32, (tq, tk), 0)
    kcol = lax.broadcasted_iota(jnp.int32, (tq, tk), 1)
    ones_half = jnp.ones((HEAD_DIM, tk), BF16)
    nt = (((1,), (1,)), ((), ()))

    for p in range(N_HEADS // 2):
        qp = q_ref[0, :, p * PAIR_W:(p + 1) * PAIR_W].astype(F32)
        qm_ref[2 * p] = jnp.where(lane < HEAD_DIM, qp, 0.0).astype(BF16)
        qm_ref[2 * p + 1] = jnp.where(lane >= HEAD_DIM, qp, 0.0).astype(BF16)
    for h in range(N_HEADS):
        cq_ref[h] = jnp.broadcast_to(cum_ref[0, :, h:h + 1], (tq, LANES))
    m_ref[...] = jnp.full(m_ref.shape, -jnp.inf, F32)
    acc_ref[...] = jnp.zeros(acc_ref.shape, F32)

    def scores(j, h):
        p = h // 2
        kt = kt_ref[0, j, p * PAIR_W:(p + 1) * PAIR_W, :]
        s_ref[h % n_slots] = jnp.dot(qm_ref[h], kt, preferred_element_type=F32)

    def softmax_pv(j, h, masked):
        p, half = divmod(h, 2)
        t = s_ref[h % n_slots] - cumt_ref[0, j, h:h + 1, :]
        if masked:
            t = jnp.where(j * tk + kcol <= qpos, t, -jnp.inf)
        cq = cq_ref[h]
        m = m_ref[h]
        m_new = jnp.maximum(m, jnp.max(t, axis=1, keepdims=True) + cq)
        alpha = jnp.exp2(m - m_new)
        d = cq - m_new
        pr = jnp.exp2(t + jnp.concatenate([d] * (tk // LANES), axis=1)).astype(BF16)
        vt = vt_ref[0, j, p * PAIR_W:(p + 1) * PAIR_W, :]
        if half == 0:
            vaug = jnp.concatenate([vt[0:HEAD_DIM], ones_half], axis=0)
        else:
            vaug = jnp.concatenate([ones_half, vt[HEAD_DIM:PAIR_W]], axis=0)
        acc_ref[h] = alpha * acc_ref[h] + lax.dot_general(pr, vaug, nt, preferred_element_type=F32)
        m_ref[h] = m_new

    def key_tile(j, masked):
        for h in range(N_HEADS):
            softmax_pv(j, h, masked)
            nh = h + ahead
            if nh < N_HEADS:
                scores(j, nh)
            elif not masked:
                scores(j + 1, nh - N_HEADS)

    for h in range(ahead):
        scores(0, h)

    def loop_body(j, carry):
        key_tile(j, False)
        return carry

    lax.fori_loop(0, j_diag, loop_body, 0)
    key_tile(j_diag, True)

    for p in range(N_HEADS // 2):
        a0 = acc_ref[2 * p]
        a1 = acc_ref[2 * p + 1]
        o0 = a0 / pltpu.roll(a0, HEAD_DIM, axis=1)
        o1 = a1 / pltpu.roll(a1, HEAD_DIM, axis=1)
        o_ref[0, :, p * PAIR_W:(p + 1) * PAIR_W] = jnp.where(lane < HEAD_DIM, o0, o1).astype(BF16)


def _attn_prompt(q, ktb, vtb, cumt, cum):
    b, s, _ = q.shape
    nk, tk = ktb.shape[1], ktb.shape[3]
    tq = TQ
    return pl.pallas_call(
        _attn_kernel,
        grid=(b, s // tq),
        in_specs=[pl.BlockSpec((1, tq, ATT_W), lambda i, j: (i, j, 0)),
                  pl.BlockSpec((1, nk, ATT_W, tk), lambda i, j: (i, 0, 0, 0)),
                  pl.BlockSpec((1, nk, ATT_W, tk), lambda i, j: (i, 0, 0, 0)),
                  pl.BlockSpec((1, nk, N_HEADS, tk), lambda i, j: (i, 0, 0, 0)),
                  pl.BlockSpec((1, tq, N_HEADS), lambda i, j: (i, j, 0))],
        out_specs=pl.BlockSpec((1, tq, ATT_W), lambda i, j: (i, j, 0)),
        out_shape=jax.ShapeDtypeStruct((b, s, ATT_W), BF16),
        scratch_shapes=[pltpu.VMEM((N_HEADS, tq, PAIR_W), BF16),
                        pltpu.VMEM((ATTN_SCORE_SLOTS, tq, tk), F32),
                        pltpu.VMEM((N_HEADS, tq, LANES), F32),
                        pltpu.VMEM((N_HEADS, tq, LANES), F32),
                        pltpu.VMEM((N_HEADS, tq, PAIR_W), F32)],
        compiler_params=_cparams(("arbitrary", "arbitrary")),
        name="prompt_attention",
    )(q, ktb, vtb, cumt, cum)


def _transpose_rows8(blocks):
    sub = lax.broadcasted_iota(jnp.int32, blocks[0].shape, 0)
    a = list(blocks)
    for s in (4, 2, 1):
        keep = (sub & s) == 0
        b = list(a)
        for i in range(8):
            if i & s == 0:
                lo, hi = a[i], a[i + s]
                b[i] = jnp.where(keep, lo, pltpu.roll(hi, s, axis=0))
                b[i + s] = jnp.where(keep, pltpu.roll(lo, 8 - s, axis=0), hi)
        a = b
    return a


def _interleave_rows(x, nv):
    out = [None] * nv
    for jh in range(nv // 8):
        t = _transpose_rows8([x[r * nv + jh * 8:r * nv + jh * 8 + 8, :] for r in range(8)])
        for jl in range(8):
            out[8 * jh + jl] = t[jl]
    return jnp.concatenate(out, axis=0)


def _deinterleave_rows(y, nv):
    nb = nv // 8
    out = [None] * nv
    for jh in range(nb):
        t = _transpose_rows8([y[(8 * jh + jl) * 8:(8 * jh + jl) * 8 + 8, :] for jl in range(8)])
        for r in range(8):
            out[r * nb + jh] = t[r]
    return jnp.concatenate(out, axis=0)


def _causal_conv3(up, prev_ref, c, wc_ref, bc_ref):
    th = up.shape[0]
    first_row = lax.broadcasted_iota(jnp.int32, (8, up.shape[1]), 0) == 0

    def wrap(prev_blk, cur_blk):
        return jnp.where(first_row, pltpu.roll(prev_blk, 1, axis=0), pltpu.roll(cur_blk, 1, axis=0))

    w1 = wrap(prev_ref[c, 8:16, :], up[th - 8:th])
    w2 = wrap(prev_ref[c, 0:8, :], up[th - 16:th - 8])
    s1 = jnp.concatenate([w1, up[0:th - 8]], axis=0)
    s2 = jnp.concatenate([w2, w1, up[0:th - 16]], axis=0)
    prev_ref[c] = up[th - 16:th]
    return bc_ref[c] + s2 * wc_ref[c, 0:1, :] + s1 * wc_ref[c, 1:2, :] + up * wc_ref[c, 2:3, :]


def _in_sample_kernel(x_ref, mod_ref, g1_ref, wq_ref, wkv_ref, wf_ref, wus_ref, bf_ref, lng_ref, lnb_ref,
                      gmat_ref, wst_ref, bs_ref, same_ref,
                      q_ref, k_ref, v_ref, lf_ref, cq_ref, svn_ref, sgu_ref):
    n = x_ref.shape[0]
    h = _rms_mod(x_ref[...], g1_ref[...], mod_ref[0], mod_ref[1]).astype(BF16)
    q_ref[...] = jnp.dot(h, wq_ref[...], preferred_element_type=F32) * ATT_SCALE
    kv = jnp.dot(h, wkv_ref[...], preferred_element_type=F32)
    k_ref[...] = kv[:, 0:ATT_W]
    v_ref[...] = kv[:, ATT_W:2 * ATT_W]
    lf = _log_sigmoid(jnp.dot(h, wf_ref[...], preferred_element_type=F32) + bf_ref[...])
    lf_ref[...] = lf

    same = same_ref[...]
    hi, mid, lo = _split3(lf)
    tri = same.astype(BF16)
    d = functools.partial(jnp.dot, preferred_element_type=F32)
    cq_ref[...] = d(tri, hi) + d(tri, mid) + d(tri, lo)

    us = jnp.dot(h, wus_ref[...], preferred_element_type=F32)
    u = jax.nn.gelu(us[:, 0:SGU_W])
    sv = jax.nn.gelu(us[:, SGU_W:2 * SGU_W])
    svn = _group_ln(sv, gmat_ref[...], lng_ref[...], lnb_ref[...])
    svn_ref[...] = svn
    wm = [jnp.where(same > 0, wst_ref[g], 0.0).astype(BF16) for g in range(N_GROUPS)]
    sgu_ref[...] = _sgu_pairs(wm, svn.astype(BF16), u, bs_ref[...]).astype(BF16)


def _in_sample(x2, mod_rows, g1, wq, wkv, wf, wus, bf_row, lng, lnb, gmat, wst, bs_full, same):
    n, d = x2.shape
    args = (x2, mod_rows, g1, wq, wkv, wf, wus, bf_row, lng, lnb, gmat, wst, bs_full, same)
    out_shape = (jax.ShapeDtypeStruct((n, ATT_W), F32), jax.ShapeDtypeStruct((n, ATT_W), F32),
                 jax.ShapeDtypeStruct((n, ATT_W), F32), jax.ShapeDtypeStruct((n, LANES), F32),
                 jax.ShapeDtypeStruct((n, LANES), F32), jax.ShapeDtypeStruct((n, SGU_W), F32),
                 jax.ShapeDtypeStruct((n, SGU_W), BF16))
    return pl.pallas_call(
        _in_sample_kernel, grid=(1,),
        in_specs=[_const_spec(a.shape) for a in args],
        out_specs=tuple(pl.BlockSpec(o.shape, lambda i: (0, 0)) for o in out_shape),
        out_shape=out_shape,
        compiler_params=_cparams(("arbitrary",)),
        name="sample_in_proj",
    )(*args)


def _decode_begin(q, kn, vn, cqcol, cqmat_ref, bb):
    t_new = q.shape[0]
    rows = t_new * N_HEADS
    sub = lax.broadcasted_iota(jnp.int32, (N_HEADS, ATT_W), 0)
    lane_head = lax.broadcasted_iota(jnp.int32, (N_HEADS, ATT_W), 1) // HEAD_DIM
    own = sub == lane_head
    qbd = jnp.concatenate(
        [jnp.where(own, jnp.broadcast_to(q[t:t + 1, :], (N_HEADS, ATT_W)), 0.0) for t in range(t_new)], axis=0)
    rt = lax.broadcasted_iota(jnp.int32, (rows, 1), 0) // N_HEADS
    s_new = []
    for t2 in range(t_new):
        sc = jnp.sum(qbd * kn[t2:t2 + 1, :], axis=1, keepdims=True) + cqcol - cqmat_ref[bb, :, t2:t2 + 1]
        s_new.append(jnp.where(rt >= t2, sc, -jnp.inf))
    m = s_new[0]
    for t2 in range(1, t_new):
        m = jnp.maximum(m, s_new[t2])
    l = jnp.zeros((rows, 1), F32)
    acc = jnp.zeros((rows, ATT_W), F32)
    for t2 in range(t_new):
        pr = jnp.exp(s_new[t2] - m)
        l = l + pr
        acc = acc + pr * vn[t2:t2 + 1, :]
    return dict(qbd_b=qbd.astype(BF16), cqcol=cqcol, own=own, t_new=t_new,
                m=m, l=l, acc=acc, c_run=jnp.zeros((N_HEADS, PAGE), F32))


def _decode_scores(st, kbuf, lbuf, slot, uo, valid):
    g_pages = kbuf.shape[1]
    t_new = st["t_new"]
    c_run = st["c_run"]
    lp = lbuf[slot].reshape(g_pages * N_HEADS, PAGE)
    r = _dot3(lp, uo)
    sufs = [None] * g_pages
    for i in reversed(range(g_pages)):
        sufs[i] = jnp.where(valid[i], r[i * N_HEADS:(i + 1) * N_HEADS, 0:PAGE] + c_run, -jnp.inf)
        c_run = c_run + r[i * N_HEADS:(i + 1) * N_HEADS, PAGE:2 * PAGE]
    scs = []
    for i in range(0, g_pages, 2):
        kt2 = jnp.concatenate([kbuf[slot, i].astype(BF16), kbuf[slot, i + 1].astype(BF16)], axis=1)
        sc = jnp.dot(st["qbd_b"], kt2, preferred_element_type=F32)
        bias = jnp.concatenate([jnp.concatenate([sufs[i]] * t_new, axis=0),
                                jnp.concatenate([sufs[i + 1]] * t_new, axis=0)], axis=1)
        scs.append(sc + bias + st["cqcol"])
    return scs, dict(st, c_run=c_run)


def _decode_update(st, scs, vbuf, slot):
    nt = (((1,), (1,)), ((), ()))
    m, l, acc = st["m"], st["l"], st["acc"]
    m_new = m
    for sc in scs:
        m_new = jnp.maximum(m_new, jnp.max(sc, axis=1, keepdims=True))
    alpha = jnp.exp(m - m_new)
    l = alpha * l
    acc = alpha * acc
    for idx, sc in enumerate(scs):
        i = 2 * idx
        pr = jnp.exp(sc - m_new)
        l = l + jnp.sum(pr, axis=1, keepdims=True)
        vt2 = jnp.concatenate([vbuf[slot, i].astype(BF16), vbuf[slot, i + 1].astype(BF16)], axis=1)
        acc = acc + lax.dot_general(pr.astype(BF16), vt2, nt, preferred_element_type=F32)
    return dict(st, m=m_new, l=l, acc=acc)


def _decode_end(st, o_ref, bb):
    o = st["acc"] / st["l"]
    for t in range(st["t_new"]):
        blk = jnp.where(st["own"], o[t * N_HEADS:(t + 1) * N_HEADS, :], 0.0)
        o_ref[bb, t:t + 1, :] = jnp.sum(blk, axis=0, keepdims=True)


def _ffn_decode_kernel(pt_ref, x_ref, att_ref, sgu_ref, mod_ref, wo_ref, g2_ref, wug_ref, wuv_ref, wcg_ref,
                       wcv_ref, bcg_ref, bcv_ref, wd_ref, gf_ref,
                       q_ref, kn_ref, vn_ref, cqcol_ref, cqmat_ref, uo_ref, kc_hbm, vc_hbm, lc_hbm,
                       y_ref, conv_ref, o_ref,
                       x1_ref, h2_ref, f_ref, pg_ref, pv_ref, kbuf, vbuf, lbuf, sem):
    s = pl.program_id(1)
    step = pl.program_id(0) * pl.num_programs(1) + s
    n_steps = pl.num_programs(0) * pl.num_programs(1)
    tm = x_ref.shape[1]
    n_chunks = wd_ref.shape[0]
    th = tm // 2
    nv = th // 8
    halves = (slice(0, th), slice(th, tm))
    n_pages = pt_ref.shape[1]
    g_pages = kbuf.shape[1]
    n_regions = 2 * n_chunks
    n_slots = kbuf.shape[0]
    ahead = n_slots - 2
    total = n_steps * n_regions

    def page_of(r, i):
        return n_pages - g_pages * (r + 1) + i

    def copies(n, slot):
        bb = n // n_regions
        r = n % n_regions
        out = []
        for i in range(g_pages):
            phys = pt_ref[bb, jnp.maximum(page_of(r, i), 0)]
            out.append(pltpu.make_async_copy(kc_hbm.at[phys], kbuf.at[slot, i], sem.at[0, slot]))
            out.append(pltpu.make_async_copy(vc_hbm.at[phys], vbuf.at[slot, i], sem.at[1, slot]))
            out.append(pltpu.make_async_copy(lc_hbm.at[phys], lbuf.at[slot, i], sem.at[2, slot]))
        return out

    @pl.when(step == 0)
    def _():
        for n0 in range(ahead):
            for c in copies(n0, n0):
                c.start()

    @pl.when(s == 0)
    def _():
        pg_ref[...] = jnp.zeros(pg_ref.shape, F32)
        pv_ref[...] = jnp.zeros(pv_ref.shape, F32)

    for i, rows in enumerate(halves):
        mix = (jnp.dot(att_ref[0, rows, :], wo_ref[0:ATT_W, :], preferred_element_type=F32)
               + jnp.dot(sgu_ref[0, rows, :], wo_ref[ATT_W:ATT_W + SGU_W, :], preferred_element_type=F32))
        x1 = _interleave_rows(x_ref[0, rows, :] + mod_ref[0, 2:3, :] * mix, nv)
        x1_ref[rows, :] = x1
        h2_ref[i] = _rms_mod(x1, g2_ref[...], mod_ref[0, 3:4, :], mod_ref[0, 4:5, :]).astype(BF16)
    f_ref[...] = jnp.zeros(f_ref.shape, F32)

    uo = uo_ref[...]
    st0 = _decode_begin(q_ref[0], kn_ref[0], vn_ref[0], cqcol_ref[0], cqmat_ref, 0)

    def trip(c, carry):
        st = dict(st0, m=carry[0], l=carry[1], acc=carry[2], c_run=carry[3])
        regions = [2 * c, 2 * c + 1]
        ns_ = [step * n_regions + r for r in regions]
        slots = [n % n_slots for n in ns_]
        for n in ns_:
            @pl.when(n + ahead < total)
            def _(n=n):
                for cp in copies(n + ahead, (n + ahead) % n_slots):
                    cp.start()
        for n, slot in zip(ns_, slots):
            for cp in copies(n, slot):
                cp.wait()
        cols = pl.ds(pl.multiple_of(c * FC, FC), FC)
        ups = [(jnp.dot(h2_ref[i], wug_ref[:, cols], preferred_element_type=F32),
                jnp.dot(h2_ref[i], wuv_ref[:, cols], preferred_element_type=F32)) for i in range(2)]
        scs = []
        for r, slot in zip(regions, slots):
            valid = [page_of(r, k) >= 0 for k in range(g_pages)]
            sc, st = _decode_scores(st, kbuf, lbuf, slot, uo, valid)
            scs.append(sc)

        def down(i):
            cg = _causal_conv3(ups[i][0], pg_ref, c, wcg_ref, bcg_ref)
            cv = _causal_conv3(ups[i][1], pv_ref, c, wcv_ref, bcv_ref)
            act = (jax.nn.silu(cg) * cv).astype(BF16)
            f_ref[i] += jnp.dot(act, wd_ref[c], preferred_element_type=F32)

        down(0)
        for sc, slot in zip(scs, slots):
            st = _decode_update(st, sc, vbuf, slot)
        down(1)
        return st["m"], st["l"], st["acc"], st["c_run"]

    m, l, acc, c_run = lax.fori_loop(0, n_chunks, trip, (st0["m"], st0["l"], st0["acc"], st0["c_run"]))
    _decode_end(dict(st0, m=m, l=l, acc=acc, c_run=c_run), o_ref, 0)

    for i, rows in enumerate(halves):
        x2 = x1_ref[rows, :] + mod_ref[0, 5:6, :] * f_ref[i]
        y = x2 * lax.rsqrt(jnp.mean(x2 * x2, axis=-1, keepdims=True) + EPS) * gf_ref[...]
        y_ref[0, rows, :] = _deinterleave_rows(y, nv)
    fc = pg_ref.shape[2]
    for c in range(n_chunks):
        conv_ref[0, :, c * fc:(c + 1) * fc] = pg_ref[c]
        conv_ref[0, :, (n_chunks + c) * fc:(n_chunks + c + 1) * fc] = pv_ref[c]


def _ffn_prompt_and_decode(page_table, x, att, sgu, mod3, wo, g2, wug, wuv, wcg, wcv, bcg, bcv, wd, gf,
                           q3, k3, v3, cqcol, cqmat, uo, kc, vc, lc):
    b, s, d = x.shape
    tm = TM_FFN
    ns = s // tm
    n_chunks, fc, _ = wd.shape
    dff = n_chunks * fc
    nb, t_new, _ = q3.shape
    g = PAGES_PER_GROUP
    assert nb == b * ns and -(-page_table.shape[1] // g) == 2 * n_chunks and g % 2 == 0
    rows = t_new * N_HEADS
    consts = (wo, g2, wug, wuv, wcg, wcv, bcg, bcv, wd, gf)
    row_blk = lambda w: pl.BlockSpec((1, tm, w), lambda i, j, pt: (i, j, 0))
    smp_blk = lambda r, w: pl.BlockSpec((1, r, w), lambda i, j, pt: (i * ns + j, 0, 0))
    const_blk = lambda a: pl.BlockSpec(a.shape, lambda i, j, pt: (0,) * a.ndim, pipeline_mode=pl.Buffered(1))
    grid_spec = pltpu.PrefetchScalarGridSpec(
        num_scalar_prefetch=1,
        grid=(b, ns),
        in_specs=[row_blk(d), row_blk(ATT_W), row_blk(SGU_W),
                  pl.BlockSpec((1, N_MOD, d), lambda i, j, pt: (i, 0, 0))]
                 + [const_blk(a) for a in consts]
                 + [smp_blk(t_new, ATT_W), smp_blk(t_new, ATT_W), smp_blk(t_new, ATT_W),
                    smp_blk(rows, 1), smp_blk(rows, t_new), const_blk(uo),
                    pl.BlockSpec(memory_space=pl.ANY), pl.BlockSpec(memory_space=pl.ANY),
                    pl.BlockSpec(memory_space=pl.ANY)],
        out_specs=(row_blk(d), pl.BlockSpec((1, 16, 2 * dff), lambda i, j, pt: (i, 0, 0)),
                   smp_blk(t_new, ATT_W)),
        scratch_shapes=[pltpu.VMEM((tm, d), F32),
                        pltpu.VMEM((2, tm // 2, d), BF16),
                        pltpu.VMEM((2, tm // 2, d), F32),
                        pltpu.VMEM((n_chunks, 16, fc), F32),
                        pltpu.VMEM((n_chunks, 16, fc), F32),
                        pltpu.VMEM((DECODE_SLOTS, g, ATT_W, PAGE), F32),
                        pltpu.VMEM((DECODE_SLOTS, g, ATT_W, PAGE), F32),
                        pltpu.VMEM((DECODE_SLOTS, g, N_HEADS, PAGE), F32),
                        pltpu.SemaphoreType.DMA((3, DECODE_SLOTS))],
    )
    return pl.pallas_call(
        _ffn_decode_kernel, grid_spec=grid_spec,
        out_shape=(jax.ShapeDtypeStruct((b, s, d), F32), jax.ShapeDtypeStruct((b, 16, 2 * dff), F32),
                   jax.ShapeDtypeStruct((nb, t_new, ATT_W), F32)),
        compiler_params=_cparams(("arbitrary", "arbitrary")),
        name="prompt_ffn_decode",
    )(page_table, x, att, sgu, mod3, *consts, q3, k3, v3, cqcol, cqmat, uo, kc, vc, lc)


def _ffn_sample_kernel(x_ref, att_ref, sgu_ref, mod_ref, wo_ref, g2_ref, wug_ref, wuv_ref, wcg_ref, wcv_ref,
                       bcg_ref, bcv_ref, stg_ref, stv_ref, e1_ref, e2_ref, wd_ref, gf_ref,
                       y_ref, upg_ref, upv_ref,
                       x1_ref, h2_ref, acc_ref):
    j = pl.program_id(0)
    n = x_ref.shape[0]
    t_new = 4

    @pl.when(j == 0)
    def _():
        mix = (jnp.dot(att_ref[...].astype(BF16), wo_ref[0:ATT_W, :], preferred_element_type=F32)
               + jnp.dot(sgu_ref[...], wo_ref[ATT_W:ATT_W + SGU_W, :], preferred_element_type=F32))
        x1 = x_ref[...] + mod_ref[2] * mix
        x1_ref[...] = x1
        h2_ref[...] = _rms_mod(x1, g2_ref[...], mod_ref[3], mod_ref[4]).astype(BF16)
        acc_ref[...] = jnp.zeros_like(acc_ref)

    h2 = h2_ref[...]
    tpos = lax.broadcasted_iota(jnp.int32, (n, FC), 0) % t_new

    def place(e_ref, parts):
        d = functools.partial(jnp.dot, preferred_element_type=F32)
        return d(e_ref[...], parts[0]) + d(e_ref[...], parts[1]) + d(e_ref[...], parts[2])

    def conv(up, wc_ref, bc_ref, st_ref):
        parts = _split3(st_ref[...])
        s1 = jnp.where(tpos >= 1, pltpu.roll(up, 1, axis=0), place(e1_ref, parts))
        s2 = jnp.where(tpos >= 2, pltpu.roll(up, 2, axis=0), place(e2_ref, parts))
        return bc_ref[0] + s2 * wc_ref[0, 0:1, :] + s1 * wc_ref[0, 1:2, :] + up * wc_ref[0, 2:3, :]

    upg = jnp.dot(h2, wug_ref[...], preferred_element_type=F32)
    upv = jnp.dot(h2, wuv_ref[...], preferred_element_type=F32)
    upg_ref[...] = upg
    upv_ref[...] = upv
    cg = conv(upg, wcg_ref, bcg_ref, stg_ref)
    cv = conv(upv, wcv_ref, bcv_ref, stv_ref)
    act = (jax.nn.silu(cg) * cv).astype(BF16)
    acc_ref[...] += jnp.dot(act, wd_ref[0], preferred_element_type=F32)

    @pl.when(j == pl.num_programs(0) - 1)
    def _():
        x2 = x1_ref[...] + mod_ref[5] * acc_ref[...]
        y_ref[...] = x2 * lax.rsqrt(jnp.mean(x2 * x2, axis=-1, keepdims=True) + EPS) * gf_ref[...]


def _ffn_sample(x2, att, sgu, mod_rows, wo, g2, wug, wuv, wcg, wcv, bcg, bcv, st2, e1, e2, wd, gf):
    n, d = x2.shape
    nf = wd.shape[0]
    dff = nf * FC
    full = lambda a: _const_spec(a.shape)
    colc = lambda r: pl.BlockSpec((r, FC), lambda j: (0, j))
    chunk = lambda a: pl.BlockSpec((1,) + a.shape[1:], lambda j: (j, 0, 0))
    in_specs = [full(x2), full(att), full(sgu), full(mod_rows), full(wo), full(g2),
                colc(d), colc(d), chunk(wcg), chunk(wcv), chunk(bcg), chunk(bcv),
                pl.BlockSpec((st2.shape[0], FC), lambda j: (0, j)),
                pl.BlockSpec((st2.shape[0], FC), lambda j: (0, nf + j)),
                full(e1), full(e2),
                chunk(wd), full(gf)]
    return pl.pallas_call(
        _ffn_sample_kernel, grid=(nf,),
        in_specs=in_specs,
        out_specs=(pl.BlockSpec((n, d), lambda j: (0, 0)), colc(n), colc(n)),
        out_shape=(jax.ShapeDtypeStruct((n, d), F32), jax.ShapeDtypeStruct((n, dff), F32),
                   jax.ShapeDtypeStruct((n, dff), F32)),
        scratch_shapes=[pltpu.VMEM((n, d), F32), pltpu.VMEM((n, d), BF16), pltpu.VMEM((n, d), F32)],
        compiler_params=_cparams(("arbitrary",)),
        name="sample_ffn",
    )(x2, att, sgu, mod_rows, wo, g2, wug, wuv, wcg, wcv, bcg, bcv, st2, st2, e1, e2, wd, gf)


def kernel(x_prompt, x_sample, c_prompt, c_sample, cache_k, cache_v, cache_logf, state_conv, page_table,
           w_ada, b_ada, norm1_g, w_in, b_f, ln_v_g, ln_v_b, w_s, b_s, w_o, norm2_g, w_up, w_conv, b_conv,
           w_down, final_g):
    bp, s, d = x_prompt.shape
    bs, t_new, _ = x_sample.shape
    n_s = bs * t_new
    dff = w_down.shape[1]
    n_phys = cache_k.shape[1]
    assert w_ada.shape[0] == 1, "single layer"
    assert s % TM_IN == 0 and s % TQ == 0 and s % TM_FFN == 0 and TM_IN % TQ == 0
    assert dff % FC == 0 and n_s == CHUNK and t_new == 4
    assert bs == bp * (s // TM_FFN), "one decode sample rides along with each FFN grid step"

    wi = w_in[0]
    k0, v0, f0 = ATT_W, 2 * ATT_W, 3 * ATT_W
    u0 = f0 + N_HEADS
    wq = wi[:, 0:k0].astype(BF16)
    wkv = wi[:, k0:f0].astype(BF16)
    wf = wi[:, f0:u0]
    wus = wi[:, u0:].astype(BF16)
    wkvf_t = jnp.concatenate([wi[:, k0:f0], wf], axis=1).T.astype(BF16)
    wf_pad = jnp.pad(wf, ((0, 0), (0, LANES - N_HEADS))).astype(BF16)
    bf_col = b_f[0].reshape(N_HEADS, 1)
    bf_row = jnp.pad(b_f[0].reshape(1, N_HEADS), ((0, 0), (0, LANES - N_HEADS)))
    lng = ln_v_g[0].reshape(1, SGU_W)
    lnb = ln_v_b[0].reshape(1, SGU_W)
    gidx = jnp.arange(SGU_W) // SGU_DIM
    gmat = jnp.where(gidx[:, None] == gidx[None, :], 1.0 / SGU_DIM, 0.0).astype(BF16)
    ws_b = w_s[0]
    bs_full = jnp.repeat(b_s[0].T, SGU_DIM, axis=1)
    wst = jnp.tile(w_s[0][:, :t_new, :t_new], (1, bs, bs))
    bs_full_s = jnp.tile(jnp.repeat(b_s[0][:, :t_new].T, SGU_DIM, axis=1), (bs, 1))
    r_idx = jnp.arange(n_s)
    same = ((r_idx[:, None] // t_new == r_idx[None, :] // t_new)
            & (r_idx[:, None] >= r_idx[None, :])).astype(F32)
    pos = jnp.arange(TM_IN)
    tri = jnp.concatenate([(pos[:, None] <= pos[None, :]).astype(BF16),
                           jnp.ones((TM_IN, LANES), BF16)], axis=1)
    pp = jnp.arange(PAGE)
    uo = jnp.concatenate([(pp[:, None] > pp[None, :]).astype(BF16),
                          jnp.ones((PAGE, PAGE), BF16)], axis=1)
    wo = w_o[0].astype(BF16)
    nfc = dff // FC
    chunked = lambda a: a.reshape(a.shape[0], nfc, FC).transpose(1, 0, 2)
    wug = w_up[0][:, :dff].astype(BF16)
    wuv = w_up[0][:, dff:].astype(BF16)
    wcg, wcv = chunked(w_conv[0][:, :dff]), chunked(w_conv[0][:, dff:])
    bcg, bcv = chunked(b_conv[:, :dff]), chunked(b_conv[:, dff:])
    wd = w_down[0].astype(BF16).reshape(nfc, FC, d)
    g1 = norm1_g
    g2 = norm2_g
    gf = final_g.reshape(1, d)

    mod = _ada(jnp.concatenate([c_prompt, c_sample], axis=0), w_ada[0], b_ada)
    mod_p = mod[:bp].reshape(bp, N_MOD, d)
    mod_s = jnp.repeat(mod[bp:].reshape(bs, N_MOD, d), t_new, axis=0).transpose(1, 0, 2)

    q, kt, vt, ktb, vtb, lft, cumt, cum, sgu = _in_prompt(
        x_prompt, mod_p, g1, wq, wkvf_t, wus, bf_col, lng, lnb, gmat, ws_b, bs_full, tri)
    att = _attn_prompt(q, ktb, vtb, cumt, cum)
    new_k_p = kt.reshape(1, bp, N_HEADS, HEAD_DIM, s).transpose(0, 1, 4, 2, 3)
    new_v_p = vt.reshape(1, bp, N_HEADS, HEAD_DIM, s).transpose(0, 1, 4, 2, 3)
    new_lf_p = lft.transpose(0, 2, 1)[None]

    x2 = x_sample.reshape(n_s, d)
    q_s, k_s, v_s, lf_s, cq_s, svn_s, sgu_s = _in_sample(
        x2, mod_s, g1, wq, wkv, wf_pad, wus, bf_row, lng, lnb, gmat, wst, bs_full_s, same)
    cq3 = cq_s[:, :N_HEADS].reshape(bs, t_new, N_HEADS)
    cqcol = cq3.reshape(bs, t_new * N_HEADS, 1)
    cqmat = jnp.tile(cq3.transpose(0, 2, 1), (1, t_new, 1))
    kc = cache_k[0].transpose(0, 2, 3, 1).reshape(n_phys, ATT_W, PAGE)
    vc = cache_v[0].transpose(0, 2, 3, 1).reshape(n_phys, ATT_W, PAGE)
    lc = cache_logf[0].transpose(0, 2, 1)
    y_prompt, conv_p, att_s = _ffn_prompt_and_decode(
        page_table, x_prompt, att, sgu, mod_p, wo, g2, wug, wuv, wcg, wcv, bcg, bcv, wd, gf,
        q_s.reshape(bs, t_new, ATT_W), k_s.reshape(bs, t_new, ATT_W), v_s.reshape(bs, t_new, ATT_W),
        cqcol, cqmat, uo, kc, vc, lc)
    new_conv_p = conv_p[:, 7::8, :][None]

    st2 = state_conv[0].reshape(2 * bs, 2 * dff)
    tok, src = jnp.arange(n_s)[:, None], jnp.arange(2 * bs)[None, :]
    smp, t_in = tok // t_new, tok % t_new
    e1 = ((t_in == 0) & (src == 2 * smp + 1)).astype(BF16)
    e2 = ((t_in <= 1) & (src == 2 * smp + t_in)).astype(BF16)
    y_s, upg, upv = _ffn_sample(x2, att_s.reshape(n_s, ATT_W), sgu_s, mod_s, wo, g2, wug, wuv, wcg, wcv,
                                bcg, bcv, st2, e1, e2, wd, gf)
    up = jnp.concatenate([upg, upv], axis=1).reshape(bs, t_new, 2 * dff)

    return (y_prompt, y_s.reshape(bs, t_new, d),
            new_k_p, new_v_p, new_lf_p, new_conv_p,
            k_s.reshape(1, bs, t_new, N_HEADS, HEAD_DIM), v_s.reshape(1, bs, t_new, N_HEADS, HEAD_DIM),
            lf_s[:, :N_HEADS].reshape(1, bs, t_new, N_HEADS),
            svn_s.reshape(1, bs, t_new, N_GROUPS, SGU_DIM),
            up[:, t_new - 2:, :][None])
```

```python
import functools

import jax
import jax.numpy as jnp
import numpy as np
from jax import lax
from jax.experimental import pallas as pl
from jax.experimental.pallas import tpu as pltpu

F32 = jnp.float32
BF16 = jnp.bfloat16

N_HEADS = 8
HEAD_DIM = 64
ATT_W = N_HEADS * HEAD_DIM
N_GROUPS = 8
SGU_DIM = 64
SGU_W = N_GROUPS * SGU_DIM
CHUNK = 128
PAGE = 128
DEC_T = 4
N_MOD = 6
EPS = 1e-6
ATT_SCALE = HEAD_DIM ** -0.5
LOG2E = 1.4426950408889634
LANES = 128
PAIR_W = 2 * HEAD_DIM
VMEM_LIMIT = 58 * 1024 * 1024

TM_IN = 512
TQ = 512
ATTN_SCORE_SLOTS = 3
TM_FFN = 512
FC = 256
PAGES_PER_GROUP = 6
DECODE_SLOTS = 6


def _cparams(sem):
    return pltpu.CompilerParams(dimension_semantics=sem, vmem_limit_bytes=VMEM_LIMIT)


def _const_spec(shape):
    nd = len(shape)
    return pl.BlockSpec(shape, lambda *_: (0,) * nd, pipeline_mode=pl.Buffered(1))


def _split3(x):
    hi = x.astype(BF16)
    r1 = x - hi.astype(F32)
    mid = r1.astype(BF16)
    lo = (r1 - mid.astype(F32)).astype(BF16)
    return hi, mid, lo


def _dot3(x, w):
    hi, mid, lo = _split3(x)
    d = functools.partial(jnp.dot, preferred_element_type=F32)
    return d(hi, w) + d(mid, w) + d(lo, w)


def _rms_mod(x, g, shift, scale):
    y = x * lax.rsqrt(jnp.mean(x * x, axis=-1, keepdims=True) + EPS)
    return (y * g) * (1.0 + scale) + shift


def _log_sigmoid(z):
    return jnp.minimum(z, 0.0) - jnp.log1p(jnp.exp(-jnp.abs(z)))


def _group_ln(sv, gmat, ln_g, ln_b):
    mu = jnp.dot(sv.astype(BF16), gmat, preferred_element_type=F32)
    d = sv - mu
    var = jnp.dot((d * d).astype(BF16), gmat, preferred_element_type=F32)
    return d * lax.rsqrt(var + EPS) * ln_g + ln_b


def _sgu_pairs(wm, svn_b, u, bias):
    lane = lax.broadcasted_iota(jnp.int32, (CHUNK, LANES), 1)
    outs = []
    for p in range(N_GROUPS // 2):
        rhs = svn_b[:, p * LANES:(p + 1) * LANES]
        lo = jnp.dot(wm[2 * p], rhs, preferred_element_type=F32)
        hi = jnp.dot(wm[2 * p + 1], rhs, preferred_element_type=F32)
        mixed = jnp.where(lane < SGU_DIM, lo, hi) + bias[:, p * LANES:(p + 1) * LANES]
        outs.append(u[:, p * LANES:(p + 1) * LANES] * mixed)
    return jnp.concatenate(outs, axis=1)


def _ada_kernel(c_ref, w_ref, b_ref, o_ref):
    a = jax.nn.silu(c_ref[...])
    o_ref[...] = jnp.dot(a.astype(BF16), w_ref[...].astype(BF16),
                         preferred_element_type=F32) + b_ref[...]


def _ada(c_all, w_ada, b_ada):
    n, d = c_all.shape
    cols = w_ada.shape[1]
    bn = d
    return pl.pallas_call(
        _ada_kernel,
        grid=(cols // bn,),
        in_specs=[pl.BlockSpec((n, d), lambda j: (0, 0)),
                  pl.BlockSpec((d, bn), lambda j: (0, j)),
                  pl.BlockSpec((1, bn), lambda j: (0, j))],
        out_specs=pl.BlockSpec((n, bn), lambda j: (0, j)),
        out_shape=jax.ShapeDtypeStruct((n, cols), F32),
        compiler_params=_cparams(("arbitrary",)),
        name="ada_ln",
    )(c_all, w_ada, b_ada)


def _in_prompt_kernel(x_ref, mod_ref, g1_ref, wq_ref, wkvf_ref, wus_ref, bf_ref, lng_ref, lnb_ref,
                      gmat_ref, ws_ref, bs_ref, tri_ref,
                      q_ref, kt_ref, vt_ref, ktb_ref, vtb_ref, lft_ref, cumt_ref, cum_ref, sgu_ref,
                      carry_ref):
    s = pl.program_id(1)
    tm = x_ref.shape[1]

    @pl.when(s == 0)
    def _():
        carry_ref[...] = jnp.zeros_like(carry_ref)

    h = _rms_mod(x_ref[0], g1_ref[...], mod_ref[0, 0:1, :], mod_ref[0, 1:2, :]).astype(BF16)

    q = jnp.dot(h, wq_ref[...], preferred_element_type=F32)
    q_ref[0] = (q * (ATT_SCALE * LOG2E)).astype(BF16)

    kvf = lax.dot_general(wkvf_ref[...], h, (((1,), (1,)), ((), ())), preferred_element_type=F32)
    kt = kvf[0:ATT_W]
    vt = kvf[ATT_W:2 * ATT_W]
    kt_ref[0] = kt
    vt_ref[0] = vt
    ktb_ref[0, 0] = kt.astype(BF16)
    vtb_ref[0, 0] = vt.astype(BF16)
    lft = _log_sigmoid(kvf[2 * ATT_W:2 * ATT_W + N_HEADS] + bf_ref[...])
    lft_ref[0] = lft

    r = _dot3(lft, tri_ref[...])
    carry = carry_ref[...]
    cumt = r[:, 0:tm] + jnp.concatenate([carry] * (tm // LANES), axis=1)
    carry_ref[...] = carry + r[:, tm:tm + LANES]
    cumt = cumt * LOG2E
    cumt_ref[0, 0] = cumt
    pad = jnp.zeros((LANES - N_HEADS, LANES), F32)
    for c in range(tm // LANES):
        blk = jnp.concatenate([cumt[:, c * LANES:(c + 1) * LANES], pad], axis=0)
        cum_ref[0, c * LANES:(c + 1) * LANES, :] = blk.T[:, 0:N_HEADS]

    us = jnp.dot(h, wus_ref[...], preferred_element_type=F32)
    u = jax.nn.gelu(us[:, 0:SGU_W])
    sv = jax.nn.gelu(us[:, SGU_W:2 * SGU_W])
    svn = _group_ln(sv, gmat_ref[...], lng_ref[...], lnb_ref[...]).astype(BF16)

    row = lax.broadcasted_iota(jnp.int32, (CHUNK, CHUNK), 0)
    col = lax.broadcasted_iota(jnp.int32, (CHUNK, CHUNK), 1)
    wm = [jnp.where(row >= col, ws_ref[g], 0.0).astype(BF16) for g in range(N_GROUPS)]
    bias = bs_ref[...]
    for c in range(tm // CHUNK):
        sl = slice(c * CHUNK, (c + 1) * CHUNK)
        sgu_ref[0, sl, :] = _sgu_pairs(wm, svn[sl], u[sl], bias).astype(BF16)


def _in_prompt(x, mod3, g1, wq, wkvf, wus, bf_col, lng, lnb, gmat, ws_b, bs_full, tri):
    b, s, d = x.shape
    tm = TM_IN
    ns = s // tm
    grid = (b, ns)
    row_blk = lambda w: pl.BlockSpec((1, tm, w), lambda i, j: (i, j, 0))
    colT_blk = lambda r: pl.BlockSpec((1, r, tm), lambda i, j: (i, 0, j))
    tiledT_blk = lambda r: pl.BlockSpec((1, 1, r, tm), lambda i, j: (i, j, 0, 0))
    out_shape = (
        jax.ShapeDtypeStruct((b, s, ATT_W), BF16),
        jax.ShapeDtypeStruct((b, ATT_W, s), F32),
        jax.ShapeDtypeStruct((b, ATT_W, s), F32),
        jax.ShapeDtypeStruct((b, ns, ATT_W, tm), BF16),
        jax.ShapeDtypeStruct((b, ns, ATT_W, tm), BF16),
        jax.ShapeDtypeStruct((b, N_HEADS, s), F32),
        jax.ShapeDtypeStruct((b, ns, N_HEADS, tm), F32),
        jax.ShapeDtypeStruct((b, s, N_HEADS), F32),
        jax.ShapeDtypeStruct((b, s, SGU_W), BF16),
    )
    out_specs = (row_blk(ATT_W), colT_blk(ATT_W), colT_blk(ATT_W), tiledT_blk(ATT_W), tiledT_blk(ATT_W),
                 colT_blk(N_HEADS), tiledT_blk(N_HEADS), row_blk(N_HEADS), row_blk(SGU_W))
    in_specs = [row_blk(d),
                pl.BlockSpec((1, N_MOD, d), lambda i, j: (i, 0, 0)),
                _const_spec(g1.shape), _const_spec(wq.shape), _const_spec(wkvf.shape), _const_spec(wus.shape),
                _const_spec(bf_col.shape), _const_spec(lng.shape), _const_spec(lnb.shape),
                _const_spec(gmat.shape), _const_spec(ws_b.shape), _const_spec(bs_full.shape),
                _const_spec(tri.shape)]
    return pl.pallas_call(
        _in_prompt_kernel, grid=grid, in_specs=in_specs, out_specs=out_specs, out_shape=out_shape,
        scratch_shapes=[pltpu.VMEM((N_HEADS, LANES), F32)],
        compiler_params=_cparams(("arbitrary", "arbitrary")),
        name="prompt_in_proj",
    )(x, mod3, g1, wq, wkvf, wus, bf_col, lng, lnb, gmat, ws_b, bs_full, tri)


def _attn_kernel(q_ref, kt_ref, vt_ref, cumt_ref, cum_ref, o_ref, qm_ref, s_ref, cq_ref, m_ref, acc_ref):
    qi = pl.program_id(1)
    tq = q_ref.shape[1]
    tk = kt_ref.shape[3]
    qs = qi * tq
    j_diag = (qs + tq - 1) // tk
    n_slots = s_ref.shape[0]
    ahead = n_slots - 1
    lane = lax.broadcasted_iota(jnp.int32, (tq, LANES), 1)
    qpos = qs + lax.broadcasted_iota(jnp.int32, (tq, tk), 0)
    kcol = lax.broadcasted_iota(jnp.int32, (tq, tk), 1)
    ones_half = jnp.ones((HEAD_DIM, tk), BF16)
    nt = (((1,), (1,)), ((), ()))

    for p in range(N_HEADS // 2):
        qp = q_ref[0, :, p * PAIR_W:(p + 1) * PAIR_W].astype(F32)
        qm_ref[2 * p] = jnp.where(lane < HEAD_DIM, qp, 0.0).astype(BF16)
        qm_ref[2 * p + 1] = jnp.where(lane >= HEAD_DIM, qp, 0.0).astype(BF16)
    for h in range(N_HEADS):
        cq_ref[h] = jnp.broadcast_to(cum_ref[0, :, h:h + 1], (tq, LANES))
    m_ref[...] = jnp.full(m_ref.shape, -jnp.inf, F32)
    acc_ref[...] = jnp.zeros(acc_ref.shape, F32)

    def scores(j, h):
        p = h // 2
        kt = kt_ref[0, j, p * PAIR_W:(p + 1) * PAIR_W, :]
        s_ref[h % n_slots] = jnp.dot(qm_ref[h], kt, preferred_element_type=F32)

    def softmax_pv(j, h, masked):
        p, half = divmod(h, 2)
        t = s_ref[h % n_slots] - cumt_ref[0, j, h:h + 1, :]
        if masked:
            t = jnp.where(j * tk + kcol <= qpos, t, -jnp.inf)
        cq = cq_ref[h]
        m = m_ref[h]
        m_new = jnp.maximum(m, jnp.max(t, axis=1, keepdims=True) + cq)
        alpha = jnp.exp2(m - m_new)
        d = cq - m_new
        pr = jnp.exp2(t + jnp.concatenate([d] * (tk // LANES), axis=1)).astype(BF16)
        vt = vt_ref[0, j, p * PAIR_W:(p + 1) * PAIR_W, :]
        if half == 0:
            vaug = jnp.concatenate([vt[0:HEAD_DIM], ones_half], axis=0)
        else:
            vaug = jnp.concatenate([ones_half, vt[HEAD_DIM:PAIR_W]], axis=0)
        acc_ref[h] = alpha * acc_ref[h] + lax.dot_general(pr, vaug, nt, preferred_element_type=F32)
        m_ref[h] = m_new

    def key_tile(j, masked):
        for h in range(N_HEADS):
            softmax_pv(j, h, masked)
            nh = h + ahead
            if nh < N_HEADS:
                scores(j, nh)
            elif not masked:
                scores(j + 1, nh - N_HEADS)

    for h in range(ahead):
        scores(0, h)

    def loop_body(j, carry):
        key_tile(j, False)
        return carry

    lax.fori_loop(0, j_diag, loop_body, 0)
    key_tile(j_diag, True)

    for p in range(N_HEADS // 2):
        a0 = acc_ref[2 * p]
        a1 = acc_ref[2 * p + 1]
        o0 = a0 / pltpu.roll(a0, HEAD_DIM, axis=1)
        o1 = a1 / pltpu.roll(a1, HEAD_DIM, axis=1)
        o_ref[0, :, p * PAIR_W:(p + 1) * PAIR_W] = jnp.where(lane < HEAD_DIM, o0, o1).astype(BF16)


def _attn_prompt(q, ktb, vtb, cumt, cum):
    b, s, _ = q.shape
    nk, tk = ktb.shape[1], ktb.shape[3]
    tq = TQ
    return pl.pallas_call(
        _attn_kernel,
        grid=(b, s // tq),
        in_specs=[pl.BlockSpec((1, tq, ATT_W), lambda i, j: (i, j, 0)),
                  pl.BlockSpec((1, nk, ATT_W, tk), lambda i, j: (i, 0, 0, 0)),
                  pl.BlockSpec((1, nk, ATT_W, tk), lambda i, j: (i, 0, 0, 0)),
                  pl.BlockSpec((1, nk, N_HEADS, tk), lambda i, j: (i, 0, 0, 0)),
                  pl.BlockSpec((1, tq, N_HEADS), lambda i, j: (i, j, 0))],
        out_specs=pl.BlockSpec((1, tq, ATT_W), lambda i, j: (i, j, 0)),
        out_shape=jax.ShapeDtypeStruct((b, s, ATT_W), BF16),
        scratch_shapes=[pltpu.VMEM((N_HEADS, tq, PAIR_W), BF16),
                        pltpu.VMEM((ATTN_SCORE_SLOTS, tq, tk), F32),
                        pltpu.VMEM((N_HEADS, tq, LANES), F32),
                        pltpu.VMEM((N_HEADS, tq, LANES), F32),
                        pltpu.VMEM((N_HEADS, tq, PAIR_W), F32)],
        compiler_params=_cparams(("arbitrary", "arbitrary")),
        name="prompt_attention",
    )(q, ktb, vtb, cumt, cum)


def _transpose_rows8(blocks):
    sub = lax.broadcasted_iota(jnp.int32, blocks[0].shape, 0)
    a = list(blocks)
    for s in (4, 2, 1):
        keep = (sub & s) == 0
        b = list(a)
        for i in range(8):
            if i & s == 0:
                lo, hi = a[i], a[i + s]
                b[i] = jnp.where(keep, lo, pltpu.roll(hi, s, axis=0))
                b[i + s] = jnp.where(keep, pltpu.roll(lo, 8 - s, axis=0), hi)
        a = b
    return a


def _interleave_rows(x, nv):
    out = [None] * nv
    for jh in range(nv // 8):
        t = _transpose_rows8([x[r * nv + jh * 8:r * nv + jh * 8 + 8, :] for r in range(8)])
        for jl in range(8):
            out[8 * jh + jl] = t[jl]
    return jnp.concatenate(out, axis=0)


def _deinterleave_rows(y, nv):
    nb = nv // 8
    out = [None] * nv
    for jh in range(nb):
        t = _transpose_rows8([y[(8 * jh + jl) * 8:(8 * jh + jl) * 8 + 8, :] for jl in range(8)])
        for r in range(8):
            out[r * nb + jh] = t[r]
    return jnp.concatenate(out, axis=0)


def _causal_conv3(up, prev_ref, c, wc_ref, bc_ref):
    th = up.shape[0]
    first_row = lax.broadcasted_iota(jnp.int32, (8, up.shape[1]), 0) == 0

    def wrap(prev_blk, cur_blk):
        return jnp.where(first_row, pltpu.roll(prev_blk, 1, axis=0), pltpu.roll(cur_blk, 1, axis=0))

    w1 = wrap(prev_ref[c, 8:16, :], up[th - 8:th])
    w2 = wrap(prev_ref[c, 0:8, :], up[th - 16:th - 8])
    s1 = jnp.concatenate([w1, up[0:th - 8]], axis=0)
    s2 = jnp.concatenate([w2, w1, up[0:th - 16]], axis=0)
    prev_ref[c] = up[th - 16:th]
    return bc_ref[c] + s2 * wc_ref[c, 0:1, :] + s1 * wc_ref[c, 1:2, :] + up * wc_ref[c, 2:3, :]


def _in_sample_kernel(x_ref, mod_ref, g1_ref, wq_ref, wkv_ref, wf_ref, wus_ref, bf_ref, lng_ref, lnb_ref,
                      gmat_ref, wst_ref, bs_ref, same_ref,
                      q_ref, k_ref, v_ref, lf_ref, cq_ref, svn_ref, sgu_ref):
    n = x_ref.shape[0]
    h = _rms_mod(x_ref[...], g1_ref[...], mod_ref[0], mod_ref[1]).astype(BF16)
    q_ref[...] = jnp.dot(h, wq_ref[...], preferred_element_type=F32) * ATT_SCALE
    kv = jnp.dot(h, wkv_ref[...], preferred_element_type=F32)
    k_ref[...] = kv[:, 0:ATT_W]
    v_ref[...] = kv[:, ATT_W:2 * ATT_W]
    lf = _log_sigmoid(jnp.dot(h, wf_ref[...], preferred_element_type=F32) + bf_ref[...])
    lf_ref[...] = lf

    same = same_ref[...]
    hi, mid, lo = _split3(lf)
    tri = same.astype(BF16)
    d = functools.partial(jnp.dot, preferred_element_type=F32)
    cq_ref[...] = d(tri, hi) + d(tri, mid) + d(tri, lo)

    us = jnp.dot(h, wus_ref[...], preferred_element_type=F32)
    u = jax.nn.gelu(us[:, 0:SGU_W])
    sv = jax.nn.gelu(us[:, SGU_W:2 * SGU_W])
    svn = _group_ln(sv, gmat_ref[...], lng_ref[...], lnb_ref[...])
    svn_ref[...] = svn
    r_i = lax.broadcasted_iota(jnp.int32, (n, n), 0)
    c_i = lax.broadcasted_iota(jnp.int32, (n, n), 1)
    pick_rows = jnp.where(c_i == r_i % DEC_T, 1.0, 0.0).astype(BF16)
    pick_cols = jnp.where(r_i == c_i % DEC_T, 1.0, 0.0).astype(BF16)
    wm = []
    for g in range(N_GROUPS):
        rows = jnp.dot(pick_rows, wst_ref[g].astype(BF16), preferred_element_type=F32)
        tiled = jnp.dot(rows.astype(BF16), pick_cols, preferred_element_type=F32)
        wm.append(jnp.where(same > 0, tiled, 0.0).astype(BF16))
    sgu_ref[...] = _sgu_pairs(wm, svn.astype(BF16), u, bs_ref[...]).astype(BF16)


def _in_sample(x2, mod_rows, g1, wq, wkv, wf, wus, bf_row, lng, lnb, gmat, wst, bs_full, same):
    n, d = x2.shape
    args = (x2, mod_rows, g1, wq, wkv, wf, wus, bf_row, lng, lnb, gmat, wst, bs_full, same)
    out_shape = (jax.ShapeDtypeStruct((n, ATT_W), F32), jax.ShapeDtypeStruct((n, ATT_W), F32),
                 jax.ShapeDtypeStruct((n, ATT_W), F32), jax.ShapeDtypeStruct((n, LANES), F32),
                 jax.ShapeDtypeStruct((n, LANES), F32), jax.ShapeDtypeStruct((n, SGU_W), F32),
                 jax.ShapeDtypeStruct((n, SGU_W), BF16))
    return pl.pallas_call(
        _in_sample_kernel, grid=(1,),
        in_specs=[_const_spec(a.shape) for a in args],
        out_specs=tuple(pl.BlockSpec(o.shape, lambda i: (0, 0)) for o in out_shape),
        out_shape=out_shape,
        compiler_params=_cparams(("arbitrary",)),
        name="sample_in_proj",
    )(*args)


def _decode_begin(q, kn, vn, cqcol, cqmat_ref, bb):
    t_new = q.shape[0]
    rows = t_new * N_HEADS
    sub = lax.broadcasted_iota(jnp.int32, (N_HEADS, ATT_W), 0)
    lane_head = lax.broadcasted_iota(jnp.int32, (N_HEADS, ATT_W), 1) // HEAD_DIM
    own = sub == lane_head
    qbd = jnp.concatenate(
        [jnp.where(own, jnp.broadcast_to(q[t:t + 1, :], (N_HEADS, ATT_W)), 0.0) for t in range(t_new)], axis=0)
    rt = lax.broadcasted_iota(jnp.int32, (rows, 1), 0) // N_HEADS
    s_new = []
    for t2 in range(t_new):
        sc = jnp.sum(qbd * kn[t2:t2 + 1, :], axis=1, keepdims=True) + cqcol - cqmat_ref[bb, :, t2:t2 + 1]
        s_new.append(jnp.where(rt >= t2, sc, -jnp.inf))
    m = s_new[0]
    for t2 in range(1, t_new):
        m = jnp.maximum(m, s_new[t2])
    l = jnp.zeros((rows, 1), F32)
    acc = jnp.zeros((rows, ATT_W), F32)
    for t2 in range(t_new):
        pr = jnp.exp(s_new[t2] - m)
        l = l + pr
        acc = acc + pr * vn[t2:t2 + 1, :]
    return dict(qbd_b=qbd.astype(BF16), cqcol=cqcol, own=own, t_new=t_new,
                m=m, l=l, acc=acc, c_run=jnp.zeros((N_HEADS, PAGE), F32))


def _decode_scores(st, kbuf, lbuf, slot, uo, valid):
    g_pages = kbuf.shape[1]
    t_new = st["t_new"]
    c_run = st["c_run"]
    lp = lbuf[slot].reshape(g_pages * N_HEADS, PAGE)
    r = _dot3(lp, uo)
    sufs = [None] * g_pages
    for i in reversed(range(g_pages)):
        sufs[i] = jnp.where(valid[i], r[i * N_HEADS:(i + 1) * N_HEADS, 0:PAGE] + c_run, -jnp.inf)
        c_run = c_run + r[i * N_HEADS:(i + 1) * N_HEADS, PAGE:2 * PAGE]
    scs = []
    for i in range(0, g_pages, 2):
        kt2 = jnp.concatenate([kbuf[slot, i].astype(BF16), kbuf[slot, i + 1].astype(BF16)], axis=1)
        sc = jnp.dot(st["qbd_b"], kt2, preferred_element_type=F32)
        bias = jnp.concatenate([jnp.concatenate([sufs[i]] * t_new, axis=0),
                                jnp.concatenate([sufs[i + 1]] * t_new, axis=0)], axis=1)
        scs.append(sc + bias + st["cqcol"])
    return scs, dict(st, c_run=c_run)


def _decode_update(st, scs, vbuf, slot):
    nt = (((1,), (1,)), ((), ()))
    m, l, acc = st["m"], st["l"], st["acc"]
    m_new = m
    for sc in scs:
        m_new = jnp.maximum(m_new, jnp.max(sc, axis=1, keepdims=True))
    alpha = jnp.exp(m - m_new)
    l = alpha * l
    acc = alpha * acc
    for idx, sc in enumerate(scs):
        i = 2 * idx
        pr = jnp.exp(sc - m_new)
        l = l + jnp.sum(pr, axis=1, keepdims=True)
        vt2 = jnp.concatenate([vbuf[slot, i].astype(BF16), vbuf[slot, i + 1].astype(BF16)], axis=1)
        acc = acc + lax.dot_general(pr.astype(BF16), vt2, nt, preferred_element_type=F32)
    return dict(st, m=m_new, l=l, acc=acc)


def _decode_end(st, o_ref, bb):
    o = st["acc"] / st["l"]
    for t in range(st["t_new"]):
        blk = jnp.where(st["own"], o[t * N_HEADS:(t + 1) * N_HEADS, :], 0.0)
        o_ref[bb, t:t + 1, :] = jnp.sum(blk, axis=0, keepdims=True)


def _ffn_decode_kernel(pt_ref, x_ref, att_ref, sgu_ref, mod_ref, wo_ref, g2_ref, wu_ref, wcg_ref,
                       wcv_ref, bcg_ref, bcv_ref, wd_ref, gf_ref,
                       q_ref, kn_ref, vn_ref, cqcol_ref, cqmat_ref, uo_ref, kc_hbm, vc_hbm, lc_hbm,
                       y_ref, conv_ref, o_ref,
                       x1_ref, h2_ref, f_ref, pg_ref, pv_ref, kbuf, vbuf, lbuf, sem):
    s = pl.program_id(1)
    step = pl.program_id(0) * pl.num_programs(1) + s
    n_steps = pl.num_programs(0) * pl.num_programs(1)
    tm = x_ref.shape[1]
    n_chunks = wd_ref.shape[0]
    th = tm // 2
    nv = th // 8
    halves = (slice(0, th), slice(th, tm))
    n_pages = pt_ref.shape[1]
    g_pages = kbuf.shape[1]
    n_regions = 2 * n_chunks
    n_slots = kbuf.shape[0]
    ahead = n_slots - 2
    total = n_steps * n_regions

    def page_of(r, i):
        return n_pages - g_pages * (r + 1) + i

    def copies(n, slot):
        bb = n // n_regions
        r = n % n_regions
        out = []
        for i in range(g_pages):
            phys = pt_ref[bb, jnp.maximum(page_of(r, i), 0)]
            out.append(pltpu.make_async_copy(kc_hbm.at[phys], kbuf.at[slot, i], sem.at[0, slot]))
            out.append(pltpu.make_async_copy(vc_hbm.at[phys], vbuf.at[slot, i], sem.at[1, slot]))
            out.append(pltpu.make_async_copy(lc_hbm.at[phys], lbuf.at[slot, i], sem.at[2, slot]))
        return out

    @pl.when(step == 0)
    def _():
        for n0 in range(ahead):
            for c in copies(n0, n0):
                c.start()

    @pl.when(s == 0)
    def _():
        pg_ref[...] = jnp.zeros(pg_ref.shape, F32)
        pv_ref[...] = jnp.zeros(pv_ref.shape, F32)

    for i, rows in enumerate(halves):
        mix = (jnp.dot(att_ref[0, rows, :], wo_ref[0:ATT_W, :], preferred_element_type=F32)
               + jnp.dot(sgu_ref[0, rows, :], wo_ref[ATT_W:ATT_W + SGU_W, :], preferred_element_type=F32))
        x1 = _interleave_rows(x_ref[0, rows, :] + mod_ref[0, 2:3, :] * mix, nv)
        x1_ref[rows, :] = x1
        h2_ref[i] = _rms_mod(x1, g2_ref[...], mod_ref[0, 3:4, :], mod_ref[0, 4:5, :]).astype(BF16)
    f_ref[...] = jnp.zeros(f_ref.shape, F32)

    uo = uo_ref[...]
    st0 = _decode_begin(q_ref[0], kn_ref[0], vn_ref[0], cqcol_ref[0], cqmat_ref, 0)

    def trip(c, carry):
        st = dict(st0, m=carry[0], l=carry[1], acc=carry[2], c_run=carry[3])
        regions = [2 * c, 2 * c + 1]
        ns_ = [step * n_regions + r for r in regions]
        slots = [n % n_slots for n in ns_]
        for n in ns_:
            @pl.when(n + ahead < total)
            def _(n=n):
                for cp in copies(n + ahead, (n + ahead) % n_slots):
                    cp.start()
        for n, slot in zip(ns_, slots):
            for cp in copies(n, slot):
                cp.wait()
        gate = pl.ds(pl.multiple_of(c * FC, FC), FC)
        value = pl.ds(pl.multiple_of((n_chunks + c) * FC, FC), FC)
        ups = [(jnp.dot(h2_ref[i], wu_ref[:, gate], preferred_element_type=F32),
                jnp.dot(h2_ref[i], wu_ref[:, value], preferred_element_type=F32)) for i in range(2)]
        scs = []
        for r, slot in zip(regions, slots):
            valid = [page_of(r, k) >= 0 for k in range(g_pages)]
            sc, st = _decode_scores(st, kbuf, lbuf, slot, uo, valid)
            scs.append(sc)

        def down(i):
            cg = _causal_conv3(ups[i][0], pg_ref, c, wcg_ref, bcg_ref)
            cv = _causal_conv3(ups[i][1], pv_ref, c, wcv_ref, bcv_ref)
            act = (jax.nn.silu(cg) * cv).astype(BF16)
            f_ref[i] += jnp.dot(act, wd_ref[c], preferred_element_type=F32)

        down(0)
        for sc, slot in zip(scs, slots):
            st = _decode_update(st, sc, vbuf, slot)
        down(1)
        return st["m"], st["l"], st["acc"], st["c_run"]

    m, l, acc, c_run = lax.fori_loop(0, n_chunks, trip, (st0["m"], st0["l"], st0["acc"], st0["c_run"]))
    _decode_end(dict(st0, m=m, l=l, acc=acc, c_run=c_run), o_ref, 0)

    for i, rows in enumerate(halves):
        x2 = x1_ref[rows, :] + mod_ref[0, 5:6, :] * f_ref[i]
        y = x2 * lax.rsqrt(jnp.mean(x2 * x2, axis=-1, keepdims=True) + EPS) * gf_ref[...]
        y_ref[0, rows, :] = _deinterleave_rows(y, nv)
    fc = pg_ref.shape[2]
    for c in range(n_chunks):
        for k, row in enumerate((7, 15)):
            conv_ref[0, k:k + 1, c * fc:(c + 1) * fc] = pg_ref[c, row:row + 1, :]
            conv_ref[0, k:k + 1, (n_chunks + c) * fc:(n_chunks + c + 1) * fc] = pv_ref[c, row:row + 1, :]


def _ffn_prompt_and_decode(page_table, x, att, sgu, mod3, wo, g2, wu, wcg, wcv, bcg, bcv, wd, gf,
                           q3, k3, v3, cqcol, cqmat, uo, kc, vc, lc):
    b, s, d = x.shape
    tm = TM_FFN
    ns = s // tm
    n_chunks, fc, _ = wd.shape
    dff = n_chunks * fc
    nb, t_new, _ = q3.shape
    g = PAGES_PER_GROUP
    assert nb == b * ns and -(-page_table.shape[1] // g) == 2 * n_chunks and g % 2 == 0
    rows = t_new * N_HEADS
    consts = (wo, g2, wu, wcg, wcv, bcg, bcv, wd, gf)
    row_blk = lambda w: pl.BlockSpec((1, tm, w), lambda i, j, pt: (i, j, 0))
    smp_blk = lambda r, w: pl.BlockSpec((1, r, w), lambda i, j, pt: (i * ns + j, 0, 0))
    const_blk = lambda a: pl.BlockSpec(a.shape, lambda i, j, pt: (0,) * a.ndim, pipeline_mode=pl.Buffered(1))
    grid_spec = pltpu.PrefetchScalarGridSpec(
        num_scalar_prefetch=1,
        grid=(b, ns),
        in_specs=[row_blk(d), row_blk(ATT_W), row_blk(SGU_W),
                  pl.BlockSpec((1, N_MOD, d), lambda i, j, pt: (i, 0, 0))]
                 + [const_blk(a) for a in consts]
                 + [smp_blk(t_new, ATT_W), smp_blk(t_new, ATT_W), smp_blk(t_new, ATT_W),
                    smp_blk(rows, 1), smp_blk(rows, t_new), const_blk(uo),
                    pl.BlockSpec(memory_space=pl.ANY), pl.BlockSpec(memory_space=pl.ANY),
                    pl.BlockSpec(memory_space=pl.ANY)],
        out_specs=(row_blk(d), pl.BlockSpec((1, 2, 2 * dff), lambda i, j, pt: (i, 0, 0)),
                   smp_blk(t_new, ATT_W)),
        scratch_shapes=[pltpu.VMEM((tm, d), F32),
                        pltpu.VMEM((2, tm // 2, d), BF16),
                        pltpu.VMEM((2, tm // 2, d), F32),
                        pltpu.VMEM((n_chunks, 16, fc), F32),
                        pltpu.VMEM((n_chunks, 16, fc), F32),
                        pltpu.VMEM((DECODE_SLOTS, g, ATT_W, PAGE), F32),
                        pltpu.VMEM((DECODE_SLOTS, g, ATT_W, PAGE), F32),
                        pltpu.VMEM((DECODE_SLOTS, g, N_HEADS, PAGE), F32),
                        pltpu.SemaphoreType.DMA((3, DECODE_SLOTS))],
    )
    return pl.pallas_call(
        _ffn_decode_kernel, grid_spec=grid_spec,
        out_shape=(jax.ShapeDtypeStruct((b, s, d), F32), jax.ShapeDtypeStruct((b, 2, 2 * dff), F32),
                   jax.ShapeDtypeStruct((nb, t_new, ATT_W), F32)),
        compiler_params=_cparams(("arbitrary", "arbitrary")),
        name="prompt_ffn_decode",
    )(page_table, x, att, sgu, mod3, *consts, q3, k3, v3, cqcol, cqmat, uo, kc, vc, lc)


def _ffn_sample_kernel(x_ref, att_ref, sgu_ref, mod_ref, wo_ref, g2_ref, wug_ref, wuv_ref, wcg_ref, wcv_ref,
                       bcg_ref, bcv_ref, stg_ref, stv_ref, e1_ref, e2_ref, wd_ref, gf_ref,
                       y_ref, upg_ref, upv_ref,
                       x1_ref, h2_ref, acc_ref):
    j = pl.program_id(0)
    n = x_ref.shape[0]
    t_new = 4

    @pl.when(j == 0)
    def _():
        mix = (jnp.dot(att_ref[...].astype(BF16), wo_ref[0:ATT_W, :], preferred_element_type=F32)
               + jnp.dot(sgu_ref[...], wo_ref[ATT_W:ATT_W + SGU_W, :], preferred_element_type=F32))
        x1 = x_ref[...] + mod_ref[2] * mix
        x1_ref[...] = x1
        h2_ref[...] = _rms_mod(x1, g2_ref[...], mod_ref[3], mod_ref[4]).astype(BF16)
        acc_ref[...] = jnp.zeros_like(acc_ref)

    h2 = h2_ref[...]
    tpos = lax.broadcasted_iota(jnp.int32, (n, FC), 0) % t_new

    def place(e_ref, parts):
        d = functools.partial(jnp.dot, preferred_element_type=F32)
        return d(e_ref[...], parts[0]) + d(e_ref[...], parts[1]) + d(e_ref[...], parts[2])

    def conv(up, wc_ref, bc_ref, st_ref):
        parts = _split3(st_ref[...])
        s1 = jnp.where(tpos >= 1, pltpu.roll(up, 1, axis=0), place(e1_ref, parts))
        s2 = jnp.where(tpos >= 2, pltpu.roll(up, 2, axis=0), place(e2_ref, parts))
        return bc_ref[0] + s2 * wc_ref[0, 0:1, :] + s1 * wc_ref[0, 1:2, :] + up * wc_ref[0, 2:3, :]

    upg = jnp.dot(h2, wug_ref[...], preferred_element_type=F32)
    upv = jnp.dot(h2, wuv_ref[...], preferred_element_type=F32)
    upg_ref[...] = upg
    upv_ref[...] = upv
    cg = conv(upg, wcg_ref, bcg_ref, stg_ref)
    cv = conv(upv, wcv_ref, bcv_ref, stv_ref)
    act = (jax.nn.silu(cg) * cv).astype(BF16)
    acc_ref[...] += jnp.dot(act, wd_ref[0], preferred_element_type=F32)

    @pl.when(j == pl.num_programs(0) - 1)
    def _():
        x2 = x1_ref[...] + mod_ref[5] * acc_ref[...]
        y_ref[...] = x2 * lax.rsqrt(jnp.mean(x2 * x2, axis=-1, keepdims=True) + EPS) * gf_ref[...]


def _ffn_sample(x2, att, sgu, mod_rows, wo, g2, wu, wcg, wcv, bcg, bcv, st2, e1, e2, wd, gf):
    n, d = x2.shape
    nf = wd.shape[0]
    dff = nf * FC
    full = lambda a: _const_spec(a.shape)
    colc = lambda r: pl.BlockSpec((r, FC), lambda j: (0, j))
    chunk = lambda a: pl.BlockSpec((1,) + a.shape[1:], lambda j: (j, 0, 0))
    in_specs = [full(x2), full(att), full(sgu), full(mod_rows), full(wo), full(g2),
                colc(d), pl.BlockSpec((d, FC), lambda j: (0, nf + j)),
                chunk(wcg), chunk(wcv), chunk(bcg), chunk(bcv),
                pl.BlockSpec((st2.shape[0], FC), lambda j: (0, j)),
                pl.BlockSpec((st2.shape[0], FC), lambda j: (0, nf + j)),
                full(e1), full(e2),
                chunk(wd), full(gf)]
    return pl.pallas_call(
        _ffn_sample_kernel, grid=(nf,),
        in_specs=in_specs,
        out_specs=(pl.BlockSpec((n, d), lambda j: (0, 0)), colc(n), colc(n)),
        out_shape=(jax.ShapeDtypeStruct((n, d), F32), jax.ShapeDtypeStruct((n, dff), F32),
                   jax.ShapeDtypeStruct((n, dff), F32)),
        scratch_shapes=[pltpu.VMEM((n, d), F32), pltpu.VMEM((n, d), BF16), pltpu.VMEM((n, d), F32)],
        compiler_params=_cparams(("arbitrary",)),
        name="sample_ffn",
    )(x2, att, sgu, mod_rows, wo, g2, wu, wu, wcg, wcv, bcg, bcv, st2, st2, e1, e2, wd, gf)


def kernel(x_prompt, x_sample, c_prompt, c_sample, cache_k, cache_v, cache_logf, state_conv, page_table,
           w_ada, b_ada, norm1_g, w_in, b_f, ln_v_g, ln_v_b, w_s, b_s, w_o, norm2_g, w_up, w_conv, b_conv,
           w_down, final_g):
    bp, s, d = x_prompt.shape
    bs, t_new, _ = x_sample.shape
    n_s = bs * t_new
    dff = w_down.shape[1]
    n_phys = cache_k.shape[1]
    assert w_ada.shape[0] == 1, "single layer"
    assert s % TM_IN == 0 and s % TQ == 0 and s % TM_FFN == 0 and TM_IN % TQ == 0
    assert dff % FC == 0 and n_s == CHUNK and t_new == 4
    assert bs == bp * (s // TM_FFN), "one decode sample rides along with each FFN grid step"

    wi = w_in[0]
    k0, v0, f0 = ATT_W, 2 * ATT_W, 3 * ATT_W
    u0 = f0 + N_HEADS
    wq = wi[:, 0:k0].astype(BF16)
    wkv = wi[:, k0:f0].astype(BF16)
    wf = wi[:, f0:u0]
    wus = wi[:, u0:].astype(BF16)
    wkvf_t = jnp.concatenate([wi[:, k0:f0], wf], axis=1).T.astype(BF16)
    wf_pad = jnp.pad(wf, ((0, 0), (0, LANES - N_HEADS))).astype(BF16)
    bf_col = b_f[0].reshape(N_HEADS, 1)
    bf_row = jnp.pad(b_f[0].reshape(1, N_HEADS), ((0, 0), (0, LANES - N_HEADS)))
    lng = ln_v_g[0].reshape(1, SGU_W)
    lnb = ln_v_b[0].reshape(1, SGU_W)
    gidx = np.arange(SGU_W) // SGU_DIM
    gmat = jnp.asarray(np.where(gidx[:, None] == gidx[None, :], 1.0 / SGU_DIM, 0.0), BF16)
    ws_b = w_s[0]
    bs_full = jnp.repeat(b_s[0].T, SGU_DIM, axis=1)
    wst = w_s[0]
    bs_full_s = jnp.tile(jnp.repeat(b_s[0][:, :t_new].T, SGU_DIM, axis=1), (bs, 1))
    r_idx = np.arange(n_s)
    same = jnp.asarray((r_idx[:, None] // t_new == r_idx[None, :] // t_new)
                       & (r_idx[:, None] >= r_idx[None, :]), F32)
    pos = np.arange(TM_IN)
    tri = jnp.asarray(np.concatenate([pos[:, None] <= pos[None, :],
                                      np.ones((TM_IN, LANES), bool)], axis=1), BF16)
    pp = np.arange(PAGE)
    uo = jnp.asarray(np.concatenate([pp[:, None] > pp[None, :],
                                     np.ones((PAGE, PAGE), bool)], axis=1), BF16)
    wo = w_o[0].astype(BF16)
    nfc = dff // FC
    chunked = lambda a: a.reshape(a.shape[0], nfc, FC).transpose(1, 0, 2)
    wu = w_up[0].astype(BF16)
    wcg, wcv = chunked(w_conv[0][:, :dff]), chunked(w_conv[0][:, dff:])
    bcg, bcv = chunked(b_conv[:, :dff]), chunked(b_conv[:, dff:])
    wd = w_down[0].astype(BF16).reshape(nfc, FC, d)
    g1 = norm1_g
    g2 = norm2_g
    gf = final_g.reshape(1, d)

    mod = _ada(jnp.concatenate([c_prompt, c_sample], axis=0), w_ada[0], b_ada)
    mod_p = mod[:bp].reshape(bp, N_MOD, d)
    mod_s = jnp.repeat(mod[bp:].reshape(bs, N_MOD, d), t_new, axis=0).transpose(1, 0, 2)

    q, kt, vt, ktb, vtb, lft, cumt, cum, sgu = _in_prompt(
        x_prompt, mod_p, g1, wq, wkvf_t, wus, bf_col, lng, lnb, gmat, ws_b, bs_full, tri)
    att = _attn_prompt(q, ktb, vtb, cumt, cum)
    new_k_p = kt.reshape(1, bp, N_HEADS, HEAD_DIM, s).transpose(0, 1, 4, 2, 3)
    new_v_p = vt.reshape(1, bp, N_HEADS, HEAD_DIM, s).transpose(0, 1, 4, 2, 3)
    new_lf_p = lft.transpose(0, 2, 1)[None]

    x2 = x_sample.reshape(n_s, d)
    q_s, k_s, v_s, lf_s, cq_s, svn_s, sgu_s = _in_sample(
        x2, mod_s, g1, wq, wkv, wf_pad, wus, bf_row, lng, lnb, gmat, wst, bs_full_s, same)
    cq3 = cq_s[:, :N_HEADS].reshape(bs, t_new, N_HEADS)
    cqcol = cq3.reshape(bs, t_new * N_HEADS, 1)
    cqmat = jnp.tile(cq3.transpose(0, 2, 1), (1, t_new, 1))
    kc = cache_k[0].transpose(0, 2, 3, 1).reshape(n_phys, ATT_W, PAGE)
    vc = cache_v[0].transpose(0, 2, 3, 1).reshape(n_phys, ATT_W, PAGE)
    lc = cache_logf[0].transpose(0, 2, 1)
    y_prompt, conv_p, att_s = _ffn_prompt_and_decode(
        page_table, x_prompt, att, sgu, mod_p, wo, g2, wu, wcg, wcv, bcg, bcv, wd, gf,
        q_s.reshape(bs, t_new, ATT_W), k_s.reshape(bs, t_new, ATT_W), v_s.reshape(bs, t_new, ATT_W),
        cqcol, cqmat, uo, kc, vc, lc)
    new_conv_p = conv_p[None]

    st2 = state_conv[0].reshape(2 * bs, 2 * dff)
    tok, src = np.arange(n_s)[:, None], np.arange(2 * bs)[None, :]
    smp, t_in = tok // t_new, tok % t_new
    e1 = jnp.asarray((t_in == 0) & (src == 2 * smp + 1), BF16)
    e2 = jnp.asarray((t_in <= 1) & (src == 2 * smp + t_in), BF16)
    y_s, upg, upv = _ffn_sample(x2, att_s.reshape(n_s, ATT_W), sgu_s, mod_s, wo, g2, wu, wcg, wcv,
                                bcg, bcv, st2, e1, e2, wd, gf)
    up = jnp.concatenate([upg, upv], axis=1).reshape(bs, t_new, 2 * dff)

    return (y_prompt, y_s.reshape(bs, t_new, d),
            new_k_p, new_v_p, new_lf_p, new_conv_p,
            k_s.reshape(1, bs, t_new, N_HEADS, HEAD_DIM), v_s.reshape(1, bs, t_new, N_HEADS, HEAD_DIM),
            lf_s[:, :N_HEADS].reshape(1, bs, t_new, N_HEADS),
            svn_s.reshape(1, bs, t_new, N_GROUPS, SGU_DIM),
            up[:, t_new - 2:, :][None])
```

```python
import functools

import jax
import jax.numpy as jnp
import numpy as np
from jax import lax
from jax.experimental import pallas as pl
from jax.experimental.pallas import tpu as pltpu

F32 = jnp.float32
BF16 = jnp.bfloat16

N_HEADS = 8
HEAD_DIM = 64
ATT_W = N_HEADS * HEAD_DIM
N_GROUPS = 8
SGU_DIM = 64
SGU_W = N_GROUPS * SGU_DIM
CHUNK = 128
PAGE = 128
DEC_T = 4
N_MOD = 6
EPS = 1e-6
ATT_SCALE = HEAD_DIM ** -0.5
LOG2E = 1.4426950408889634
LANES = 128
PAIR_W = 2 * HEAD_DIM
VMEM_LIMIT = 58 * 1024 * 1024

TM_IN = 512
TQ = 512
ATTN_SCORE_SLOTS = 3
TM_FFN = 512
FC = 256
PAGES_PER_GROUP = 6
DECODE_SLOTS = 6


def _cparams(sem):
    return pltpu.CompilerParams(dimension_semantics=sem, vmem_limit_bytes=VMEM_LIMIT)


def _const_spec(shape):
    nd = len(shape)
    return pl.BlockSpec(shape, lambda *_: (0,) * nd, pipeline_mode=pl.Buffered(1))


def _split3(x):
    hi = x.astype(BF16)
    r1 = x - hi.astype(F32)
    mid = r1.astype(BF16)
    lo = (r1 - mid.astype(F32)).astype(BF16)
    return hi, mid, lo


def _dot3(x, w):
    hi, mid, lo = _split3(x)
    d = functools.partial(jnp.dot, preferred_element_type=F32)
    return d(hi, w) + d(mid, w) + d(lo, w)


def _rms_mod(x, g, shift, scale):
    y = x * lax.rsqrt(jnp.mean(x * x, axis=-1, keepdims=True) + EPS)
    return (y * g) * (1.0 + scale) + shift


def _mod_rows(mod_ref, i):
    d = mod_ref.shape[1] // N_MOD
    return mod_ref[:, i * d:(i + 1) * d]


def _log_sigmoid(z):
    return jnp.minimum(z, 0.0) - jnp.log1p(jnp.exp(-jnp.abs(z)))


def _group_ln(sv, gmat, ln_g, ln_b):
    mu = jnp.dot(sv.astype(BF16), gmat, preferred_element_type=F32)
    d = sv - mu
    var = jnp.dot((d * d).astype(BF16), gmat, preferred_element_type=F32)
    return d * lax.rsqrt(var + EPS) * ln_g + ln_b


def _sgu_pairs(wm, svn_b, u, bias):
    lane = lax.broadcasted_iota(jnp.int32, (CHUNK, LANES), 1)
    outs = []
    for p in range(N_GROUPS // 2):
        rhs = svn_b[:, p * LANES:(p + 1) * LANES]
        lo = jnp.dot(wm[2 * p], rhs, preferred_element_type=F32)
        hi = jnp.dot(wm[2 * p + 1], rhs, preferred_element_type=F32)
        mixed = jnp.where(lane < SGU_DIM, lo, hi) + bias[:, p * LANES:(p + 1) * LANES]
        outs.append(u[:, p * LANES:(p + 1) * LANES] * mixed)
    return jnp.concatenate(outs, axis=1)


def _ada_kernel(c_ref, w_ref, b_ref, o_ref):
    a = jax.nn.silu(c_ref[...])
    o_ref[...] = jnp.dot(a.astype(BF16), w_ref[...].astype(BF16),
                         preferred_element_type=F32) + b_ref[...]


def _ada(c_all, w_ada, b_ada):
    n, d = c_all.shape
    cols = w_ada.shape[1]
    bn = d
    return pl.pallas_call(
        _ada_kernel,
        grid=(cols // bn,),
        in_specs=[pl.BlockSpec((n, d), lambda j: (0, 0)),
                  pl.BlockSpec((d, bn), lambda j: (0, j)),
                  pl.BlockSpec((1, bn), lambda j: (0, j))],
        out_specs=pl.BlockSpec((n, bn), lambda j: (0, j)),
        out_shape=jax.ShapeDtypeStruct((n, cols), F32),
        compiler_params=_cparams(("arbitrary",)),
        name="ada_ln",
    )(c_all, w_ada, b_ada)


def _in_prompt_kernel(x_ref, mod_ref, g1_ref, wq_ref, wkvf_ref, wus_ref, bf_ref, lng_ref, lnb_ref,
                      gmat_ref, ws_ref, bs_ref, tri_ref,
                      q_ref, kt_ref, vt_ref, ktb_ref, vtb_ref, lft_ref, cumt_ref, cum_ref, sgu_ref,
                      carry_ref):
    s = pl.program_id(1)
    tm = x_ref.shape[1]

    @pl.when(s == 0)
    def _():
        carry_ref[...] = jnp.zeros_like(carry_ref)

    h = _rms_mod(x_ref[0], g1_ref[...], mod_ref[0, 0:1, :], mod_ref[0, 1:2, :]).astype(BF16)

    q = jnp.dot(h, wq_ref[...], preferred_element_type=F32)
    q_ref[0] = (q * (ATT_SCALE * LOG2E)).astype(BF16)

    kvf = lax.dot_general(wkvf_ref[...], h, (((1,), (1,)), ((), ())), preferred_element_type=F32)
    kt = kvf[0:ATT_W]
    vt = kvf[ATT_W:2 * ATT_W]
    kt_ref[0] = kt
    vt_ref[0] = vt
    ktb_ref[0, 0] = kt.astype(BF16)
    vtb_ref[0, 0] = vt.astype(BF16)
    lft = _log_sigmoid(kvf[2 * ATT_W:2 * ATT_W + N_HEADS] + bf_ref[...])
    lft_ref[0] = lft

    r = _dot3(lft, tri_ref[...])
    carry = carry_ref[...]
    cumt = r[:, 0:tm] + jnp.concatenate([carry] * (tm // LANES), axis=1)
    carry_ref[...] = carry + r[:, tm:tm + LANES]
    cumt = cumt * LOG2E
    cumt_ref[0, 0] = cumt
    pad = jnp.zeros((LANES - N_HEADS, LANES), F32)
    for c in range(tm // LANES):
        blk = jnp.concatenate([cumt[:, c * LANES:(c + 1) * LANES], pad], axis=0)
        cum_ref[0, c * LANES:(c + 1) * LANES, :] = blk.T[:, 0:N_HEADS]

    us = jnp.dot(h, wus_ref[...], preferred_element_type=F32)
    u = jax.nn.gelu(us[:, 0:SGU_W])
    sv = jax.nn.gelu(us[:, SGU_W:2 * SGU_W])
    svn = _group_ln(sv, gmat_ref[...], lng_ref[...], lnb_ref[...]).astype(BF16)

    row = lax.broadcasted_iota(jnp.int32, (CHUNK, CHUNK), 0)
    col = lax.broadcasted_iota(jnp.int32, (CHUNK, CHUNK), 1)
    wm = [jnp.where(row >= col, ws_ref[g], 0.0).astype(BF16) for g in range(N_GROUPS)]
    bias = bs_ref[...]
    for c in range(tm // CHUNK):
        sl = slice(c * CHUNK, (c + 1) * CHUNK)
        sgu_ref[0, sl, :] = _sgu_pairs(wm, svn[sl], u[sl], bias).astype(BF16)


def _in_prompt(x, mod3, g1, wq, wkvf, wus, bf_col, lng, lnb, gmat, ws_b, bs_full, tri):
    b, s, d = x.shape
    tm = TM_IN
    ns = s // tm
    grid = (b, ns)
    row_blk = lambda w: pl.BlockSpec((1, tm, w), lambda i, j: (i, j, 0))
    colT_blk = lambda r: pl.BlockSpec((1, r, tm), lambda i, j: (i, 0, j))
    tiledT_blk = lambda r: pl.BlockSpec((1, 1, r, tm), lambda i, j: (i, j, 0, 0))
    out_shape = (
        jax.ShapeDtypeStruct((b, s, ATT_W), BF16),
        jax.ShapeDtypeStruct((b, ATT_W, s), F32),
        jax.ShapeDtypeStruct((b, ATT_W, s), F32),
        jax.ShapeDtypeStruct((b, ns, ATT_W, tm), BF16),
        jax.ShapeDtypeStruct((b, ns, ATT_W, tm), BF16),
        jax.ShapeDtypeStruct((b, N_HEADS, s), F32),
        jax.ShapeDtypeStruct((b, ns, N_HEADS, tm), F32),
        jax.ShapeDtypeStruct((b, s, N_HEADS), F32),
        jax.ShapeDtypeStruct((b, s, SGU_W), BF16),
    )
    out_specs = (row_blk(ATT_W), colT_blk(ATT_W), colT_blk(ATT_W), tiledT_blk(ATT_W), tiledT_blk(ATT_W),
                 colT_blk(N_HEADS), tiledT_blk(N_HEADS), row_blk(N_HEADS), row_blk(SGU_W))
    in_specs = [row_blk(d),
                pl.BlockSpec((1, N_MOD, d), lambda i, j: (i, 0, 0)),
                _const_spec(g1.shape), _const_spec(wq.shape), _const_spec(wkvf.shape), _const_spec(wus.shape),
                _const_spec(bf_col.shape), _const_spec(lng.shape), _const_spec(lnb.shape),
                _const_spec(gmat.shape), _const_spec(ws_b.shape), _const_spec(bs_full.shape),
                _const_spec(tri.shape)]
    return pl.pallas_call(
        _in_prompt_kernel, grid=grid, in_specs=in_specs, out_specs=out_specs, out_shape=out_shape,
        scratch_shapes=[pltpu.VMEM((N_HEADS, LANES), F32)],
        compiler_params=_cparams(("arbitrary", "arbitrary")),
        name="prompt_in_proj",
    )(x, mod3, g1, wq, wkvf, wus, bf_col, lng, lnb, gmat, ws_b, bs_full, tri)


def _attn_kernel(q_ref, kt_ref, vt_ref, cumt_ref, cum_ref, o_ref, qm_ref, s_ref, cq_ref, m_ref, acc_ref):
    qi = pl.program_id(1)
    tq = q_ref.shape[1]
    tk = kt_ref.shape[3]
    qs = qi * tq
    j_diag = (qs + tq - 1) // tk
    n_slots = s_ref.shape[0]
    ahead = n_slots - 1
    lane = lax.broadcasted_iota(jnp.int32, (tq, LANES), 1)
    qpos = qs + lax.broadcasted_iota(jnp.int32, (tq, tk), 0)
    kcol = lax.broadcasted_iota(jnp.int32, (tq, tk), 1)
    ones_half = jnp.ones((HEAD_DIM, tk), BF16)
    nt = (((1,), (1,)), ((), ()))

    for p in range(N_HEADS // 2):
        qp = q_ref[0, :, p * PAIR_W:(p + 1) * PAIR_W].astype(F32)
        qm_ref[2 * p] = jnp.where(lane < HEAD_DIM, qp, 0.0).astype(BF16)
        qm_ref[2 * p + 1] = jnp.where(lane >= HEAD_DIM, qp, 0.0).astype(BF16)
    for h in range(N_HEADS):
        cq_ref[h] = jnp.broadcast_to(cum_ref[0, :, h:h + 1], (tq, LANES))
    m_ref[...] = jnp.full(m_ref.shape, -jnp.inf, F32)
    acc_ref[...] = jnp.zeros(acc_ref.shape, F32)

    def scores(j, h):
        p = h // 2
        kt = kt_ref[0, j, p * PAIR_W:(p + 1) * PAIR_W, :]
        s_ref[h % n_slots] = jnp.dot(qm_ref[h], kt, preferred_element_type=F32)

    def softmax_pv(j, h, masked):
        p, half = divmod(h, 2)
        t = s_ref[h % n_slots] - cumt_ref[0, j, h:h + 1, :]
        if masked:
            t = jnp.where(j * tk + kcol <= qpos, t, -jnp.inf)
        cq = cq_ref[h]
        m = m_ref[h]
        m_new = jnp.maximum(m, jnp.max(t, axis=1, keepdims=True) + cq)
        alpha = jnp.exp2(m - m_new)
        d = cq - m_new
        pr = jnp.exp2(t + jnp.concatenate([d] * (tk // LANES), axis=1)).astype(BF16)
        vt = vt_ref[0, j, p * PAIR_W:(p + 1) * PAIR_W, :]
        if half == 0:
            vaug = jnp.concatenate([vt[0:HEAD_DIM], ones_half], axis=0)
        else:
            vaug = jnp.concatenate([ones_half, vt[HEAD_DIM:PAIR_W]], axis=0)
        acc_ref[h] = alpha * acc_ref[h] + lax.dot_general(pr, vaug, nt, preferred_element_type=F32)
        m_ref[h] = m_new

    def key_tile(j, masked):
        for h in range(N_HEADS):
            softmax_pv(j, h, masked)
            nh = h + ahead
            if nh < N_HEADS:
                scores(j, nh)
            elif not masked:
                scores(j + 1, nh - N_HEADS)

    for h in range(ahead):
        scores(0, h)

    def loop_body(j, carry):
        key_tile(j, False)
        return carry

    lax.fori_loop(0, j_diag, loop_body, 0)
    key_tile(j_diag, True)

    for p in range(N_HEADS // 2):
        a0 = acc_ref[2 * p]
        a1 = acc_ref[2 * p + 1]
        o0 = a0 / pltpu.roll(a0, HEAD_DIM, axis=1)
        o1 = a1 / pltpu.roll(a1, HEAD_DIM, axis=1)
        o_ref[0, :, p * PAIR_W:(p + 1) * PAIR_W] = jnp.where(lane < HEAD_DIM, o0, o1).astype(BF16)


def _attn_prompt(q, ktb, vtb, cumt, cum):
    b, s, _ = q.shape
    nk, tk = ktb.shape[1], ktb.shape[3]
    tq = TQ
    return pl.pallas_call(
        _attn_kernel,
        grid=(b, s // tq),
        in_specs=[pl.BlockSpec((1, tq, ATT_W), lambda i, j: (i, j, 0)),
                  pl.BlockSpec((1, nk, ATT_W, tk), lambda i, j: (i, 0, 0, 0)),
                  pl.BlockSpec((1, nk, ATT_W, tk), lambda i, j: (i, 0, 0, 0)),
                  pl.BlockSpec((1, nk, N_HEADS, tk), lambda i, j: (i, 0, 0, 0)),
                  pl.BlockSpec((1, tq, N_HEADS), lambda i, j: (i, j, 0))],
        out_specs=pl.BlockSpec((1, tq, ATT_W), lambda i, j: (i, j, 0)),
        out_shape=jax.ShapeDtypeStruct((b, s, ATT_W), BF16),
        scratch_shapes=[pltpu.VMEM((N_HEADS, tq, PAIR_W), BF16),
                        pltpu.VMEM((ATTN_SCORE_SLOTS, tq, tk), F32),
                        pltpu.VMEM((N_HEADS, tq, LANES), F32),
                        pltpu.VMEM((N_HEADS, tq, LANES), F32),
                        pltpu.VMEM((N_HEADS, tq, PAIR_W), F32)],
        compiler_params=_cparams(("arbitrary", "arbitrary")),
        name="prompt_attention",
    )(q, ktb, vtb, cumt, cum)


def _transpose_rows8(blocks):
    sub = lax.broadcasted_iota(jnp.int32, blocks[0].shape, 0)
    a = list(blocks)
    for s in (4, 2, 1):
        keep = (sub & s) == 0
        b = list(a)
        for i in range(8):
            if i & s == 0:
                lo, hi = a[i], a[i + s]
                b[i] = jnp.where(keep, lo, pltpu.roll(hi, s, axis=0))
                b[i + s] = jnp.where(keep, pltpu.roll(lo, 8 - s, axis=0), hi)
        a = b
    return a


def _interleave_rows(x, nv):
    out = [None] * nv
    for jh in range(nv // 8):
        t = _transpose_rows8([x[r * nv + jh * 8:r * nv + jh * 8 + 8, :] for r in range(8)])
        for jl in range(8):
            out[8 * jh + jl] = t[jl]
    return jnp.concatenate(out, axis=0)


def _deinterleave_rows(y, nv):
    nb = nv // 8
    out = [None] * nv
    for jh in range(nb):
        t = _transpose_rows8([y[(8 * jh + jl) * 8:(8 * jh + jl) * 8 + 8, :] for jl in range(8)])
        for r in range(8):
            out[r * nb + jh] = t[r]
    return jnp.concatenate(out, axis=0)


def _causal_conv3(up, prev_ref, c, wc_ref, bc_ref):
    th = up.shape[0]
    first_row = lax.broadcasted_iota(jnp.int32, (8, up.shape[1]), 0) == 0

    def wrap(prev_blk, cur_blk):
        return jnp.where(first_row, pltpu.roll(prev_blk, 1, axis=0), pltpu.roll(cur_blk, 1, axis=0))

    w1 = wrap(prev_ref[c, 8:16, :], up[th - 8:th])
    w2 = wrap(prev_ref[c, 0:8, :], up[th - 16:th - 8])
    s1 = jnp.concatenate([w1, up[0:th - 8]], axis=0)
    s2 = jnp.concatenate([w2, w1, up[0:th - 16]], axis=0)
    prev_ref[c] = up[th - 16:th]
    return bc_ref[c] + s2 * wc_ref[c, 0:1, :] + s1 * wc_ref[c, 1:2, :] + up * wc_ref[c, 2:3, :]


def _in_sample_kernel(x_ref, mod_ref, g1_ref, wq_ref, wkv_ref, wf_ref, wus_ref, bf_ref, lng_ref, lnb_ref,
                      gmat_ref, wst_ref, bs_ref, same_ref,
                      q_ref, k_ref, v_ref, lf_ref, cq_ref, svn_ref, sgu_ref):
    n = x_ref.shape[0]
    h = _rms_mod(x_ref[...], g1_ref[...], _mod_rows(mod_ref, 0), _mod_rows(mod_ref, 1)).astype(BF16)
    q_ref[...] = jnp.dot(h, wq_ref[...], preferred_element_type=F32) * ATT_SCALE
    kv = jnp.dot(h, wkv_ref[...], preferred_element_type=F32)
    k_ref[...] = kv[:, 0:ATT_W]
    v_ref[...] = kv[:, ATT_W:2 * ATT_W]
    lf = _log_sigmoid(jnp.dot(h, wf_ref[...], preferred_element_type=F32) + bf_ref[...])
    lf_ref[...] = lf

    same = same_ref[...]
    hi, mid, lo = _split3(lf)
    tri = same.astype(BF16)
    d = functools.partial(jnp.dot, preferred_element_type=F32)
    cq_ref[...] = d(tri, hi) + d(tri, mid) + d(tri, lo)

    us = jnp.dot(h, wus_ref[...], preferred_element_type=F32)
    u = jax.nn.gelu(us[:, 0:SGU_W])
    sv = jax.nn.gelu(us[:, SGU_W:2 * SGU_W])
    svn = _group_ln(sv, gmat_ref[...], lng_ref[...], lnb_ref[...])
    svn_ref[...] = svn
    r_i = lax.broadcasted_iota(jnp.int32, (n, n), 0)
    c_i = lax.broadcasted_iota(jnp.int32, (n, n), 1)
    pick_rows = jnp.where(c_i == r_i % DEC_T, 1.0, 0.0).astype(BF16)
    pick_cols = jnp.where(r_i == c_i % DEC_T, 1.0, 0.0).astype(BF16)
    wm = []
    for g in range(N_GROUPS):
        rows = jnp.dot(pick_rows, wst_ref[g].astype(BF16), preferred_element_type=F32)
        tiled = jnp.dot(rows.astype(BF16), pick_cols, preferred_element_type=F32)
        wm.append(jnp.where(same > 0, tiled, 0.0).astype(BF16))
    sgu_ref[...] = _sgu_pairs(wm, svn.astype(BF16), u, bs_ref[...]).astype(BF16)


def _in_sample(x2, mod_rows, g1, wq, wkv, wf, wus, bf_row, lng, lnb, gmat, wst, bs_full, same):
    n, d = x2.shape
    args = (x2, mod_rows, g1, wq, wkv, wf, wus, bf_row, lng, lnb, gmat, wst, bs_full, same)
    out_shape = (jax.ShapeDtypeStruct((n, ATT_W), F32), jax.ShapeDtypeStruct((n, ATT_W), F32),
                 jax.ShapeDtypeStruct((n, ATT_W), F32), jax.ShapeDtypeStruct((n, LANES), F32),
                 jax.ShapeDtypeStruct((n, LANES), F32), jax.ShapeDtypeStruct((n, SGU_W), F32),
                 jax.ShapeDtypeStruct((n, SGU_W), BF16))
    return pl.pallas_call(
        _in_sample_kernel, grid=(1,),
        in_specs=[_const_spec(a.shape) for a in args],
        out_specs=tuple(pl.BlockSpec(o.shape, lambda i: (0, 0)) for o in out_shape),
        out_shape=out_shape,
        compiler_params=_cparams(("arbitrary",)),
        name="sample_in_proj",
    )(*args)


def _decode_begin(q, kn, vn, cqcol, cqmat_ref, bb):
    t_new = q.shape[0]
    rows = t_new * N_HEADS
    sub = lax.broadcasted_iota(jnp.int32, (N_HEADS, ATT_W), 0)
    lane_head = lax.broadcasted_iota(jnp.int32, (N_HEADS, ATT_W), 1) // HEAD_DIM
    own = sub == lane_head
    qbd = jnp.concatenate(
        [jnp.where(own, jnp.broadcast_to(q[t:t + 1, :], (N_HEADS, ATT_W)), 0.0) for t in range(t_new)], axis=0)
    rt = lax.broadcasted_iota(jnp.int32, (rows, 1), 0) // N_HEADS
    s_new = []
    for t2 in range(t_new):
        sc = jnp.sum(qbd * kn[t2:t2 + 1, :], axis=1, keepdims=True) + cqcol - cqmat_ref[bb, :, t2:t2 + 1]
        s_new.append(jnp.where(rt >= t2, sc, -jnp.inf))
    m = s_new[0]
    for t2 in range(1, t_new):
        m = jnp.maximum(m, s_new[t2])
    l = jnp.zeros((rows, 1), F32)
    acc = jnp.zeros((rows, ATT_W), F32)
    for t2 in range(t_new):
        pr = jnp.exp(s_new[t2] - m)
        l = l + pr
        acc = acc + pr * vn[t2:t2 + 1, :]
    return dict(qbd_b=qbd.astype(BF16), cqcol=cqcol, own=own, t_new=t_new,
                m=m, l=l, acc=acc, c_run=jnp.zeros((N_HEADS, PAGE), F32))


def _decode_scores(st, kbuf, lbuf, slot, uo, valid):
    g_pages = kbuf.shape[1]
    t_new = st["t_new"]
    c_run = st["c_run"]
    lp = lbuf[slot].reshape(g_pages * N_HEADS, PAGE)
    r = _dot3(lp, uo)
    sufs = [None] * g_pages
    for i in reversed(range(g_pages)):
        sufs[i] = jnp.where(valid[i], r[i * N_HEADS:(i + 1) * N_HEADS, 0:PAGE] + c_run, -jnp.inf)
        c_run = c_run + r[i * N_HEADS:(i + 1) * N_HEADS, PAGE:2 * PAGE]
    scs = []
    for i in range(0, g_pages, 2):
        kt2 = jnp.concatenate([kbuf[slot, i].astype(BF16), kbuf[slot, i + 1].astype(BF16)], axis=1)
        sc = jnp.dot(st["qbd_b"], kt2, preferred_element_type=F32)
        bias = jnp.concatenate([jnp.concatenate([sufs[i]] * t_new, axis=0),
                                jnp.concatenate([sufs[i + 1]] * t_new, axis=0)], axis=1)
        scs.append(sc + bias + st["cqcol"])
    return scs, dict(st, c_run=c_run)


def _decode_update(st, scs, vbuf, slot):
    nt = (((1,), (1,)), ((), ()))
    m, l, acc = st["m"], st["l"], st["acc"]
    m_new = m
    for sc in scs:
        m_new = jnp.maximum(m_new, jnp.max(sc, axis=1, keepdims=True))
    alpha = jnp.exp(m - m_new)
    l = alpha * l
    acc = alpha * acc
    for idx, sc in enumerate(scs):
        i = 2 * idx
        pr = jnp.exp(sc - m_new)
        l = l + jnp.sum(pr, axis=1, keepdims=True)
        vt2 = jnp.concatenate([vbuf[slot, i].astype(BF16), vbuf[slot, i + 1].astype(BF16)], axis=1)
        acc = acc + lax.dot_general(pr.astype(BF16), vt2, nt, preferred_element_type=F32)
    return dict(st, m=m_new, l=l, acc=acc)


def _decode_end(st, o_ref, bb):
    o = st["acc"] / st["l"]
    for t in range(st["t_new"]):
        blk = jnp.where(st["own"], o[t * N_HEADS:(t + 1) * N_HEADS, :], 0.0)
        o_ref[bb, t:t + 1, :] = jnp.sum(blk, axis=0, keepdims=True)


def _ffn_decode_kernel(pt_ref, x_ref, att_ref, sgu_ref, mod_ref, wo_ref, g2_ref, wu_ref, wcg_ref,
                       wcv_ref, bcg_ref, bcv_ref, wd_ref, gf_ref,
                       q_ref, kn_ref, vn_ref, cqcol_ref, cqmat_ref, uo_ref, kc_hbm, vc_hbm, lc_hbm,
                       y_ref, conv_ref, o_ref,
                       x1_ref, h2_ref, f_ref, pg_ref, pv_ref, kbuf, vbuf, lbuf, sem):
    s = pl.program_id(1)
    step = pl.program_id(0) * pl.num_programs(1) + s
    n_steps = pl.num_programs(0) * pl.num_programs(1)
    tm = x_ref.shape[1]
    n_chunks = wd_ref.shape[0]
    th = tm // 2
    nv = th // 8
    halves = (slice(0, th), slice(th, tm))
    n_pages = pt_ref.shape[1]
    g_pages = kbuf.shape[1]
    n_regions = 2 * n_chunks
    n_slots = kbuf.shape[0]
    ahead = n_slots - 2
    total = n_steps * n_regions

    def page_of(r, i):
        return n_pages - g_pages * (r + 1) + i

    def copies(n, slot):
        bb = n // n_regions
        r = n % n_regions
        out = []
        for i in range(g_pages):
            phys = pt_ref[bb, jnp.maximum(page_of(r, i), 0)]
            out.append(pltpu.make_async_copy(kc_hbm.at[phys], kbuf.at[slot, i], sem.at[0, slot]))
            out.append(pltpu.make_async_copy(vc_hbm.at[phys], vbuf.at[slot, i], sem.at[1, slot]))
            out.append(pltpu.make_async_copy(lc_hbm.at[phys], lbuf.at[slot, i], sem.at[2, slot]))
        return out

    @pl.when(step == 0)
    def _():
        for n0 in range(ahead):
            for c in copies(n0, n0):
                c.start()

    @pl.when(s == 0)
    def _():
        pg_ref[...] = jnp.zeros(pg_ref.shape, F32)
        pv_ref[...] = jnp.zeros(pv_ref.shape, F32)

    for i, rows in enumerate(halves):
        mix = (jnp.dot(att_ref[0, rows, :], wo_ref[0:ATT_W, :], preferred_element_type=F32)
               + jnp.dot(sgu_ref[0, rows, :], wo_ref[ATT_W:ATT_W + SGU_W, :], preferred_element_type=F32))
        x1 = _interleave_rows(x_ref[0, rows, :] + mod_ref[0, 2:3, :] * mix, nv)
        x1_ref[rows, :] = x1
        h2_ref[i] = _rms_mod(x1, g2_ref[...], mod_ref[0, 3:4, :], mod_ref[0, 4:5, :]).astype(BF16)
    f_ref[...] = jnp.zeros(f_ref.shape, F32)

    uo = uo_ref[...]
    st0 = _decode_begin(q_ref[0], kn_ref[0], vn_ref[0], cqcol_ref[0], cqmat_ref, 0)

    def trip(c, carry):
        st = dict(st0, m=carry[0], l=carry[1], acc=carry[2], c_run=carry[3])
        regions = [2 * c, 2 * c + 1]
        ns_ = [step * n_regions + r for r in regions]
        slots = [n % n_slots for n in ns_]
        for n in ns_:
            @pl.when(n + ahead < total)
            def _(n=n):
                for cp in copies(n + ahead, (n + ahead) % n_slots):
                    cp.start()
        for n, slot in zip(ns_, slots):
            for cp in copies(n, slot):
                cp.wait()
        gate = pl.ds(pl.multiple_of(c * FC, FC), FC)
        value = pl.ds(pl.multiple_of((n_chunks + c) * FC, FC), FC)
        ups = [(jnp.dot(h2_ref[i], wu_ref[:, gate], preferred_element_type=F32),
                jnp.dot(h2_ref[i], wu_ref[:, value], preferred_element_type=F32)) for i in range(2)]
        scs = []
        for r, slot in zip(regions, slots):
            valid = [page_of(r, k) >= 0 for k in range(g_pages)]
            sc, st = _decode_scores(st, kbuf, lbuf, slot, uo, valid)
            scs.append(sc)

        def down(i):
            cg = _causal_conv3(ups[i][0], pg_ref, c, wcg_ref, bcg_ref)
            cv = _causal_conv3(ups[i][1], pv_ref, c, wcv_ref, bcv_ref)
            act = (jax.nn.silu(cg) * cv).astype(BF16)
            f_ref[i] += jnp.dot(act, wd_ref[c], preferred_element_type=F32)

        down(0)
        for sc, slot in zip(scs, slots):
            st = _decode_update(st, sc, vbuf, slot)
        down(1)
        return st["m"], st["l"], st["acc"], st["c_run"]

    m, l, acc, c_run = lax.fori_loop(0, n_chunks, trip, (st0["m"], st0["l"], st0["acc"], st0["c_run"]))
    _decode_end(dict(st0, m=m, l=l, acc=acc, c_run=c_run), o_ref, 0)

    for i, rows in enumerate(halves):
        x2 = x1_ref[rows, :] + mod_ref[0, 5:6, :] * f_ref[i]
        y = x2 * lax.rsqrt(jnp.mean(x2 * x2, axis=-1, keepdims=True) + EPS) * gf_ref[...]
        y_ref[0, rows, :] = _deinterleave_rows(y, nv)
    fc = pg_ref.shape[2]
    for c in range(n_chunks):
        for k, row in enumerate((7, 15)):
            conv_ref[0, k:k + 1, c * fc:(c + 1) * fc] = pg_ref[c, row:row + 1, :]
            conv_ref[0, k:k + 1, (n_chunks + c) * fc:(n_chunks + c + 1) * fc] = pv_ref[c, row:row + 1, :]


def _ffn_prompt_and_decode(page_table, x, att, sgu, mod3, wo, g2, wu, wcg, wcv, bcg, bcv, wd, gf,
                           q3, k3, v3, cqcol, cqmat, uo, kc, vc, lc):
    b, s, d = x.shape
    tm = TM_FFN
    ns = s // tm
    n_chunks, fc, _ = wd.shape
    dff = n_chunks * fc
    nb, t_new, _ = q3.shape
    g = PAGES_PER_GROUP
    assert nb == b * ns and -(-page_table.shape[1] // g) == 2 * n_chunks and g % 2 == 0
    rows = t_new * N_HEADS
    consts = (wo, g2, wu, wcg, wcv, bcg, bcv, wd, gf)
    row_blk = lambda w: pl.BlockSpec((1, tm, w), lambda i, j, pt: (i, j, 0))
    smp_blk = lambda r, w: pl.BlockSpec((1, r, w), lambda i, j, pt: (i * ns + j, 0, 0))
    const_blk = lambda a: pl.BlockSpec(a.shape, lambda i, j, pt: (0,) * a.ndim, pipeline_mode=pl.Buffered(1))
    grid_spec = pltpu.PrefetchScalarGridSpec(
        num_scalar_prefetch=1,
        grid=(b, ns),
        in_specs=[row_blk(d), row_blk(ATT_W), row_blk(SGU_W),
                  pl.BlockSpec((1, N_MOD, d), lambda i, j, pt: (i, 0, 0))]
                 + [const_blk(a) for a in consts]
                 + [smp_blk(t_new, ATT_W), smp_blk(t_new, ATT_W), smp_blk(t_new, ATT_W),
                    smp_blk(rows, 1), smp_blk(rows, t_new), const_blk(uo),
                    pl.BlockSpec(memory_space=pl.ANY), pl.BlockSpec(memory_space=pl.ANY),
                    pl.BlockSpec(memory_space=pl.ANY)],
        out_specs=(row_blk(d), pl.BlockSpec((1, 2, 2 * dff), lambda i, j, pt: (i, 0, 0)),
                   smp_blk(t_new, ATT_W)),
        scratch_shapes=[pltpu.VMEM((tm, d), F32),
                        pltpu.VMEM((2, tm // 2, d), BF16),
                        pltpu.VMEM((2, tm // 2, d), F32),
                        pltpu.VMEM((n_chunks, 16, fc), F32),
                        pltpu.VMEM((n_chunks, 16, fc), F32),
                        pltpu.VMEM((DECODE_SLOTS, g, ATT_W, PAGE), F32),
                        pltpu.VMEM((DECODE_SLOTS, g, ATT_W, PAGE), F32),
                        pltpu.VMEM((DECODE_SLOTS, g, N_HEADS, PAGE), F32),
                        pltpu.SemaphoreType.DMA((3, DECODE_SLOTS))],
    )
    return pl.pallas_call(
        _ffn_decode_kernel, grid_spec=grid_spec,
        out_shape=(jax.ShapeDtypeStruct((b, s, d), F32), jax.ShapeDtypeStruct((b, 2, 2 * dff), F32),
                   jax.ShapeDtypeStruct((nb, t_new, ATT_W), F32)),
        compiler_params=_cparams(("arbitrary", "arbitrary")),
        name="prompt_ffn_decode",
    )(page_table, x, att, sgu, mod3, *consts, q3, k3, v3, cqcol, cqmat, uo, kc, vc, lc)


def _ffn_sample_kernel(x_ref, att_ref, sgu_ref, mod_ref, wo_ref, g2_ref, wug_ref, wuv_ref, wcg_ref, wcv_ref,
                       bcg_ref, bcv_ref, stg_ref, stv_ref, e1_ref, e2_ref, wd_ref, gf_ref,
                       y_ref, upg_ref, upv_ref,
                       x1_ref, h2_ref, acc_ref):
    j = pl.program_id(0)
    n = x_ref.shape[0]
    t_new = 4

    @pl.when(j == 0)
    def _():
        mix = (jnp.dot(att_ref[...].astype(BF16), wo_ref[0:ATT_W, :], preferred_element_type=F32)
               + jnp.dot(sgu_ref[...], wo_ref[ATT_W:ATT_W + SGU_W, :], preferred_element_type=F32))
        x1 = x_ref[...] + _mod_rows(mod_ref, 2) * mix
        x1_ref[...] = x1
        h2_ref[...] = _rms_mod(x1, g2_ref[...], _mod_rows(mod_ref, 3), _mod_rows(mod_ref, 4)).astype(BF16)
        acc_ref[...] = jnp.zeros_like(acc_ref)

    h2 = h2_ref[...]
    tpos = lax.broadcasted_iota(jnp.int32, (n, FC), 0) % t_new

    def place(e_ref, parts):
        d = functools.partial(jnp.dot, preferred_element_type=F32)
        return d(e_ref[...], parts[0]) + d(e_ref[...], parts[1]) + d(e_ref[...], parts[2])

    def conv(up, wc_ref, bc_ref, st_ref):
        parts = _split3(st_ref[...])
        s1 = jnp.where(tpos >= 1, pltpu.roll(up, 1, axis=0), place(e1_ref, parts))
        s2 = jnp.where(tpos >= 2, pltpu.roll(up, 2, axis=0), place(e2_ref, parts))
        return bc_ref[0] + s2 * wc_ref[0, 0:1, :] + s1 * wc_ref[0, 1:2, :] + up * wc_ref[0, 2:3, :]

    upg = jnp.dot(h2, wug_ref[...], preferred_element_type=F32)
    upv = jnp.dot(h2, wuv_ref[...], preferred_element_type=F32)
    upg_ref[...] = upg
    upv_ref[...] = upv
    cg = conv(upg, wcg_ref, bcg_ref, stg_ref)
    cv = conv(upv, wcv_ref, bcv_ref, stv_ref)
    act = (jax.nn.silu(cg) * cv).astype(BF16)
    acc_ref[...] += jnp.dot(act, wd_ref[0], preferred_element_type=F32)

    @pl.when(j == pl.num_programs(0) - 1)
    def _():
        x2 = x1_ref[...] + _mod_rows(mod_ref, 5) * acc_ref[...]
        y_ref[...] = x2 * lax.rsqrt(jnp.mean(x2 * x2, axis=-1, keepdims=True) + EPS) * gf_ref[...]


def _ffn_sample(x2, att, sgu, mod_rows, wo, g2, wu, wcg, wcv, bcg, bcv, st2, e1, e2, wd, gf):
    n, d = x2.shape
    nf = wd.shape[0]
    dff = nf * FC
    full = lambda a: _const_spec(a.shape)
    colc = lambda r: pl.BlockSpec((r, FC), lambda j: (0, j))
    chunk = lambda a: pl.BlockSpec((1,) + a.shape[1:], lambda j: (j, 0, 0))
    in_specs = [full(x2), full(att), full(sgu), full(mod_rows), full(wo), full(g2),
                colc(d), pl.BlockSpec((d, FC), lambda j: (0, nf + j)),
                chunk(wcg), chunk(wcv), chunk(bcg), chunk(bcv),
                pl.BlockSpec((st2.shape[0], FC), lambda j: (0, j)),
                pl.BlockSpec((st2.shape[0], FC), lambda j: (0, nf + j)),
                full(e1), full(e2),
                chunk(wd), full(gf)]
    return pl.pallas_call(
        _ffn_sample_kernel, grid=(nf,),
        in_specs=in_specs,
        out_specs=(pl.BlockSpec((n, d), lambda j: (0, 0)), colc(n), colc(n)),
        out_shape=(jax.ShapeDtypeStruct((n, d), F32), jax.ShapeDtypeStruct((n, dff), F32),
                   jax.ShapeDtypeStruct((n, dff), F32)),
        scratch_shapes=[pltpu.VMEM((n, d), F32), pltpu.VMEM((n, d), BF16), pltpu.VMEM((n, d), F32)],
        compiler_params=_cparams(("arbitrary",)),
        name="sample_ffn",
    )(x2, att, sgu, mod_rows, wo, g2, wu, wu, wcg, wcv, bcg, bcv, st2, st2, e1, e2, wd, gf)


def kernel(x_prompt, x_sample, c_prompt, c_sample, cache_k, cache_v, cache_logf, state_conv, page_table,
           w_ada, b_ada, norm1_g, w_in, b_f, ln_v_g, ln_v_b, w_s, b_s, w_o, norm2_g, w_up, w_conv, b_conv,
           w_down, final_g):
    bp, s, d = x_prompt.shape
    bs, t_new, _ = x_sample.shape
    n_s = bs * t_new
    dff = w_down.shape[1]
    n_phys = cache_k.shape[1]
    assert w_ada.shape[0] == 1, "single layer"
    assert s % TM_IN == 0 and s % TQ == 0 and s % TM_FFN == 0 and TM_IN % TQ == 0
    assert dff % FC == 0 and n_s == CHUNK and t_new == 4
    assert bs == bp * (s // TM_FFN), "one decode sample rides along with each FFN grid step"

    wi = w_in[0]
    k0, v0, f0 = ATT_W, 2 * ATT_W, 3 * ATT_W
    u0 = f0 + N_HEADS
    wq = wi[:, 0:k0].astype(BF16)
    wkv = wi[:, k0:f0].astype(BF16)
    wf = wi[:, f0:u0]
    wus = wi[:, u0:].astype(BF16)
    wkvf_t = jnp.concatenate([wi[:, k0:f0], wf], axis=1).T.astype(BF16)
    wf_pad = jnp.pad(wf, ((0, 0), (0, LANES - N_HEADS))).astype(BF16)
    bf_col = b_f[0].reshape(N_HEADS, 1)
    bf_row = jnp.pad(b_f[0].reshape(1, N_HEADS), ((0, 0), (0, LANES - N_HEADS)))
    lng = ln_v_g[0].reshape(1, SGU_W)
    lnb = ln_v_b[0].reshape(1, SGU_W)
    gidx = np.arange(SGU_W) // SGU_DIM
    gmat = jnp.asarray(np.where(gidx[:, None] == gidx[None, :], 1.0 / SGU_DIM, 0.0), BF16)
    ws_b = w_s[0]
    bs_full = jnp.repeat(b_s[0].T, SGU_DIM, axis=1)
    wst = w_s[0]
    bs_full_s = jnp.tile(jnp.repeat(b_s[0][:, :t_new].T, SGU_DIM, axis=1), (bs, 1))
    r_idx = np.arange(n_s)
    same = jnp.asarray((r_idx[:, None] // t_new == r_idx[None, :] // t_new)
                       & (r_idx[:, None] >= r_idx[None, :]), F32)
    pos = np.arange(TM_IN)
    tri = jnp.asarray(np.concatenate([pos[:, None] <= pos[None, :],
                                      np.ones((TM_IN, LANES), bool)], axis=1), BF16)
    pp = np.arange(PAGE)
    uo = jnp.asarray(np.concatenate([pp[:, None] > pp[None, :],
                                     np.ones((PAGE, PAGE), bool)], axis=1), BF16)
    wo = w_o[0].astype(BF16)
    nfc = dff // FC
    chunked = lambda a: a.reshape(a.shape[0], nfc, FC).transpose(1, 0, 2)
    wu = w_up[0].astype(BF16)
    wcg, wcv = chunked(w_conv[0][:, :dff]), chunked(w_conv[0][:, dff:])
    bcg, bcv = chunked(b_conv[:, :dff]), chunked(b_conv[:, dff:])
    wd = w_down[0].astype(BF16).reshape(nfc, FC, d)
    g1 = norm1_g
    g2 = norm2_g
    gf = final_g.reshape(1, d)

    c_all = jnp.concatenate([c_prompt, jnp.repeat(c_sample, t_new, axis=0)], axis=0)
    mod = _ada(c_all, w_ada[0], b_ada)
    mod_p = mod[:bp].reshape(bp, N_MOD, d)
    mod_s = mod[bp:]

    q, kt, vt, ktb, vtb, lft, cumt, cum, sgu = _in_prompt(
        x_prompt, mod_p, g1, wq, wkvf_t, wus, bf_col, lng, lnb, gmat, ws_b, bs_full, tri)
    att = _attn_prompt(q, ktb, vtb, cumt, cum)
    new_k_p = kt.reshape(1, bp, N_HEADS, HEAD_DIM, s).transpose(0, 1, 4, 2, 3)
    new_v_p = vt.reshape(1, bp, N_HEADS, HEAD_DIM, s).transpose(0, 1, 4, 2, 3)
    new_lf_p = lft.transpose(0, 2, 1)[None]

    x2 = x_sample.reshape(n_s, d)
    q_s, k_s, v_s, lf_s, cq_s, svn_s, sgu_s = _in_sample(
        x2, mod_s, g1, wq, wkv, wf_pad, wus, bf_row, lng, lnb, gmat, wst, bs_full_s, same)
    cq3 = cq_s[:, :N_HEADS].reshape(bs, t_new, N_HEADS)
    cqcol = cq3.reshape(bs, t_new * N_HEADS, 1)
    cqmat = jnp.tile(cq3.transpose(0, 2, 1), (1, t_new, 1))
    kc = cache_k[0].transpose(0, 2, 3, 1).reshape(n_phys, ATT_W, PAGE)
    vc = cache_v[0].transpose(0, 2, 3, 1).reshape(n_phys, ATT_W, PAGE)
    lc = cache_logf[0].transpose(0, 2, 1)
    y_prompt, conv_p, att_s = _ffn_prompt_and_decode(
        page_table, x_prompt, att, sgu, mod_p, wo, g2, wu, wcg, wcv, bcg, bcv, wd, gf,
        q_s.reshape(bs, t_new, ATT_W), k_s.reshape(bs, t_new, ATT_W), v_s.reshape(bs, t_new, ATT_W),
        cqcol, cqmat, uo, kc, vc, lc)
    new_conv_p = conv_p[None]

    st2 = state_conv[0].reshape(2 * bs, 2 * dff)
    tok, src = np.arange(n_s)[:, None], np.arange(2 * bs)[None, :]
    smp, t_in = tok // t_new, tok % t_new
    e1 = jnp.asarray((t_in == 0) & (src == 2 * smp + 1), BF16)
    e2 = jnp.asarray((t_in <= 1) & (src == 2 * smp + t_in), BF16)
    y_s, upg, upv = _ffn_sample(x2, att_s.reshape(n_s, ATT_W), sgu_s, mod_s, wo, g2, wu, wcg, wcv,
                                bcg, bcv, st2, e1, e2, wd, gf)
    up = jnp.concatenate([upg, upv], axis=1).reshape(bs, t_new, 2 * dff)

    return (y_prompt, y_s.reshape(bs, t_new, d),
            new_k_p, new_v_p, new_lf_p, new_conv_p,
            k_s.reshape(1, bs, t_new, N_HEADS, HEAD_DIM), v_s.reshape(1, bs, t_new, N_HEADS, HEAD_DIM),
            lf_s[:, :N_HEADS].reshape(1, bs, t_new, N_HEADS),
            svn_s.reshape(1, bs, t_new, N_GROUPS, SGU_DIM),
            up[:, t_new - 2:, :][None])
```

```python
import functools

import jax
import jax.numpy as jnp
import numpy as np
from jax import lax
from jax.experimental import pallas as pl
from jax.experimental.pallas import tpu as pltpu

F32 = jnp.float32
BF16 = jnp.bfloat16

N_HEADS = 8
HEAD_DIM = 64
ATT_W = N_HEADS * HEAD_DIM
N_GROUPS = 8
SGU_DIM = 64
SGU_W = N_GROUPS * SGU_DIM
CHUNK = 128
PAGE = 128
DEC_T = 4
N_MOD = 6
EPS = 1e-6
ATT_SCALE = HEAD_DIM ** -0.5
LOG2E = 1.4426950408889634
LANES = 128
PAIR_W = 2 * HEAD_DIM
VMEM_LIMIT = 58 * 1024 * 1024

TM_IN = 512
TQ = 512
ATTN_SCORE_SLOTS = 3
TM_FFN = 512
FC = 256
PAGES_PER_GROUP = 6
DECODE_SLOTS = 6


def _cparams(sem):
    return pltpu.CompilerParams(dimension_semantics=sem, vmem_limit_bytes=VMEM_LIMIT)


def _const_spec(shape):
    nd = len(shape)
    return pl.BlockSpec(shape, lambda *_: (0,) * nd, pipeline_mode=pl.Buffered(1))


def _split3(x):
    hi = x.astype(BF16)
    r1 = x - hi.astype(F32)
    mid = r1.astype(BF16)
    lo = (r1 - mid.astype(F32)).astype(BF16)
    return hi, mid, lo


def _dot3(x, w):
    hi, mid, lo = _split3(x)
    d = functools.partial(jnp.dot, preferred_element_type=F32)
    return d(hi, w) + d(mid, w) + d(lo, w)


def _rms_mod(x, g, shift, scale):
    y = x * lax.rsqrt(jnp.mean(x * x, axis=-1, keepdims=True) + EPS)
    return (y * g) * (1.0 + scale) + shift


def _mod_rows(mod_ref, i):
    d = mod_ref.shape[1] // N_MOD
    return mod_ref[:, i * d:(i + 1) * d]


def _log_sigmoid(z):
    return jnp.minimum(z, 0.0) - jnp.log1p(jnp.exp(-jnp.abs(z)))


def _group_ln(sv, gmat, ln_g, ln_b):
    mu = jnp.dot(sv.astype(BF16), gmat, preferred_element_type=F32)
    d = sv - mu
    var = jnp.dot((d * d).astype(BF16), gmat, preferred_element_type=F32)
    return d * lax.rsqrt(var + EPS) * ln_g + ln_b


def _sgu_pairs(wm, svn_b, u, bias):
    lane = lax.broadcasted_iota(jnp.int32, (CHUNK, LANES), 1)
    outs = []
    for p in range(N_GROUPS // 2):
        rhs = svn_b[:, p * LANES:(p + 1) * LANES]
        lo = jnp.dot(wm[2 * p], rhs, preferred_element_type=F32)
        hi = jnp.dot(wm[2 * p + 1], rhs, preferred_element_type=F32)
        mixed = jnp.where(lane < SGU_DIM, lo, hi) + bias[:, p * LANES:(p + 1) * LANES]
        outs.append(u[:, p * LANES:(p + 1) * LANES] * mixed)
    return jnp.concatenate(outs, axis=1)


def _ada_kernel(c_ref, w_ref, b_ref, o_ref):
    a = jax.nn.silu(c_ref[...])
    o_ref[...] = jnp.dot(a.astype(BF16), w_ref[...].astype(BF16),
                         preferred_element_type=F32) + b_ref[...]


def _ada(c_all, w_ada, b_ada):
    n, d = c_all.shape
    cols = w_ada.shape[1]
    bn = d
    return pl.pallas_call(
        _ada_kernel,
        grid=(cols // bn,),
        in_specs=[pl.BlockSpec((n, d), lambda j: (0, 0)),
                  pl.BlockSpec((d, bn), lambda j: (0, j)),
                  pl.BlockSpec((1, bn), lambda j: (0, j))],
        out_specs=pl.BlockSpec((n, bn), lambda j: (0, j)),
        out_shape=jax.ShapeDtypeStruct((n, cols), F32),
        compiler_params=_cparams(("arbitrary",)),
        name="ada_ln",
    )(c_all, w_ada, b_ada)


def _in_prompt_kernel(x_ref, mod_ref, g1_ref, wq_ref, wkvf_ref, wus_ref, bf_ref, lng_ref, lnb_ref,
                      gmat_ref, ws_ref, bs_ref, tri_ref,
                      q_ref, kt_ref, vt_ref, ktb_ref, vtb_ref, lft_ref, cumt_ref, cum_ref, sgu_ref,
                      carry_ref):
    s = pl.program_id(1)
    tm = x_ref.shape[1]

    @pl.when(s == 0)
    def _():
        carry_ref[...] = jnp.zeros_like(carry_ref)

    h = _rms_mod(x_ref[0], g1_ref[...], mod_ref[0, 0:1, :], mod_ref[0, 1:2, :]).astype(BF16)

    q = jnp.dot(h, wq_ref[...], preferred_element_type=F32)
    q_ref[0] = (q * (ATT_SCALE * LOG2E)).astype(BF16)

    kvf = lax.dot_general(wkvf_ref[...], h, (((1,), (1,)), ((), ())), preferred_element_type=F32)
    kt = kvf[0:ATT_W]
    vt = kvf[ATT_W:2 * ATT_W]
    kt_ref[0] = kt
    vt_ref[0] = vt
    ktb_ref[0, 0] = kt.astype(BF16)
    vtb_ref[0, 0] = vt.astype(BF16)
    lft = _log_sigmoid(kvf[2 * ATT_W:2 * ATT_W + N_HEADS] + bf_ref[...])
    lft_ref[0] = lft

    r = _dot3(lft, tri_ref[...])
    carry = carry_ref[...]
    cumt = r[:, 0:tm] + jnp.concatenate([carry] * (tm // LANES), axis=1)
    carry_ref[...] = carry + r[:, tm:tm + LANES]
    cumt = cumt * LOG2E
    cumt_ref[0, 0] = cumt
    pad = jnp.zeros((LANES - N_HEADS, LANES), F32)
    for c in range(tm // LANES):
        blk = jnp.concatenate([cumt[:, c * LANES:(c + 1) * LANES], pad], axis=0)
        cum_ref[0, c * LANES:(c + 1) * LANES, :] = blk.T[:, 0:N_HEADS]

    us = jnp.dot(h, wus_ref[...], preferred_element_type=F32)
    u = jax.nn.gelu(us[:, 0:SGU_W])
    sv = jax.nn.gelu(us[:, SGU_W:2 * SGU_W])
    svn = _group_ln(sv, gmat_ref[...], lng_ref[...], lnb_ref[...]).astype(BF16)

    row = lax.broadcasted_iota(jnp.int32, (CHUNK, CHUNK), 0)
    col = lax.broadcasted_iota(jnp.int32, (CHUNK, CHUNK), 1)
    wm = [jnp.where(row >= col, ws_ref[g], 0.0).astype(BF16) for g in range(N_GROUPS)]
    bias = bs_ref[...]
    for c in range(tm // CHUNK):
        sl = slice(c * CHUNK, (c + 1) * CHUNK)
        sgu_ref[0, sl, :] = _sgu_pairs(wm, svn[sl], u[sl], bias).astype(BF16)


def _in_prompt(x, mod3, g1, wq, wkvf, wus, bf_col, lng, lnb, gmat, ws_b, bs_full, tri):
    b, s, d = x.shape
    tm = TM_IN
    ns = s // tm
    grid = (b, ns)
    row_blk = lambda w: pl.BlockSpec((1, tm, w), lambda i, j: (i, j, 0))
    colT_blk = lambda r: pl.BlockSpec((1, r, tm), lambda i, j: (i, 0, j))
    tiledT_blk = lambda r: pl.BlockSpec((1, 1, r, tm), lambda i, j: (i, j, 0, 0))
    out_shape = (
        jax.ShapeDtypeStruct((b, s, ATT_W), BF16),
        jax.ShapeDtypeStruct((b, ATT_W, s), F32),
        jax.ShapeDtypeStruct((b, ATT_W, s), F32),
        jax.ShapeDtypeStruct((b, ns, ATT_W, tm), BF16),
        jax.ShapeDtypeStruct((b, ns, ATT_W, tm), BF16),
        jax.ShapeDtypeStruct((b, N_HEADS, s), F32),
        jax.ShapeDtypeStruct((b, ns, N_HEADS, tm), F32),
        jax.ShapeDtypeStruct((b, s, N_HEADS), F32),
        jax.ShapeDtypeStruct((b, s, SGU_W), BF16),
    )
    out_specs = (row_blk(ATT_W), colT_blk(ATT_W), colT_blk(ATT_W), tiledT_blk(ATT_W), tiledT_blk(ATT_W),
                 colT_blk(N_HEADS), tiledT_blk(N_HEADS), row_blk(N_HEADS), row_blk(SGU_W))
    in_specs = [row_blk(d),
                pl.BlockSpec((1, N_MOD, d), lambda i, j: (i, 0, 0)),
                _const_spec(g1.shape), _const_spec(wq.shape), _const_spec(wkvf.shape), _const_spec(wus.shape),
                _const_spec(bf_col.shape), _const_spec(lng.shape), _const_spec(lnb.shape),
                _const_spec(gmat.shape), _const_spec(ws_b.shape), _const_spec(bs_full.shape),
                _const_spec(tri.shape)]
    return pl.pallas_call(
        _in_prompt_kernel, grid=grid, in_specs=in_specs, out_specs=out_specs, out_shape=out_shape,
        scratch_shapes=[pltpu.VMEM((N_HEADS, LANES), F32)],
        compiler_params=_cparams(("arbitrary", "arbitrary")),
        name="prompt_in_proj",
    )(x, mod3, g1, wq, wkvf, wus, bf_col, lng, lnb, gmat, ws_b, bs_full, tri)


def _attn_kernel(q_ref, kt_ref, vt_ref, cumt_ref, cum_ref, o_ref, qm_ref, s_ref, cq_ref, m_ref, acc_ref):
    qi = pl.program_id(1)
    tq = q_ref.shape[1]
    tk = kt_ref.shape[3]
    qs = qi * tq
    j_diag = (qs + tq - 1) // tk
    n_slots = s_ref.shape[0]
    ahead = n_slots - 1
    lane = lax.broadcasted_iota(jnp.int32, (tq, LANES), 1)
    qpos = qs + lax.broadcasted_iota(jnp.int32, (tq, tk), 0)
    kcol = lax.broadcasted_iota(jnp.int32, (tq, tk), 1)
    ones_half = jnp.ones((HEAD_DIM, tk), BF16)
    nt = (((1,), (1,)), ((), ()))

    for p in range(N_HEADS // 2):
        qp = q_ref[0, :, p * PAIR_W:(p + 1) * PAIR_W].astype(F32)
        qm_ref[2 * p] = jnp.where(lane < HEAD_DIM, qp, 0.0).astype(BF16)
        qm_ref[2 * p + 1] = jnp.where(lane >= HEAD_DIM, qp, 0.0).astype(BF16)
    for h in range(N_HEADS):
        cq_ref[h] = jnp.broadcast_to(cum_ref[0, :, h:h + 1], (tq, LANES))
    m_ref[...] = jnp.full(m_ref.shape, -jnp.inf, F32)
    acc_ref[...] = jnp.zeros(acc_ref.shape, F32)

    def scores(j, h):
        p = h // 2
        kt = kt_ref[0, j, p * PAIR_W:(p + 1) * PAIR_W, :]
        s_ref[h % n_slots] = jnp.dot(qm_ref[h], kt, preferred_element_type=F32)

    def softmax_pv(j, h, masked):
        p, half = divmod(h, 2)
        t = s_ref[h % n_slots] - cumt_ref[0, j, h:h + 1, :]
        if masked:
            t = jnp.where(j * tk + kcol <= qpos, t, -jnp.inf)
        cq = cq_ref[h]
        m = m_ref[h]
        m_new = jnp.maximum(m, jnp.max(t, axis=1, keepdims=True) + cq)
        alpha = jnp.exp2(m - m_new)
        d = cq - m_new
        pr = jnp.exp2(t + jnp.concatenate([d] * (tk // LANES), axis=1)).astype(BF16)
        vt = vt_ref[0, j, p * PAIR_W:(p + 1) * PAIR_W, :]
        if half == 0:
            vaug = jnp.concatenate([vt[0:HEAD_DIM], ones_half], axis=0)
        else:
            vaug = jnp.concatenate([ones_half, vt[HEAD_DIM:PAIR_W]], axis=0)
        acc_ref[h] = alpha * acc_ref[h] + lax.dot_general(pr, vaug, nt, preferred_element_type=F32)
        m_ref[h] = m_new

    def key_tile(j, masked):
        for h in range(N_HEADS):
            softmax_pv(j, h, masked)
            nh = h + ahead
            if nh < N_HEADS:
                scores(j, nh)
            elif not masked:
                scores(j + 1, nh - N_HEADS)

    for h in range(ahead):
        scores(0, h)

    def loop_body(j, carry):
        key_tile(j, False)
        return carry

    lax.fori_loop(0, j_diag, loop_body, 0)
    key_tile(j_diag, True)

    for p in range(N_HEADS // 2):
        a0 = acc_ref[2 * p]
        a1 = acc_ref[2 * p + 1]
        o0 = a0 / pltpu.roll(a0, HEAD_DIM, axis=1)
        o1 = a1 / pltpu.roll(a1, HEAD_DIM, axis=1)
        o_ref[0, :, p * PAIR_W:(p + 1) * PAIR_W] = jnp.where(lane < HEAD_DIM, o0, o1).astype(BF16)


def _attn_prompt(q, ktb, vtb, cumt, cum):
    b, s, _ = q.shape
    nk, tk = ktb.shape[1], ktb.shape[3]
    tq = TQ
    return pl.pallas_call(
        _attn_kernel,
        grid=(b, s // tq),
        in_specs=[pl.BlockSpec((1, tq, ATT_W), lambda i, j: (i, j, 0)),
                  pl.BlockSpec((1, nk, ATT_W, tk), lambda i, j: (i, 0, 0, 0)),
                  pl.BlockSpec((1, nk, ATT_W, tk), lambda i, j: (i, 0, 0, 0)),
                  pl.BlockSpec((1, nk, N_HEADS, tk), lambda i, j: (i, 0, 0, 0)),
                  pl.BlockSpec((1, tq, N_HEADS), lambda i, j: (i, j, 0))],
        out_specs=pl.BlockSpec((1, tq, ATT_W), lambda i, j: (i, j, 0)),
        out_shape=jax.ShapeDtypeStruct((b, s, ATT_W), BF16),
        scratch_shapes=[pltpu.VMEM((N_HEADS, tq, PAIR_W), BF16),
                        pltpu.VMEM((ATTN_SCORE_SLOTS, tq, tk), F32),
                        pltpu.VMEM((N_HEADS, tq, LANES), F32),
                        pltpu.VMEM((N_HEADS, tq, LANES), F32),
                        pltpu.VMEM((N_HEADS, tq, PAIR_W), F32)],
        compiler_params=_cparams(("arbitrary", "arbitrary")),
        name="prompt_attention",
    )(q, ktb, vtb, cumt, cum)


def _transpose_rows8(blocks):
    sub = lax.broadcasted_iota(jnp.int32, blocks[0].shape, 0)
    a = list(blocks)
    for s in (4, 2, 1):
        keep = (sub & s) == 0
        b = list(a)
        for i in range(8):
            if i & s == 0:
                lo, hi = a[i], a[i + s]
                b[i] = jnp.where(keep, lo, pltpu.roll(hi, s, axis=0))
                b[i + s] = jnp.where(keep, pltpu.roll(lo, 8 - s, axis=0), hi)
        a = b
    return a


def _interleave_rows(x, nv):
    out = [None] * nv
    for jh in range(nv // 8):
        t = _transpose_rows8([x[r * nv + jh * 8:r * nv + jh * 8 + 8, :] for r in range(8)])
        for jl in range(8):
            out[8 * jh + jl] = t[jl]
    return jnp.concatenate(out, axis=0)


def _deinterleave_rows(y, nv):
    nb = nv // 8
    out = [None] * nv
    for jh in range(nb):
        t = _transpose_rows8([y[(8 * jh + jl) * 8:(8 * jh + jl) * 8 + 8, :] for jl in range(8)])
        for r in range(8):
            out[r * nb + jh] = t[r]
    return jnp.concatenate(out, axis=0)


def _causal_conv3(up, prev_ref, c, wc_ref, bc_ref):
    th = up.shape[0]
    first_row = lax.broadcasted_iota(jnp.int32, (8, up.shape[1]), 0) == 0

    def wrap(prev_blk, cur_blk):
        return jnp.where(first_row, pltpu.roll(prev_blk, 1, axis=0), pltpu.roll(cur_blk, 1, axis=0))

    w1 = wrap(prev_ref[c, 8:16, :], up[th - 8:th])
    w2 = wrap(prev_ref[c, 0:8, :], up[th - 16:th - 8])
    s1 = jnp.concatenate([w1, up[0:th - 8]], axis=0)
    s2 = jnp.concatenate([w2, w1, up[0:th - 16]], axis=0)
    prev_ref[c] = up[th - 16:th]
    return bc_ref[c] + s2 * wc_ref[c, 0:1, :] + s1 * wc_ref[c, 1:2, :] + up * wc_ref[c, 2:3, :]


def _in_sample_kernel(x_ref, mod_ref, g1_ref, wq_ref, wkv_ref, wf_ref, wus_ref, bf_ref, lng_ref, lnb_ref,
                      gmat_ref, wst_ref, bs_ref, same_ref,
                      q_ref, k_ref, v_ref, lf_ref, cq_ref, svn_ref, sgu_ref):
    n = x_ref.shape[0]
    h = _rms_mod(x_ref[...], g1_ref[...], _mod_rows(mod_ref, 0), _mod_rows(mod_ref, 1)).astype(BF16)
    q_ref[...] = jnp.dot(h, wq_ref[...], preferred_element_type=F32) * ATT_SCALE
    kv = jnp.dot(h, wkv_ref[...], preferred_element_type=F32)
    k_ref[...] = kv[:, 0:ATT_W]
    v_ref[...] = kv[:, ATT_W:2 * ATT_W]
    lf = _log_sigmoid(jnp.dot(h, wf_ref[...], preferred_element_type=F32) + bf_ref[...])
    lf_ref[...] = lf

    same = same_ref[...]
    hi, mid, lo = _split3(lf)
    tri = same.astype(BF16)
    d = functools.partial(jnp.dot, preferred_element_type=F32)
    cq_ref[...] = d(tri, hi) + d(tri, mid) + d(tri, lo)

    us = jnp.dot(h, wus_ref[...], preferred_element_type=F32)
    u = jax.nn.gelu(us[:, 0:SGU_W])
    sv = jax.nn.gelu(us[:, SGU_W:2 * SGU_W])
    svn = _group_ln(sv, gmat_ref[...], lng_ref[...], lnb_ref[...])
    svn_ref[...] = svn
    r_i = lax.broadcasted_iota(jnp.int32, (n, n), 0)
    c_i = lax.broadcasted_iota(jnp.int32, (n, n), 1)
    pick_rows = jnp.where(c_i == r_i % DEC_T, 1.0, 0.0).astype(BF16)
    pick_cols = jnp.where(r_i == c_i % DEC_T, 1.0, 0.0).astype(BF16)
    wm = []
    for g in range(N_GROUPS):
        rows = jnp.dot(pick_rows, wst_ref[g].astype(BF16), preferred_element_type=F32)
        tiled = jnp.dot(rows.astype(BF16), pick_cols, preferred_element_type=F32)
        wm.append(jnp.where(same > 0, tiled, 0.0).astype(BF16))
    sgu_ref[...] = _sgu_pairs(wm, svn.astype(BF16), u, bs_ref[...]).astype(BF16)


def _in_sample(x2, mod_rows, g1, wq, wkv, wf, wus, bf_row, lng, lnb, gmat, wst, bs_full, same):
    n, d = x2.shape
    args = (x2, mod_rows, g1, wq, wkv, wf, wus, bf_row, lng, lnb, gmat, wst, bs_full, same)
    out_shape = (jax.ShapeDtypeStruct((n, ATT_W), F32), jax.ShapeDtypeStruct((n, ATT_W), F32),
                 jax.ShapeDtypeStruct((n, ATT_W), F32), jax.ShapeDtypeStruct((n, LANES), F32),
                 jax.ShapeDtypeStruct((n, LANES), F32), jax.ShapeDtypeStruct((n, SGU_W), F32),
                 jax.ShapeDtypeStruct((n, SGU_W), BF16))
    return pl.pallas_call(
        _in_sample_kernel, grid=(1,),
        in_specs=[_const_spec((n,) + a.shape[1:] if a is mod_rows else a.shape) for a in args],
        out_specs=tuple(pl.BlockSpec(o.shape, lambda i: (0, 0)) for o in out_shape),
        out_shape=out_shape,
        compiler_params=_cparams(("arbitrary",)),
        name="sample_in_proj",
    )(*args)


def _decode_begin(q, kn, vn, cqcol, cqmat_ref, bb):
    t_new = q.shape[0]
    rows = t_new * N_HEADS
    sub = lax.broadcasted_iota(jnp.int32, (N_HEADS, ATT_W), 0)
    lane_head = lax.broadcasted_iota(jnp.int32, (N_HEADS, ATT_W), 1) // HEAD_DIM
    own = sub == lane_head
    qbd = jnp.concatenate(
        [jnp.where(own, jnp.broadcast_to(q[t:t + 1, :], (N_HEADS, ATT_W)), 0.0) for t in range(t_new)], axis=0)
    rt = lax.broadcasted_iota(jnp.int32, (rows, 1), 0) // N_HEADS
    s_new = []
    for t2 in range(t_new):
        sc = jnp.sum(qbd * kn[t2:t2 + 1, :], axis=1, keepdims=True) + cqcol - cqmat_ref[bb, :, t2:t2 + 1]
        s_new.append(jnp.where(rt >= t2, sc, -jnp.inf))
    m = s_new[0]
    for t2 in range(1, t_new):
        m = jnp.maximum(m, s_new[t2])
    l = jnp.zeros((rows, 1), F32)
    acc = jnp.zeros((rows, ATT_W), F32)
    for t2 in range(t_new):
        pr = jnp.exp(s_new[t2] - m)
        l = l + pr
        acc = acc + pr * vn[t2:t2 + 1, :]
    return dict(qbd_b=qbd.astype(BF16), cqcol=cqcol, own=own, t_new=t_new,
                m=m, l=l, acc=acc, c_run=jnp.zeros((N_HEADS, PAGE), F32))


def _decode_scores(st, kbuf, lbuf, slot, uo, valid):
    g_pages = kbuf.shape[1]
    t_new = st["t_new"]
    c_run = st["c_run"]
    lp = lbuf[slot].reshape(g_pages * N_HEADS, PAGE)
    r = _dot3(lp, uo)
    sufs = [None] * g_pages
    for i in reversed(range(g_pages)):
        sufs[i] = jnp.where(valid[i], r[i * N_HEADS:(i + 1) * N_HEADS, 0:PAGE] + c_run, -jnp.inf)
        c_run = c_run + r[i * N_HEADS:(i + 1) * N_HEADS, PAGE:2 * PAGE]
    scs = []
    for i in range(0, g_pages, 2):
        kt2 = jnp.concatenate([kbuf[slot, i].astype(BF16), kbuf[slot, i + 1].astype(BF16)], axis=1)
        sc = jnp.dot(st["qbd_b"], kt2, preferred_element_type=F32)
        bias = jnp.concatenate([jnp.concatenate([sufs[i]] * t_new, axis=0),
                                jnp.concatenate([sufs[i + 1]] * t_new, axis=0)], axis=1)
        scs.append(sc + bias + st["cqcol"])
    return scs, dict(st, c_run=c_run)


def _decode_update(st, scs, vbuf, slot):
    nt = (((1,), (1,)), ((), ()))
    m, l, acc = st["m"], st["l"], st["acc"]
    m_new = m
    for sc in scs:
        m_new = jnp.maximum(m_new, jnp.max(sc, axis=1, keepdims=True))
    alpha = jnp.exp(m - m_new)
    l = alpha * l
    acc = alpha * acc
    for idx, sc in enumerate(scs):
        i = 2 * idx
        pr = jnp.exp(sc - m_new)
        l = l + jnp.sum(pr, axis=1, keepdims=True)
        vt2 = jnp.concatenate([vbuf[slot, i].astype(BF16), vbuf[slot, i + 1].astype(BF16)], axis=1)
        acc = acc + lax.dot_general(pr.astype(BF16), vt2, nt, preferred_element_type=F32)
    return dict(st, m=m_new, l=l, acc=acc)


def _decode_end(st, o_ref, bb):
    o = st["acc"] / st["l"]
    for t in range(st["t_new"]):
        blk = jnp.where(st["own"], o[t * N_HEADS:(t + 1) * N_HEADS, :], 0.0)
        o_ref[bb, t:t + 1, :] = jnp.sum(blk, axis=0, keepdims=True)


def _ffn_decode_kernel(pt_ref, x_ref, att_ref, sgu_ref, mod_ref, wo_ref, g2_ref, wu_ref, wcg_ref,
                       wcv_ref, bcg_ref, bcv_ref, wd_ref, gf_ref,
                       q_ref, kn_ref, vn_ref, cqcol_ref, cqmat_ref, uo_ref, kc_hbm, vc_hbm, lc_hbm,
                       y_ref, conv_ref, o_ref,
                       x1_ref, h2_ref, f_ref, pg_ref, pv_ref, kbuf, vbuf, lbuf, sem):
    s = pl.program_id(1)
    step = pl.program_id(0) * pl.num_programs(1) + s
    n_steps = pl.num_programs(0) * pl.num_programs(1)
    tm = x_ref.shape[1]
    n_chunks = wd_ref.shape[0]
    th = tm // 2
    nv = th // 8
    halves = (slice(0, th), slice(th, tm))
    n_pages = pt_ref.shape[1]
    g_pages = kbuf.shape[1]
    n_regions = 2 * n_chunks
    n_slots = kbuf.shape[0]
    ahead = n_slots - 2
    total = n_steps * n_regions

    def page_of(r, i):
        return n_pages - g_pages * (r + 1) + i

    def copies(n, slot):
        bb = n // n_regions
        r = n % n_regions
        out = []
        for i in range(g_pages):
            phys = pt_ref[bb, jnp.maximum(page_of(r, i), 0)]
            out.append(pltpu.make_async_copy(kc_hbm.at[phys], kbuf.at[slot, i], sem.at[0, slot]))
            out.append(pltpu.make_async_copy(vc_hbm.at[phys], vbuf.at[slot, i], sem.at[1, slot]))
            out.append(pltpu.make_async_copy(lc_hbm.at[phys], lbuf.at[slot, i], sem.at[2, slot]))
        return out

    @pl.when(step == 0)
    def _():
        for n0 in range(ahead):
            for c in copies(n0, n0):
                c.start()

    @pl.when(s == 0)
    def _():
        pg_ref[...] = jnp.zeros(pg_ref.shape, F32)
        pv_ref[...] = jnp.zeros(pv_ref.shape, F32)

    for i, rows in enumerate(halves):
        mix = (jnp.dot(att_ref[0, rows, :], wo_ref[0:ATT_W, :], preferred_element_type=F32)
               + jnp.dot(sgu_ref[0, rows, :], wo_ref[ATT_W:ATT_W + SGU_W, :], preferred_element_type=F32))
        x1 = _interleave_rows(x_ref[0, rows, :] + mod_ref[0, 2:3, :] * mix, nv)
        x1_ref[rows, :] = x1
        h2_ref[i] = _rms_mod(x1, g2_ref[...], mod_ref[0, 3:4, :], mod_ref[0, 4:5, :]).astype(BF16)
    f_ref[...] = jnp.zeros(f_ref.shape, F32)

    uo = uo_ref[...]
    st0 = _decode_begin(q_ref[0], kn_ref[0], vn_ref[0], cqcol_ref[0], cqmat_ref, 0)

    def trip(c, carry):
        st = dict(st0, m=carry[0], l=carry[1], acc=carry[2], c_run=carry[3])
        regions = [2 * c, 2 * c + 1]
        ns_ = [step * n_regions + r for r in regions]
        slots = [n % n_slots for n in ns_]
        for n in ns_:
            @pl.when(n + ahead < total)
            def _(n=n):
                for cp in copies(n + ahead, (n + ahead) % n_slots):
                    cp.start()
        for n, slot in zip(ns_, slots):
            for cp in copies(n, slot):
                cp.wait()
        gate = pl.ds(pl.multiple_of(c * FC, FC), FC)
        value = pl.ds(pl.multiple_of((n_chunks + c) * FC, FC), FC)
        ups = [(jnp.dot(h2_ref[i], wu_ref[:, gate], preferred_element_type=F32),
                jnp.dot(h2_ref[i], wu_ref[:, value], preferred_element_type=F32)) for i in range(2)]
        scs = []
        for r, slot in zip(regions, slots):
            valid = [page_of(r, k) >= 0 for k in range(g_pages)]
            sc, st = _decode_scores(st, kbuf, lbuf, slot, uo, valid)
            scs.append(sc)

        def down(i):
            cg = _causal_conv3(ups[i][0], pg_ref, c, wcg_ref, bcg_ref)
            cv = _causal_conv3(ups[i][1], pv_ref, c, wcv_ref, bcv_ref)
            act = (jax.nn.silu(cg) * cv).astype(BF16)
            f_ref[i] += jnp.dot(act, wd_ref[c], preferred_element_type=F32)

        down(0)
        for sc, slot in zip(scs, slots):
            st = _decode_update(st, sc, vbuf, slot)
        down(1)
        return st["m"], st["l"], st["acc"], st["c_run"]

    m, l, acc, c_run = lax.fori_loop(0, n_chunks, trip, (st0["m"], st0["l"], st0["acc"], st0["c_run"]))
    _decode_end(dict(st0, m=m, l=l, acc=acc, c_run=c_run), o_ref, 0)

    for i, rows in enumerate(halves):
        x2 = x1_ref[rows, :] + mod_ref[0, 5:6, :] * f_ref[i]
        y = x2 * lax.rsqrt(jnp.mean(x2 * x2, axis=-1, keepdims=True) + EPS) * gf_ref[...]
        y_ref[0, rows, :] = _deinterleave_rows(y, nv)
    fc = pg_ref.shape[2]
    for c in range(n_chunks):
        for k, row in enumerate((7, 15)):
            conv_ref[0, k:k + 1, c * fc:(c + 1) * fc] = pg_ref[c, row:row + 1, :]
            conv_ref[0, k:k + 1, (n_chunks + c) * fc:(n_chunks + c + 1) * fc] = pv_ref[c, row:row + 1, :]


def _ffn_prompt_and_decode(page_table, x, att, sgu, mod3, wo, g2, wu, wcg, wcv, bcg, bcv, wd, gf,
                           q3, k3, v3, cqcol, cqmat, uo, kc, vc, lc):
    b, s, d = x.shape
    tm = TM_FFN
    ns = s // tm
    n_chunks, fc, _ = wd.shape
    dff = n_chunks * fc
    nb, t_new, _ = q3.shape
    g = PAGES_PER_GROUP
    assert nb == b * ns and -(-page_table.shape[1] // g) == 2 * n_chunks and g % 2 == 0
    rows = t_new * N_HEADS
    consts = (wo, g2, wu, wcg, wcv, bcg, bcv, wd, gf)
    row_blk = lambda w: pl.BlockSpec((1, tm, w), lambda i, j, pt: (i, j, 0))
    smp_blk = lambda r, w: pl.BlockSpec((1, r, w), lambda i, j, pt: (i * ns + j, 0, 0))
    const_blk = lambda a: pl.BlockSpec(a.shape, lambda i, j, pt: (0,) * a.ndim, pipeline_mode=pl.Buffered(1))
    grid_spec = pltpu.PrefetchScalarGridSpec(
        num_scalar_prefetch=1,
        grid=(b, ns),
        in_specs=[row_blk(d), row_blk(ATT_W), row_blk(SGU_W),
                  pl.BlockSpec((1, N_MOD, d), lambda i, j, pt: (i, 0, 0))]
                 + [const_blk(a) for a in consts]
                 + [smp_blk(t_new, ATT_W), smp_blk(t_new, ATT_W), smp_blk(t_new, ATT_W),
                    smp_blk(rows, 1), smp_blk(rows, t_new), const_blk(uo),
                    pl.BlockSpec(memory_space=pl.ANY), pl.BlockSpec(memory_space=pl.ANY),
                    pl.BlockSpec(memory_space=pl.ANY)],
        out_specs=(row_blk(d), pl.BlockSpec((1, 2, 2 * dff), lambda i, j, pt: (i, 0, 0)),
                   smp_blk(t_new, ATT_W)),
        scratch_shapes=[pltpu.VMEM((tm, d), F32),
                        pltpu.VMEM((2, tm // 2, d), BF16),
                        pltpu.VMEM((2, tm // 2, d), F32),
                        pltpu.VMEM((n_chunks, 16, fc), F32),
                        pltpu.VMEM((n_chunks, 16, fc), F32),
                        pltpu.VMEM((DECODE_SLOTS, g, ATT_W, PAGE), F32),
                        pltpu.VMEM((DECODE_SLOTS, g, ATT_W, PAGE), F32),
                        pltpu.VMEM((DECODE_SLOTS, g, N_HEADS, PAGE), F32),
                        pltpu.SemaphoreType.DMA((3, DECODE_SLOTS))],
    )
    return pl.pallas_call(
        _ffn_decode_kernel, grid_spec=grid_spec,
        out_shape=(jax.ShapeDtypeStruct((b, s, d), F32), jax.ShapeDtypeStruct((b, 2, 2 * dff), F32),
                   jax.ShapeDtypeStruct((nb, t_new, ATT_W), F32)),
        compiler_params=_cparams(("arbitrary", "arbitrary")),
        name="prompt_ffn_decode",
    )(page_table, x, att, sgu, mod3, *consts, q3, k3, v3, cqcol, cqmat, uo, kc, vc, lc)


def _ffn_sample_kernel(x_ref, att_ref, sgu_ref, mod_ref, wo_ref, g2_ref, wug_ref, wuv_ref, wcg_ref, wcv_ref,
                       bcg_ref, bcv_ref, stg_ref, stv_ref, e1_ref, e2_ref, wd_ref, gf_ref,
                       y_ref, upg_ref, upv_ref,
                       x1_ref, h2_ref, acc_ref):
    j = pl.program_id(0)
    n = x_ref.shape[0]
    t_new = 4

    @pl.when(j == 0)
    def _():
        mix = (jnp.dot(att_ref[...].astype(BF16), wo_ref[0:ATT_W, :], preferred_element_type=F32)
               + jnp.dot(sgu_ref[...], wo_ref[ATT_W:ATT_W + SGU_W, :], preferred_element_type=F32))
        x1 = x_ref[...] + _mod_rows(mod_ref, 2) * mix
        x1_ref[...] = x1
        h2_ref[...] = _rms_mod(x1, g2_ref[...], _mod_rows(mod_ref, 3), _mod_rows(mod_ref, 4)).astype(BF16)
        acc_ref[...] = jnp.zeros_like(acc_ref)

    h2 = h2_ref[...]
    tpos = lax.broadcasted_iota(jnp.int32, (n, FC), 0) % t_new

    def place(e_ref, parts):
        d = functools.partial(jnp.dot, preferred_element_type=F32)
        return d(e_ref[...], parts[0]) + d(e_ref[...], parts[1]) + d(e_ref[...], parts[2])

    def conv(up, wc_ref, bc_ref, st_ref):
        parts = _split3(st_ref[...])
        s1 = jnp.where(tpos >= 1, pltpu.roll(up, 1, axis=0), place(e1_ref, parts))
        s2 = jnp.where(tpos >= 2, pltpu.roll(up, 2, axis=0), place(e2_ref, parts))
        return bc_ref[0] + s2 * wc_ref[0, 0:1, :] + s1 * wc_ref[0, 1:2, :] + up * wc_ref[0, 2:3, :]

    upg = jnp.dot(h2, wug_ref[...], preferred_element_type=F32)
    upv = jnp.dot(h2, wuv_ref[...], preferred_element_type=F32)
    upg_ref[...] = upg
    upv_ref[...] = upv
    cg = conv(upg, wcg_ref, bcg_ref, stg_ref)
    cv = conv(upv, wcv_ref, bcv_ref, stv_ref)
    act = (jax.nn.silu(cg) * cv).astype(BF16)
    acc_ref[...] += jnp.dot(act, wd_ref[0], preferred_element_type=F32)

    @pl.when(j == pl.num_programs(0) - 1)
    def _():
        x2 = x1_ref[...] + _mod_rows(mod_ref, 5) * acc_ref[...]
        y_ref[...] = x2 * lax.rsqrt(jnp.mean(x2 * x2, axis=-1, keepdims=True) + EPS) * gf_ref[...]


def _ffn_sample(x2, att, sgu, mod_rows, wo, g2, wu, wcg, wcv, bcg, bcv, st2, e1, e2, wd, gf):
    n, d = x2.shape
    nf = wd.shape[0]
    dff = nf * FC
    full = lambda a: _const_spec(a.shape)
    colc = lambda r: pl.BlockSpec((r, FC), lambda j: (0, j))
    chunk = lambda a: pl.BlockSpec((1,) + a.shape[1:], lambda j: (j, 0, 0))
    in_specs = [full(x2), full(att), full(sgu), _const_spec((n, mod_rows.shape[1])), full(wo), full(g2),
                colc(d), pl.BlockSpec((d, FC), lambda j: (0, nf + j)),
                chunk(wcg), chunk(wcv), chunk(bcg), chunk(bcv),
                pl.BlockSpec((st2.shape[0], FC), lambda j: (0, j)),
                pl.BlockSpec((st2.shape[0], FC), lambda j: (0, nf + j)),
                full(e1), full(e2),
                chunk(wd), full(gf)]
    return pl.pallas_call(
        _ffn_sample_kernel, grid=(nf,),
        in_specs=in_specs,
        out_specs=(pl.BlockSpec((n, d), lambda j: (0, 0)), colc(n), colc(n)),
        out_shape=(jax.ShapeDtypeStruct((n, d), F32), jax.ShapeDtypeStruct((n, dff), F32),
                   jax.ShapeDtypeStruct((n, dff), F32)),
        scratch_shapes=[pltpu.VMEM((n, d), F32), pltpu.VMEM((n, d), BF16), pltpu.VMEM((n, d), F32)],
        compiler_params=_cparams(("arbitrary",)),
        name="sample_ffn",
    )(x2, att, sgu, mod_rows, wo, g2, wu, wu, wcg, wcv, bcg, bcv, st2, st2, e1, e2, wd, gf)


def kernel(x_prompt, x_sample, c_prompt, c_sample, cache_k, cache_v, cache_logf, state_conv, page_table,
           w_ada, b_ada, norm1_g, w_in, b_f, ln_v_g, ln_v_b, w_s, b_s, w_o, norm2_g, w_up, w_conv, b_conv,
           w_down, final_g):
    bp, s, d = x_prompt.shape
    bs, t_new, _ = x_sample.shape
    n_s = bs * t_new
    dff = w_down.shape[1]
    n_phys = cache_k.shape[1]
    assert w_ada.shape[0] == 1, "single layer"
    assert s % TM_IN == 0 and s % TQ == 0 and s % TM_FFN == 0 and TM_IN % TQ == 0
    assert dff % FC == 0 and n_s == CHUNK and t_new == 4
    assert bs == bp * (s // TM_FFN), "one decode sample rides along with each FFN grid step"

    wi = w_in[0]
    k0, v0, f0 = ATT_W, 2 * ATT_W, 3 * ATT_W
    u0 = f0 + N_HEADS
    wq = wi[:, 0:k0].astype(BF16)
    wkv = wi[:, k0:f0].astype(BF16)
    wf = wi[:, f0:u0]
    wus = wi[:, u0:].astype(BF16)
    wkvf_t = jnp.concatenate([wi[:, k0:f0], wf], axis=1).T.astype(BF16)
    wf_pad = jnp.pad(wf, ((0, 0), (0, LANES - N_HEADS))).astype(BF16)
    bf_col = b_f[0].reshape(N_HEADS, 1)
    bf_row = jnp.pad(b_f[0].reshape(1, N_HEADS), ((0, 0), (0, LANES - N_HEADS)))
    lng = ln_v_g[0].reshape(1, SGU_W)
    lnb = ln_v_b[0].reshape(1, SGU_W)
    gidx = np.arange(SGU_W) // SGU_DIM
    gmat = jnp.asarray(np.where(gidx[:, None] == gidx[None, :], 1.0 / SGU_DIM, 0.0), BF16)
    ws_b = w_s[0]
    bs_full = jnp.repeat(b_s[0].T, SGU_DIM, axis=1)
    wst = w_s[0]
    bs_full_s = jnp.tile(jnp.repeat(b_s[0][:, :t_new].T, SGU_DIM, axis=1), (bs, 1))
    r_idx = np.arange(n_s)
    same = jnp.asarray((r_idx[:, None] // t_new == r_idx[None, :] // t_new)
                       & (r_idx[:, None] >= r_idx[None, :]), F32)
    pos = np.arange(TM_IN)
    tri = jnp.asarray(np.concatenate([pos[:, None] <= pos[None, :],
                                      np.ones((TM_IN, LANES), bool)], axis=1), BF16)
    pp = np.arange(PAGE)
    uo = jnp.asarray(np.concatenate([pp[:, None] > pp[None, :],
                                     np.ones((PAGE, PAGE), bool)], axis=1), BF16)
    wo = w_o[0].astype(BF16)
    nfc = dff // FC
    chunked = lambda a: a.reshape(a.shape[0], nfc, FC).transpose(1, 0, 2)
    wu = w_up[0].astype(BF16)
    wcg, wcv = chunked(w_conv[0][:, :dff]), chunked(w_conv[0][:, dff:])
    bcg, bcv = chunked(b_conv[:, :dff]), chunked(b_conv[:, dff:])
    wd = w_down[0].astype(BF16).reshape(nfc, FC, d)
    g1 = norm1_g
    g2 = norm2_g
    gf = final_g.reshape(1, d)

    c_all = jnp.concatenate([jnp.repeat(c_sample, t_new, axis=0), c_prompt], axis=0)
    mod = _ada(c_all, w_ada[0], b_ada)
    mod_p = mod[n_s:].reshape(bp, N_MOD, d)
    mod_s = mod

    q, kt, vt, ktb, vtb, lft, cumt, cum, sgu = _in_prompt(
        x_prompt, mod_p, g1, wq, wkvf_t, wus, bf_col, lng, lnb, gmat, ws_b, bs_full, tri)
    att = _attn_prompt(q, ktb, vtb, cumt, cum)
    new_k_p = kt.reshape(1, bp, N_HEADS, HEAD_DIM, s).transpose(0, 1, 4, 2, 3)
    new_v_p = vt.reshape(1, bp, N_HEADS, HEAD_DIM, s).transpose(0, 1, 4, 2, 3)
    new_lf_p = lft.transpose(0, 2, 1)[None]

    x2 = x_sample.reshape(n_s, d)
    q_s, k_s, v_s, lf_s, cq_s, svn_s, sgu_s = _in_sample(
        x2, mod_s, g1, wq, wkv, wf_pad, wus, bf_row, lng, lnb, gmat, wst, bs_full_s, same)
    cq3 = cq_s[:, :N_HEADS].reshape(bs, t_new, N_HEADS)
    cqcol = cq3.reshape(bs, t_new * N_HEADS, 1)
    cqmat = jnp.tile(cq3.transpose(0, 2, 1), (1, t_new, 1))
    kc = cache_k[0].transpose(0, 2, 3, 1).reshape(n_phys, ATT_W, PAGE)
    vc = cache_v[0].transpose(0, 2, 3, 1).reshape(n_phys, ATT_W, PAGE)
    lc = cache_logf[0].transpose(0, 2, 1)
    y_prompt, conv_p, att_s = _ffn_prompt_and_decode(
        page_table, x_prompt, att, sgu, mod_p, wo, g2, wu, wcg, wcv, bcg, bcv, wd, gf,
        q_s.reshape(bs, t_new, ATT_W), k_s.reshape(bs, t_new, ATT_W), v_s.reshape(bs, t_new, ATT_W),
        cqcol, cqmat, uo, kc, vc, lc)
    new_conv_p = conv_p[None]

    st2 = state_conv[0].reshape(2 * bs, 2 * dff)
    tok, src = np.arange(n_s)[:, None], np.arange(2 * bs)[None, :]
    smp, t_in = tok // t_new, tok % t_new
    e1 = jnp.asarray((t_in == 0) & (src == 2 * smp + 1), BF16)
    e2 = jnp.asarray((t_in <= 1) & (src == 2 * smp + t_in), BF16)
    y_s, upg, upv = _ffn_sample(x2, att_s.reshape(n_s, ATT_W), sgu_s, mod_s, wo, g2, wu, wcg, wcv,
                                bcg, bcv, st2, e1, e2, wd, gf)
    up = jnp.concatenate([upg, upv], axis=1).reshape(bs, t_new, 2 * dff)

    return (y_prompt, y_s.reshape(bs, t_new, d),
            new_k_p, new_v_p, new_lf_p, new_conv_p,
            k_s.reshape(1, bs, t_new, N_HEADS, HEAD_DIM), v_s.reshape(1, bs, t_new, N_HEADS, HEAD_DIM),
            lf_s[:, :N_HEADS].reshape(1, bs, t_new, N_HEADS),
            svn_s.reshape(1, bs, t_new, N_GROUPS, SGU_DIM),
            up[:, t_new - 2:, :][None])
```

```python
import functools

import jax
import jax.numpy as jnp
import numpy as np
from jax import lax
from jax.experimental import pallas as pl
from jax.experimental.pallas import tpu as pltpu

F32 = jnp.float32
BF16 = jnp.bfloat16

N_HEADS = 8
HEAD_DIM = 64
ATT_W = N_HEADS * HEAD_DIM
N_GROUPS = 8
SGU_DIM = 64
SGU_W = N_GROUPS * SGU_DIM
CHUNK = 128
PAGE = 128
DEC_T = 4
N_MOD = 6
EPS = 1e-6
ATT_SCALE = HEAD_DIM ** -0.5
LOG2E = 1.4426950408889634
LANES = 128
PAIR_W = 2 * HEAD_DIM
VMEM_LIMIT = 58 * 1024 * 1024

TM_IN = 512
TQ = 512
ATTN_SCORE_SLOTS = 3
TM_FFN = 512
FC = 256
PAGES_PER_GROUP = 6
DECODE_SLOTS = 6


def _cparams(sem):
    return pltpu.CompilerParams(dimension_semantics=sem, vmem_limit_bytes=VMEM_LIMIT)


def _const_spec(shape):
    nd = len(shape)
    return pl.BlockSpec(shape, lambda *_: (0,) * nd, pipeline_mode=pl.Buffered(1))


def _split3(x):
    hi = x.astype(BF16)
    r1 = x - hi.astype(F32)
    mid = r1.astype(BF16)
    lo = (r1 - mid.astype(F32)).astype(BF16)
    return hi, mid, lo


def _dot3(x, w):
    hi, mid, lo = _split3(x)
    d = functools.partial(jnp.dot, preferred_element_type=F32)
    return d(hi, w) + d(mid, w) + d(lo, w)


def _rms_mod(x, g, shift, scale):
    y = x * lax.rsqrt(jnp.mean(x * x, axis=-1, keepdims=True) + EPS)
    return (y * g) * (1.0 + scale) + shift


def _mod_rows(mod_ref, i):
    d = mod_ref.shape[1] // N_MOD
    return mod_ref[:, i * d:(i + 1) * d]


def _log_sigmoid(z):
    return jnp.minimum(z, 0.0) - jnp.log1p(jnp.exp(-jnp.abs(z)))


def _group_ln(sv, gmat, ln_g, ln_b):
    mu = jnp.dot(sv.astype(BF16), gmat, preferred_element_type=F32)
    d = sv - mu
    var = jnp.dot((d * d).astype(BF16), gmat, preferred_element_type=F32)
    return d * lax.rsqrt(var + EPS) * ln_g + ln_b


def _sgu_pairs(wm, svn_b, u, bias):
    lane = lax.broadcasted_iota(jnp.int32, (CHUNK, LANES), 1)
    outs = []
    for p in range(N_GROUPS // 2):
        rhs = svn_b[:, p * LANES:(p + 1) * LANES]
        lo = jnp.dot(wm[2 * p], rhs, preferred_element_type=F32)
        hi = jnp.dot(wm[2 * p + 1], rhs, preferred_element_type=F32)
        mixed = jnp.where(lane < SGU_DIM, lo, hi) + bias[:, p * LANES:(p + 1) * LANES]
        outs.append(u[:, p * LANES:(p + 1) * LANES] * mixed)
    return jnp.concatenate(outs, axis=1)


def _ada_kernel(c_ref, w_ref, b_ref, o_ref):
    a = jax.nn.silu(c_ref[...])
    o_ref[...] = jnp.dot(a.astype(BF16), w_ref[...].astype(BF16),
                         preferred_element_type=F32) + b_ref[...]


def _ada(c_all, w_ada, b_ada):
    n, d = c_all.shape
    cols = w_ada.shape[1]
    bn = d
    return pl.pallas_call(
        _ada_kernel,
        grid=(cols // bn,),
        in_specs=[pl.BlockSpec((n, d), lambda j: (0, 0)),
                  pl.BlockSpec((d, bn), lambda j: (0, j)),
                  pl.BlockSpec((1, bn), lambda j: (0, j))],
        out_specs=pl.BlockSpec((n, bn), lambda j: (0, j)),
        out_shape=jax.ShapeDtypeStruct((n, cols), F32),
        compiler_params=_cparams(("arbitrary",)),
        name="ada_ln",
    )(c_all, w_ada, b_ada)


def _in_prompt_kernel(x_ref, mod_ref, g1_ref, wq_ref, wkvf_ref, wus_ref, bf_ref, lng_ref, lnb_ref,
                      gmat_ref, ws_ref, bs_ref, tri_ref,
                      q_ref, kt_ref, vt_ref, ktb_ref, vtb_ref, lft_ref, cumt_ref, cum_ref, sgu_ref,
                      carry_ref):
    s = pl.program_id(1)
    tm = x_ref.shape[1]

    @pl.when(s == 0)
    def _():
        carry_ref[...] = jnp.zeros_like(carry_ref)

    h = _rms_mod(x_ref[0], g1_ref[...], mod_ref[0, 0:1, :], mod_ref[0, 1:2, :]).astype(BF16)

    q = jnp.dot(h, wq_ref[...], preferred_element_type=F32)
    q_ref[0] = (q * (ATT_SCALE * LOG2E)).astype(BF16)

    kvf = lax.dot_general(wkvf_ref[...], h, (((1,), (1,)), ((), ())), preferred_element_type=F32)
    kt = kvf[0:ATT_W]
    vt = kvf[ATT_W:2 * ATT_W]
    kt_ref[0] = kt
    vt_ref[0] = vt
    ktb_ref[0, 0] = kt.astype(BF16)
    vtb_ref[0, 0] = vt.astype(BF16)
    lft = _log_sigmoid(kvf[2 * ATT_W:2 * ATT_W + N_HEADS] + bf_ref[...])
    lft_ref[0] = lft

    r = _dot3(lft, tri_ref[...])
    carry = carry_ref[...]
    cumt = r[:, 0:tm] + jnp.concatenate([carry] * (tm // LANES), axis=1)
    carry_ref[...] = carry + r[:, tm:tm + LANES]
    cumt = cumt * LOG2E
    cumt_ref[0, 0] = cumt
    pad = jnp.zeros((LANES - N_HEADS, LANES), F32)
    for c in range(tm // LANES):
        blk = jnp.concatenate([cumt[:, c * LANES:(c + 1) * LANES], pad], axis=0)
        cum_ref[0, c * LANES:(c + 1) * LANES, :] = blk.T[:, 0:N_HEADS]

    us = jnp.dot(h, wus_ref[...], preferred_element_type=F32)
    u = jax.nn.gelu(us[:, 0:SGU_W])
    sv = jax.nn.gelu(us[:, SGU_W:2 * SGU_W])
    svn = _group_ln(sv, gmat_ref[...], lng_ref[...], lnb_ref[...]).astype(BF16)

    row = lax.broadcasted_iota(jnp.int32, (CHUNK, CHUNK), 0)
    col = lax.broadcasted_iota(jnp.int32, (CHUNK, CHUNK), 1)
    wm = [jnp.where(row >= col, ws_ref[g], 0.0).astype(BF16) for g in range(N_GROUPS)]
    bias = bs_ref[...]
    for c in range(tm // CHUNK):
        sl = slice(c * CHUNK, (c + 1) * CHUNK)
        sgu_ref[0, sl, :] = _sgu_pairs(wm, svn[sl], u[sl], bias).astype(BF16)


def _in_prompt(x, mod3, g1, wq, wkvf, wus, bf_col, lng, lnb, gmat, ws_b, bs_full, tri):
    b, s, d = x.shape
    tm = TM_IN
    ns = s // tm
    grid = (b, ns)
    row_blk = lambda w: pl.BlockSpec((1, tm, w), lambda i, j: (i, j, 0))
    colT_blk = lambda r: pl.BlockSpec((1, r, tm), lambda i, j: (i, 0, j))
    tiledT_blk = lambda r: pl.BlockSpec((1, 1, r, tm), lambda i, j: (i, j, 0, 0))
    out_shape = (
        jax.ShapeDtypeStruct((b, s, ATT_W), BF16),
        jax.ShapeDtypeStruct((b, ATT_W, s), F32),
        jax.ShapeDtypeStruct((b, ATT_W, s), F32),
        jax.ShapeDtypeStruct((b, ns, ATT_W, tm), BF16),
        jax.ShapeDtypeStruct((b, ns, ATT_W, tm), BF16),
        jax.ShapeDtypeStruct((b, N_HEADS, s), F32),
        jax.ShapeDtypeStruct((b, ns, N_HEADS, tm), F32),
        jax.ShapeDtypeStruct((b, s, N_HEADS), F32),
        jax.ShapeDtypeStruct((b, s, SGU_W), BF16),
    )
    out_specs = (row_blk(ATT_W), colT_blk(ATT_W), colT_blk(ATT_W), tiledT_blk(ATT_W), tiledT_blk(ATT_W),
                 colT_blk(N_HEADS), tiledT_blk(N_HEADS), row_blk(N_HEADS), row_blk(SGU_W))
    in_specs = [row_blk(d),
                pl.BlockSpec((1, N_MOD, d), lambda i, j: (i, 0, 0)),
                _const_spec(g1.shape), _const_spec(wq.shape), _const_spec(wkvf.shape), _const_spec(wus.shape),
                _const_spec(bf_col.shape), _const_spec(lng.shape), _const_spec(lnb.shape),
                _const_spec(gmat.shape), _const_spec(ws_b.shape), _const_spec(bs_full.shape),
                _const_spec(tri.shape)]
    return pl.pallas_call(
        _in_prompt_kernel, grid=grid, in_specs=in_specs, out_specs=out_specs, out_shape=out_shape,
        scratch_shapes=[pltpu.VMEM((N_HEADS, LANES), F32)],
        compiler_params=_cparams(("arbitrary", "arbitrary")),
        name="prompt_in_proj",
    )(x, mod3, g1, wq, wkvf, wus, bf_col, lng, lnb, gmat, ws_b, bs_full, tri)


def _attn_kernel(q_ref, kt_ref, vt_ref, cumt_ref, cum_ref, o_ref, qm_ref, s_ref, cq_ref, m_ref, acc_ref):
    qi = pl.program_id(1)
    tq = q_ref.shape[1]
    tk = kt_ref.shape[3]
    qs = qi * tq
    j_diag = (qs + tq - 1) // tk
    n_slots = s_ref.shape[0]
    ahead = n_slots - 1
    lane = lax.broadcasted_iota(jnp.int32, (tq, LANES), 1)
    qpos = qs + lax.broadcasted_iota(jnp.int32, (tq, tk), 0)
    kcol = lax.broadcasted_iota(jnp.int32, (tq, tk), 1)
    ones_half = jnp.ones((HEAD_DIM, tk), BF16)
    nt = (((1,), (1,)), ((), ()))

    for p in range(N_HEADS // 2):
        qp = q_ref[0, :, p * PAIR_W:(p + 1) * PAIR_W].astype(F32)
        qm_ref[2 * p] = jnp.where(lane < HEAD_DIM, qp, 0.0).astype(BF16)
        qm_ref[2 * p + 1] = jnp.where(lane >= HEAD_DIM, qp, 0.0).astype(BF16)
    for h in range(N_HEADS):
        cq_ref[h] = jnp.broadcast_to(cum_ref[0, :, h:h + 1], (tq, LANES))
    m_ref[...] = jnp.full(m_ref.shape, -jnp.inf, F32)
    acc_ref[...] = jnp.zeros(acc_ref.shape, F32)

    def scores(j, h):
        p = h // 2
        kt = kt_ref[0, j, p * PAIR_W:(p + 1) * PAIR_W, :]
        s_ref[h % n_slots] = jnp.dot(qm_ref[h], kt, preferred_element_type=F32)

    def softmax_pv(j, h, masked):
        p, half = divmod(h, 2)
        t = s_ref[h % n_slots] - cumt_ref[0, j, h:h + 1, :]
        if masked:
            t = jnp.where(j * tk + kcol <= qpos, t, -jnp.inf)
        cq = cq_ref[h]
        m = m_ref[h]
        m_new = jnp.maximum(m, jnp.max(t, axis=1, keepdims=True) + cq)
        alpha = jnp.exp2(m - m_new)
        d = cq - m_new
        pr = jnp.exp2(t + jnp.concatenate([d] * (tk // LANES), axis=1)).astype(BF16)
        vt = vt_ref[0, j, p * PAIR_W:(p + 1) * PAIR_W, :]
        if half == 0:
            vaug = jnp.concatenate([vt[0:HEAD_DIM], ones_half], axis=0)
        else:
            vaug = jnp.concatenate([ones_half, vt[HEAD_DIM:PAIR_W]], axis=0)
        acc_ref[h] = alpha * acc_ref[h] + lax.dot_general(pr, vaug, nt, preferred_element_type=F32)
        m_ref[h] = m_new

    def key_tile(j, masked):
        for h in range(N_HEADS):
            softmax_pv(j, h, masked)
            nh = h + ahead
            if nh < N_HEADS:
                scores(j, nh)
            elif not masked:
                scores(j + 1, nh - N_HEADS)

    for h in range(ahead):
        scores(0, h)

    def loop_body(j, carry):
        key_tile(j, False)
        return carry

    lax.fori_loop(0, j_diag, loop_body, 0)
    key_tile(j_diag, True)

    for p in range(N_HEADS // 2):
        a0 = acc_ref[2 * p]
        a1 = acc_ref[2 * p + 1]
        o0 = a0 / pltpu.roll(a0, HEAD_DIM, axis=1)
        o1 = a1 / pltpu.roll(a1, HEAD_DIM, axis=1)
        o_ref[0, :, p * PAIR_W:(p + 1) * PAIR_W] = jnp.where(lane < HEAD_DIM, o0, o1).astype(BF16)


def _attn_prompt(q, ktb, vtb, cumt, cum):
    b, s, _ = q.shape
    nk, tk = ktb.shape[1], ktb.shape[3]
    tq = TQ
    return pl.pallas_call(
        _attn_kernel,
        grid=(b, s // tq),
        in_specs=[pl.BlockSpec((1, tq, ATT_W), lambda i, j: (i, j, 0)),
                  pl.BlockSpec((1, nk, ATT_W, tk), lambda i, j: (i, 0, 0, 0)),
                  pl.BlockSpec((1, nk, ATT_W, tk), lambda i, j: (i, 0, 0, 0)),
                  pl.BlockSpec((1, nk, N_HEADS, tk), lambda i, j: (i, 0, 0, 0)),
                  pl.BlockSpec((1, tq, N_HEADS), lambda i, j: (i, j, 0))],
        out_specs=pl.BlockSpec((1, tq, ATT_W), lambda i, j: (i, j, 0)),
        out_shape=jax.ShapeDtypeStruct((b, s, ATT_W), BF16),
        scratch_shapes=[pltpu.VMEM((N_HEADS, tq, PAIR_W), BF16),
                        pltpu.VMEM((ATTN_SCORE_SLOTS, tq, tk), F32),
                        pltpu.VMEM((N_HEADS, tq, LANES), F32),
                        pltpu.VMEM((N_HEADS, tq, LANES), F32),
                        pltpu.VMEM((N_HEADS, tq, PAIR_W), F32)],
        compiler_params=_cparams(("arbitrary", "arbitrary")),
        name="prompt_attention",
    )(q, ktb, vtb, cumt, cum)


def _transpose_rows8(blocks):
    sub = lax.broadcasted_iota(jnp.int32, blocks[0].shape, 0)
    a = list(blocks)
    for s in (4, 2, 1):
        keep = (sub & s) == 0
        b = list(a)
        for i in range(8):
            if i & s == 0:
                lo, hi = a[i], a[i + s]
                b[i] = jnp.where(keep, lo, pltpu.roll(hi, s, axis=0))
                b[i + s] = jnp.where(keep, pltpu.roll(lo, 8 - s, axis=0), hi)
        a = b
    return a


def _interleave_rows(x, nv):
    out = [None] * nv
    for jh in range(nv // 8):
        t = _transpose_rows8([x[r * nv + jh * 8:r * nv + jh * 8 + 8, :] for r in range(8)])
        for jl in range(8):
            out[8 * jh + jl] = t[jl]
    return jnp.concatenate(out, axis=0)


def _deinterleave_rows(y, nv):
    nb = nv // 8
    out = [None] * nv
    for jh in range(nb):
        t = _transpose_rows8([y[(8 * jh + jl) * 8:(8 * jh + jl) * 8 + 8, :] for jl in range(8)])
        for r in range(8):
            out[r * nb + jh] = t[r]
    return jnp.concatenate(out, axis=0)


def _causal_conv3(up, prev_ref, c, wc_ref, bc_ref):
    th = up.shape[0]
    first_row = lax.broadcasted_iota(jnp.int32, (8, up.shape[1]), 0) == 0

    def wrap(prev_blk, cur_blk):
        return jnp.where(first_row, pltpu.roll(prev_blk, 1, axis=0), pltpu.roll(cur_blk, 1, axis=0))

    w1 = wrap(prev_ref[c, 8:16, :], up[th - 8:th])
    w2 = wrap(prev_ref[c, 0:8, :], up[th - 16:th - 8])
    s1 = jnp.concatenate([w1, up[0:th - 8]], axis=0)
    s2 = jnp.concatenate([w2, w1, up[0:th - 16]], axis=0)
    prev_ref[c] = up[th - 16:th]
    return bc_ref[c] + s2 * wc_ref[c, 0:1, :] + s1 * wc_ref[c, 1:2, :] + up * wc_ref[c, 2:3, :]


def _in_sample_kernel(x_ref, mod_ref, g1_ref, wq_ref, wkv_ref, wf_ref, wus_ref, bf_ref, lng_ref, lnb_ref,
                      gmat_ref, wst_ref, bs_ref, same_ref,
                      q_ref, k_ref, v_ref, lf_ref, cq_ref, svn_ref, sgu_ref):
    n = x_ref.shape[0]
    h = _rms_mod(x_ref[...], g1_ref[...], _mod_rows(mod_ref, 0), _mod_rows(mod_ref, 1)).astype(BF16)
    q_ref[...] = jnp.dot(h, wq_ref[...], preferred_element_type=F32) * ATT_SCALE
    kv = jnp.dot(h, wkv_ref[...], preferred_element_type=F32)
    k_ref[...] = kv[:, 0:ATT_W]
    v_ref[...] = kv[:, ATT_W:2 * ATT_W]
    lf = _log_sigmoid(jnp.dot(h, wf_ref[...], preferred_element_type=F32) + bf_ref[...])
    lf_ref[...] = lf

    same = same_ref[...]
    hi, mid, lo = _split3(lf)
    tri = same.astype(BF16)
    d = functools.partial(jnp.dot, preferred_element_type=F32)
    cq_ref[...] = d(tri, hi) + d(tri, mid) + d(tri, lo)

    us = jnp.dot(h, wus_ref[...], preferred_element_type=F32)
    u = jax.nn.gelu(us[:, 0:SGU_W])
    sv = jax.nn.gelu(us[:, SGU_W:2 * SGU_W])
    svn = _group_ln(sv, gmat_ref[...], lng_ref[...], lnb_ref[...])
    svn_ref[...] = svn
    r_i = lax.broadcasted_iota(jnp.int32, (n, n), 0)
    c_i = lax.broadcasted_iota(jnp.int32, (n, n), 1)
    pick_rows = jnp.where(c_i == r_i % DEC_T, 1.0, 0.0).astype(BF16)
    pick_cols = jnp.where(r_i == c_i % DEC_T, 1.0, 0.0).astype(BF16)
    wm = []
    for g in range(N_GROUPS):
        rows = jnp.dot(pick_rows, wst_ref[g].astype(BF16), preferred_element_type=F32)
        tiled = jnp.dot(rows.astype(BF16), pick_cols, preferred_element_type=F32)
        wm.append(jnp.where(same > 0, tiled, 0.0).astype(BF16))
    sgu_ref[...] = _sgu_pairs(wm, svn.astype(BF16), u, bs_ref[...]).astype(BF16)


def _in_sample(x2, mod_rows, g1, wq, wkv, wf, wus, bf_row, lng, lnb, gmat, wst, bs_full, same):
    n, d = x2.shape
    args = (x2, mod_rows, g1, wq, wkv, wf, wus, bf_row, lng, lnb, gmat, wst, bs_full, same)
    out_shape = (jax.ShapeDtypeStruct((n, ATT_W), F32), jax.ShapeDtypeStruct((n, ATT_W), F32),
                 jax.ShapeDtypeStruct((n, ATT_W), F32), jax.ShapeDtypeStruct((n, LANES), F32),
                 jax.ShapeDtypeStruct((n, LANES), F32), jax.ShapeDtypeStruct((n, SGU_W), F32),
                 jax.ShapeDtypeStruct((n, SGU_W), BF16))
    return pl.pallas_call(
        _in_sample_kernel, grid=(1,),
        in_specs=[_const_spec((n,) + a.shape[1:] if a is mod_rows else a.shape) for a in args],
        out_specs=tuple(pl.BlockSpec(o.shape, lambda i: (0, 0)) for o in out_shape),
        out_shape=out_shape,
        compiler_params=_cparams(("arbitrary",)),
        name="sample_in_proj",
    )(*args)


def _decode_begin(q, kn, vn, cqcol, cqmat_ref, bb):
    t_new = q.shape[0]
    rows = t_new * N_HEADS
    sub = lax.broadcasted_iota(jnp.int32, (N_HEADS, ATT_W), 0)
    lane_head = lax.broadcasted_iota(jnp.int32, (N_HEADS, ATT_W), 1) // HEAD_DIM
    own = sub == lane_head
    qbd = jnp.concatenate(
        [jnp.where(own, jnp.broadcast_to(q[t:t + 1, :], (N_HEADS, ATT_W)), 0.0) for t in range(t_new)], axis=0)
    rt = lax.broadcasted_iota(jnp.int32, (rows, 1), 0) // N_HEADS
    s_new = []
    for t2 in range(t_new):
        sc = jnp.sum(qbd * kn[t2:t2 + 1, :], axis=1, keepdims=True) + cqcol - cqmat_ref[bb, :, t2:t2 + 1]
        s_new.append(jnp.where(rt >= t2, sc, -jnp.inf))
    m = s_new[0]
    for t2 in range(1, t_new):
        m = jnp.maximum(m, s_new[t2])
    l = jnp.zeros((rows, 1), F32)
    acc = jnp.zeros((rows, ATT_W), F32)
    for t2 in range(t_new):
        pr = jnp.exp(s_new[t2] - m)
        l = l + pr
        acc = acc + pr * vn[t2:t2 + 1, :]
    return dict(qbd_b=qbd.astype(BF16), cqcol=cqcol, own=own, t_new=t_new,
                m=m, l=l, acc=acc, c_run=jnp.zeros((N_HEADS, PAGE), F32))


def _decode_scores(st, kbuf, lbuf, slot, uo, valid):
    g_pages = kbuf.shape[1]
    t_new = st["t_new"]
    c_run = st["c_run"]
    lp = lbuf[slot].reshape(g_pages * N_HEADS, PAGE)
    r = _dot3(lp, uo)
    sufs = [None] * g_pages
    for i in reversed(range(g_pages)):
        sufs[i] = jnp.where(valid[i], r[i * N_HEADS:(i + 1) * N_HEADS, 0:PAGE] + c_run, -jnp.inf)
        c_run = c_run + r[i * N_HEADS:(i + 1) * N_HEADS, PAGE:2 * PAGE]
    scs = []
    for i in range(0, g_pages, 2):
        kt2 = jnp.concatenate([kbuf[slot, i].astype(BF16), kbuf[slot, i + 1].astype(BF16)], axis=1)
        sc = jnp.dot(st["qbd_b"], kt2, preferred_element_type=F32)
        bias = jnp.concatenate([jnp.concatenate([sufs[i]] * t_new, axis=0),
                                jnp.concatenate([sufs[i + 1]] * t_new, axis=0)], axis=1)
        scs.append(sc + bias + st["cqcol"])
    return scs, dict(st, c_run=c_run)


def _decode_update(st, scs, vbuf, slot):
    nt = (((1,), (1,)), ((), ()))
    m, l, acc = st["m"], st["l"], st["acc"]
    m_new = m
    for sc in scs:
        m_new = jnp.maximum(m_new, jnp.max(sc, axis=1, keepdims=True))
    alpha = jnp.exp(m - m_new)
    l = alpha * l
    acc = alpha * acc
    for idx, sc in enumerate(scs):
        i = 2 * idx
        pr = jnp.exp(sc - m_new)
        l = l + jnp.sum(pr, axis=1, keepdims=True)
        vt2 = jnp.concatenate([vbuf[slot, i].astype(BF16), vbuf[slot, i + 1].astype(BF16)], axis=1)
        acc = acc + lax.dot_general(pr.astype(BF16), vt2, nt, preferred_element_type=F32)
    return dict(st, m=m_new, l=l, acc=acc)


def _decode_end(st, o_ref, bb):
    o = st["acc"] / st["l"]
    for t in range(st["t_new"]):
        blk = jnp.where(st["own"], o[t * N_HEADS:(t + 1) * N_HEADS, :], 0.0)
        o_ref[bb, t:t + 1, :] = jnp.sum(blk, axis=0, keepdims=True)


def _ffn_decode_kernel(pt_ref, x_ref, att_ref, sgu_ref, mod_ref, wo_ref, g2_ref, wu_ref, wcg_ref,
                       wcv_ref, bcg_ref, bcv_ref, wd_ref, gf_ref,
                       q_ref, kn_ref, vn_ref, cqcol_ref, cqmat_ref, uo_ref, kc_hbm, vc_hbm, lc_hbm,
                       y_ref, conv_ref, o_ref,
                       x1_ref, h2_ref, f_ref, pg_ref, pv_ref, kbuf, vbuf, lbuf, sem):
    s = pl.program_id(1)
    step = pl.program_id(0) * pl.num_programs(1) + s
    n_steps = pl.num_programs(0) * pl.num_programs(1)
    tm = x_ref.shape[1]
    n_chunks = wd_ref.shape[0]
    th = tm // 2
    nv = th // 8
    halves = (slice(0, th), slice(th, tm))
    n_pages = pt_ref.shape[1]
    g_pages = kbuf.shape[1]
    n_regions = 2 * n_chunks
    n_slots = kbuf.shape[0]
    ahead = n_slots - 2
    total = n_steps * n_regions

    def page_of(r, i):
        return n_pages - g_pages * (r + 1) + i

    def copies(n, slot):
        bb = n // n_regions
        r = n % n_regions
        out = []
        for i in range(g_pages):
            phys = pt_ref[bb, jnp.maximum(page_of(r, i), 0)]
            out.append(pltpu.make_async_copy(kc_hbm.at[phys], kbuf.at[slot, i], sem.at[0, slot]))
            out.append(pltpu.make_async_copy(vc_hbm.at[phys], vbuf.at[slot, i], sem.at[1, slot]))
            out.append(pltpu.make_async_copy(lc_hbm.at[phys], lbuf.at[slot, i], sem.at[2, slot]))
        return out

    @pl.when(step == 0)
    def _():
        for n0 in range(ahead):
            for c in copies(n0, n0):
                c.start()

    @pl.when(s == 0)
    def _():
        pg_ref[...] = jnp.zeros(pg_ref.shape, F32)
        pv_ref[...] = jnp.zeros(pv_ref.shape, F32)

    for i, rows in enumerate(halves):
        mix = (jnp.dot(att_ref[0, rows, :], wo_ref[0:ATT_W, :], preferred_element_type=F32)
               + jnp.dot(sgu_ref[0, rows, :], wo_ref[ATT_W:ATT_W + SGU_W, :], preferred_element_type=F32))
        x1 = _interleave_rows(x_ref[0, rows, :] + mod_ref[0, 2:3, :] * mix, nv)
        x1_ref[rows, :] = x1
        h2_ref[i] = _rms_mod(x1, g2_ref[...], mod_ref[0, 3:4, :], mod_ref[0, 4:5, :]).astype(BF16)
    f_ref[...] = jnp.zeros(f_ref.shape, F32)

    uo = uo_ref[...]
    st0 = _decode_begin(q_ref[0], kn_ref[0], vn_ref[0], cqcol_ref[0], cqmat_ref, 0)

    def trip(c, carry):
        st = dict(st0, m=carry[0], l=carry[1], acc=carry[2], c_run=carry[3])
        regions = [2 * c, 2 * c + 1]
        ns_ = [step * n_regions + r for r in regions]
        slots = [n % n_slots for n in ns_]
        for n in ns_:
            @pl.when(n + ahead < total)
            def _(n=n):
                for cp in copies(n + ahead, (n + ahead) % n_slots):
                    cp.start()
        for n, slot in zip(ns_, slots):
            for cp in copies(n, slot):
                cp.wait()
        gate = pl.ds(pl.multiple_of(c * FC, FC), FC)
        value = pl.ds(pl.multiple_of((n_chunks + c) * FC, FC), FC)
        ups = [(jnp.dot(h2_ref[i], wu_ref[:, gate], preferred_element_type=F32),
                jnp.dot(h2_ref[i], wu_ref[:, value], preferred_element_type=F32)) for i in range(2)]
        scs = []
        for r, slot in zip(regions, slots):
            valid = [page_of(r, k) >= 0 for k in range(g_pages)]
            sc, st = _decode_scores(st, kbuf, lbuf, slot, uo, valid)
            scs.append(sc)

        def down(i):
            cg = _causal_conv3(ups[i][0], pg_ref, c, wcg_ref, bcg_ref)
            cv = _causal_conv3(ups[i][1], pv_ref, c, wcv_ref, bcv_ref)
            act = (jax.nn.silu(cg) * cv).astype(BF16)
            f_ref[i] += jnp.dot(act, wd_ref[c], preferred_element_type=F32)

        down(0)
        down(1)
        for sc, slot in zip(scs, slots):
            st = _decode_update(st, sc, vbuf, slot)
        return st["m"], st["l"], st["acc"], st["c_run"]

    m, l, acc, c_run = lax.fori_loop(0, n_chunks, trip, (st0["m"], st0["l"], st0["acc"], st0["c_run"]))
    _decode_end(dict(st0, m=m, l=l, acc=acc, c_run=c_run), o_ref, 0)

    for i, rows in enumerate(halves):
        x2 = x1_ref[rows, :] + mod_ref[0, 5:6, :] * f_ref[i]
        y = x2 * lax.rsqrt(jnp.mean(x2 * x2, axis=-1, keepdims=True) + EPS) * gf_ref[...]
        y_ref[0, rows, :] = _deinterleave_rows(y, nv)
    fc = pg_ref.shape[2]
    for c in range(n_chunks):
        for k, row in enumerate((7, 15)):
            conv_ref[0, k:k + 1, c * fc:(c + 1) * fc] = pg_ref[c, row:row + 1, :]
            conv_ref[0, k:k + 1, (n_chunks + c) * fc:(n_chunks + c + 1) * fc] = pv_ref[c, row:row + 1, :]


def _ffn_prompt_and_decode(page_table, x, att, sgu, mod3, wo, g2, wu, wcg, wcv, bcg, bcv, wd, gf,
                           q3, k3, v3, cqcol, cqmat, uo, kc, vc, lc):
    b, s, d = x.shape
    tm = TM_FFN
    ns = s // tm
    n_chunks, fc, _ = wd.shape
    dff = n_chunks * fc
    nb, t_new, _ = q3.shape
    g = PAGES_PER_GROUP
    assert nb == b * ns and -(-page_table.shape[1] // g) == 2 * n_chunks and g % 2 == 0
    rows = t_new * N_HEADS
    consts = (wo, g2, wu, wcg, wcv, bcg, bcv, wd, gf)
    row_blk = lambda w: pl.BlockSpec((1, tm, w), lambda i, j, pt: (i, j, 0))
    smp_blk = lambda r, w: pl.BlockSpec((1, r, w), lambda i, j, pt: (i * ns + j, 0, 0))
    const_blk = lambda a: pl.BlockSpec(a.shape, lambda i, j, pt: (0,) * a.ndim, pipeline_mode=pl.Buffered(1))
    grid_spec = pltpu.PrefetchScalarGridSpec(
        num_scalar_prefetch=1,
        grid=(b, ns),
        in_specs=[row_blk(d), row_blk(ATT_W), row_blk(SGU_W),
                  pl.BlockSpec((1, N_MOD, d), lambda i, j, pt: (i, 0, 0))]
                 + [const_blk(a) for a in consts]
                 + [smp_blk(t_new, ATT_W), smp_blk(t_new, ATT_W), smp_blk(t_new, ATT_W),
                    smp_blk(rows, 1), smp_blk(rows, t_new), const_blk(uo),
                    pl.BlockSpec(memory_space=pl.ANY), pl.BlockSpec(memory_space=pl.ANY),
                    pl.BlockSpec(memory_space=pl.ANY)],
        out_specs=(row_blk(d), pl.BlockSpec((1, 2, 2 * dff), lambda i, j, pt: (i, 0, 0)),
                   smp_blk(t_new, ATT_W)),
        scratch_shapes=[pltpu.VMEM((tm, d), F32),
                        pltpu.VMEM((2, tm // 2, d), BF16),
                        pltpu.VMEM((2, tm // 2, d), F32),
                        pltpu.VMEM((n_chunks, 16, fc), F32),
                        pltpu.VMEM((n_chunks, 16, fc), F32),
                        pltpu.VMEM((DECODE_SLOTS, g, ATT_W, PAGE), F32),
                        pltpu.VMEM((DECODE_SLOTS, g, ATT_W, PAGE), F32),
                        pltpu.VMEM((DECODE_SLOTS, g, N_HEADS, PAGE), F32),
                        pltpu.SemaphoreType.DMA((3, DECODE_SLOTS))],
    )
    return pl.pallas_call(
        _ffn_decode_kernel, grid_spec=grid_spec,
        out_shape=(jax.ShapeDtypeStruct((b, s, d), F32), jax.ShapeDtypeStruct((b, 2, 2 * dff), F32),
                   jax.ShapeDtypeStruct((nb, t_new, ATT_W), F32)),
        compiler_params=_cparams(("arbitrary", "arbitrary")),
        name="prompt_ffn_decode",
    )(page_table, x, att, sgu, mod3, *consts, q3, k3, v3, cqcol, cqmat, uo, kc, vc, lc)


def _ffn_sample_kernel(x_ref, att_ref, sgu_ref, mod_ref, wo_ref, g2_ref, wug_ref, wuv_ref, wcg_ref, wcv_ref,
                       bcg_ref, bcv_ref, stg_ref, stv_ref, e1_ref, e2_ref, wd_ref, gf_ref,
                       y_ref, upg_ref, upv_ref,
                       x1_ref, h2_ref, acc_ref):
    j = pl.program_id(0)
    n = x_ref.shape[0]
    t_new = 4

    @pl.when(j == 0)
    def _():
        mix = (jnp.dot(att_ref[...].astype(BF16), wo_ref[0:ATT_W, :], preferred_element_type=F32)
               + jnp.dot(sgu_ref[...], wo_ref[ATT_W:ATT_W + SGU_W, :], preferred_element_type=F32))
        x1 = x_ref[...] + _mod_rows(mod_ref, 2) * mix
        x1_ref[...] = x1
        h2_ref[...] = _rms_mod(x1, g2_ref[...], _mod_rows(mod_ref, 3), _mod_rows(mod_ref, 4)).astype(BF16)
        acc_ref[...] = jnp.zeros_like(acc_ref)

    h2 = h2_ref[...]
    tpos = lax.broadcasted_iota(jnp.int32, (n, FC), 0) % t_new

    def place(e_ref, parts):
        d = functools.partial(jnp.dot, preferred_element_type=F32)
        return d(e_ref[...], parts[0]) + d(e_ref[...], parts[1]) + d(e_ref[...], parts[2])

    def conv(up, wc_ref, bc_ref, st_ref):
        parts = _split3(st_ref[...])
        s1 = jnp.where(tpos >= 1, pltpu.roll(up, 1, axis=0), place(e1_ref, parts))
        s2 = jnp.where(tpos >= 2, pltpu.roll(up, 2, axis=0), place(e2_ref, parts))
        return bc_ref[0] + s2 * wc_ref[0, 0:1, :] + s1 * wc_ref[0, 1:2, :] + up * wc_ref[0, 2:3, :]

    upg = jnp.dot(h2, wug_ref[...], preferred_element_type=F32)
    upv = jnp.dot(h2, wuv_ref[...], preferred_element_type=F32)
    upg_ref[...] = upg
    upv_ref[...] = upv
    cg = conv(upg, wcg_ref, bcg_ref, stg_ref)
    cv = conv(upv, wcv_ref, bcv_ref, stv_ref)
    act = (jax.nn.silu(cg) * cv).astype(BF16)
    acc_ref[...] += jnp.dot(act, wd_ref[0], preferred_element_type=F32)

    @pl.when(j == pl.num_programs(0) - 1)
    def _():
        x2 = x1_ref[...] + _mod_rows(mod_ref, 5) * acc_ref[...]
        y_ref[...] = x2 * lax.rsqrt(jnp.mean(x2 * x2, axis=-1, keepdims=True) + EPS) * gf_ref[...]


def _ffn_sample(x2, att, sgu, mod_rows, wo, g2, wu, wcg, wcv, bcg, bcv, st2, e1, e2, wd, gf):
    n, d = x2.shape
    nf = wd.shape[0]
    dff = nf * FC
    full = lambda a: _const_spec(a.shape)
    colc = lambda r: pl.BlockSpec((r, FC), lambda j: (0, j))
    chunk = lambda a: pl.BlockSpec((1,) + a.shape[1:], lambda j: (j, 0, 0))
    in_specs = [full(x2), full(att), full(sgu), _const_spec((n, mod_rows.shape[1])), full(wo), full(g2),
                colc(d), pl.BlockSpec((d, FC), lambda j: (0, nf + j)),
                chunk(wcg), chunk(wcv), chunk(bcg), chunk(bcv),
                pl.BlockSpec((st2.shape[0], FC), lambda j: (0, j)),
                pl.BlockSpec((st2.shape[0], FC), lambda j: (0, nf + j)),
                full(e1), full(e2),
                chunk(wd), full(gf)]
    return pl.pallas_call(
        _ffn_sample_kernel, grid=(nf,),
        in_specs=in_specs,
        out_specs=(pl.BlockSpec((n, d), lambda j: (0, 0)), colc(n), colc(n)),
        out_shape=(jax.ShapeDtypeStruct((n, d), F32), jax.ShapeDtypeStruct((n, dff), F32),
                   jax.ShapeDtypeStruct((n, dff), F32)),
        scratch_shapes=[pltpu.VMEM((n, d), F32), pltpu.VMEM((n, d), BF16), pltpu.VMEM((n, d), F32)],
        compiler_params=_cparams(("arbitrary",)),
        name="sample_ffn",
    )(x2, att, sgu, mod_rows, wo, g2, wu, wu, wcg, wcv, bcg, bcv, st2, st2, e1, e2, wd, gf)


def kernel(x_prompt, x_sample, c_prompt, c_sample, cache_k, cache_v, cache_logf, state_conv, page_table,
           w_ada, b_ada, norm1_g, w_in, b_f, ln_v_g, ln_v_b, w_s, b_s, w_o, norm2_g, w_up, w_conv, b_conv,
           w_down, final_g):
    bp, s, d = x_prompt.shape
    bs, t_new, _ = x_sample.shape
    n_s = bs * t_new
    dff = w_down.shape[1]
    n_phys = cache_k.shape[1]
    assert w_ada.shape[0] == 1, "single layer"
    assert s % TM_IN == 0 and s % TQ == 0 and s % TM_FFN == 0 and TM_IN % TQ == 0
    assert dff % FC == 0 and n_s == CHUNK and t_new == 4
    assert bs == bp * (s // TM_FFN), "one decode sample rides along with each FFN grid step"

    wi = w_in[0]
    k0, v0, f0 = ATT_W, 2 * ATT_W, 3 * ATT_W
    u0 = f0 + N_HEADS
    wq = wi[:, 0:k0].astype(BF16)
    wkv = wi[:, k0:f0].astype(BF16)
    wf = wi[:, f0:u0]
    wus = wi[:, u0:].astype(BF16)
    wkvf_t = jnp.concatenate([wi[:, k0:f0], wf], axis=1).T.astype(BF16)
    wf_pad = jnp.pad(wf, ((0, 0), (0, LANES - N_HEADS))).astype(BF16)
    bf_col = b_f[0].reshape(N_HEADS, 1)
    bf_row = jnp.pad(b_f[0].reshape(1, N_HEADS), ((0, 0), (0, LANES - N_HEADS)))
    lng = ln_v_g[0].reshape(1, SGU_W)
    lnb = ln_v_b[0].reshape(1, SGU_W)
    gidx = np.arange(SGU_W) // SGU_DIM
    gmat = jnp.asarray(np.where(gidx[:, None] == gidx[None, :], 1.0 / SGU_DIM, 0.0), BF16)
    ws_b = w_s[0]
    bs_full = jnp.repeat(b_s[0].T, SGU_DIM, axis=1)
    wst = w_s[0]
    bs_full_s = jnp.tile(jnp.repeat(b_s[0][:, :t_new].T, SGU_DIM, axis=1), (bs, 1))
    r_idx = np.arange(n_s)
    same = jnp.asarray((r_idx[:, None] // t_new == r_idx[None, :] // t_new)
                       & (r_idx[:, None] >= r_idx[None, :]), F32)
    pos = np.arange(TM_IN)
    tri = jnp.asarray(np.concatenate([pos[:, None] <= pos[None, :],
                                      np.ones((TM_IN, LANES), bool)], axis=1), BF16)
    pp = np.arange(PAGE)
    uo = jnp.asarray(np.concatenate([pp[:, None] > pp[None, :],
                                     np.ones((PAGE, PAGE), bool)], axis=1), BF16)
    wo = w_o[0].astype(BF16)
    nfc = dff // FC
    chunked = lambda a: a.reshape(a.shape[0], nfc, FC).transpose(1, 0, 2)
    wu = w_up[0].astype(BF16)
    wcg, wcv = chunked(w_conv[0][:, :dff]), chunked(w_conv[0][:, dff:])
    bcg, bcv = chunked(b_conv[:, :dff]), chunked(b_conv[:, dff:])
    wd = w_down[0].astype(BF16).reshape(nfc, FC, d)
    g1 = norm1_g
    g2 = norm2_g
    gf = final_g.reshape(1, d)

    c_all = jnp.concatenate([jnp.repeat(c_sample, t_new, axis=0), c_prompt], axis=0)
    mod = _ada(c_all, w_ada[0], b_ada)
    mod_p = mod[n_s:].reshape(bp, N_MOD, d)
    mod_s = mod

    q, kt, vt, ktb, vtb, lft, cumt, cum, sgu = _in_prompt(
        x_prompt, mod_p, g1, wq, wkvf_t, wus, bf_col, lng, lnb, gmat, ws_b, bs_full, tri)
    att = _attn_prompt(q, ktb, vtb, cumt, cum)
    new_k_p = kt.reshape(1, bp, N_HEADS, HEAD_DIM, s).transpose(0, 1, 4, 2, 3)
    new_v_p = vt.reshape(1, bp, N_HEADS, HEAD_DIM, s).transpose(0, 1, 4, 2, 3)
    new_lf_p = lft.transpose(0, 2, 1)[None]

    x2 = x_sample.reshape(n_s, d)
    q_s, k_s, v_s, lf_s, cq_s, svn_s, sgu_s = _in_sample(
        x2, mod_s, g1, wq, wkv, wf_pad, wus, bf_row, lng, lnb, gmat, wst, bs_full_s, same)
    cq3 = cq_s[:, :N_HEADS].reshape(bs, t_new, N_HEADS)
    cqcol = cq3.reshape(bs, t_new * N_HEADS, 1)
    cqmat = jnp.tile(cq3.transpose(0, 2, 1), (1, t_new, 1))
    kc = cache_k[0].transpose(0, 2, 3, 1).reshape(n_phys, ATT_W, PAGE)
    vc = cache_v[0].transpose(0, 2, 3, 1).reshape(n_phys, ATT_W, PAGE)
    lc = cache_logf[0].transpose(0, 2, 1)
    y_prompt, conv_p, att_s = _ffn_prompt_and_decode(
        page_table, x_prompt, att, sgu, mod_p, wo, g2, wu, wcg, wcv, bcg, bcv, wd, gf,
        q_s.reshape(bs, t_new, ATT_W), k_s.reshape(bs, t_new, ATT_W), v_s.reshape(bs, t_new, ATT_W),
        cqcol, cqmat, uo, kc, vc, lc)
    new_conv_p = conv_p[None]

    st2 = state_conv[0].reshape(2 * bs, 2 * dff)
    tok, src = np.arange(n_s)[:, None], np.arange(2 * bs)[None, :]
    smp, t_in = tok // t_new, tok % t_new
    e1 = jnp.asarray((t_in == 0) & (src == 2 * smp + 1), BF16)
    e2 = jnp.asarray((t_in <= 1) & (src == 2 * smp + t_in), BF16)
    y_s, upg, upv = _ffn_sample(x2, att_s.reshape(n_s, ATT_W), sgu_s, mod_s, wo, g2, wu, wcg, wcv,
                                bcg, bcv, st2, e1, e2, wd, gf)
    up = jnp.concatenate([upg, upv], axis=1).reshape(bs, t_new, 2 * dff)

    return (y_prompt, y_s.reshape(bs, t_new, d),
            new_k_p, new_v_p, new_lf_p, new_conv_p,
            k_s.reshape(1, bs, t_new, N_HEADS, HEAD_DIM), v_s.reshape(1, bs, t_new, N_HEADS, HEAD_DIM),
            lf_s[:, :N_HEADS].reshape(1, bs, t_new, N_HEADS),
            svn_s.reshape(1, bs, t_new, N_GROUPS, SGU_DIM),
            up[:, t_new - 2:, :][None])
```

```python
import functools

import jax
import jax.numpy as jnp
import numpy as np
from jax import lax
from jax.experimental import pallas as pl
from jax.experimental.pallas import tpu as pltpu

F32 = jnp.float32
BF16 = jnp.bfloat16

N_HEADS = 8
HEAD_DIM = 64
ATT_W = N_HEADS * HEAD_DIM
N_GROUPS = 8
SGU_DIM = 64
SGU_W = N_GROUPS * SGU_DIM
CHUNK = 128
PAGE = 128
DEC_T = 4
N_MOD = 6
EPS = 1e-6
ATT_SCALE = HEAD_DIM ** -0.5
LOG2E = 1.4426950408889634
LANES = 128
PAIR_W = 2 * HEAD_DIM
VMEM_LIMIT = 58 * 1024 * 1024

TM_IN = 512
TQ = 512
ATTN_SCORE_SLOTS = 3
TM_FFN = 512
FC = 256
PAGES_PER_GROUP = 6
DECODE_SLOTS = 6


def _cparams(sem):
    return pltpu.CompilerParams(dimension_semantics=sem, vmem_limit_bytes=VMEM_LIMIT)


def _const_spec(shape):
    nd = len(shape)
    return pl.BlockSpec(shape, lambda *_: (0,) * nd, pipeline_mode=pl.Buffered(1))


def _split3(x):
    hi = x.astype(BF16)
    r1 = x - hi.astype(F32)
    mid = r1.astype(BF16)
    lo = (r1 - mid.astype(F32)).astype(BF16)
    return hi, mid, lo


def _dot3(x, w):
    hi, mid, lo = _split3(x)
    d = functools.partial(jnp.dot, preferred_element_type=F32)
    return d(hi, w) + d(mid, w) + d(lo, w)


def _rms_mod(x, g, shift, scale):
    y = x * lax.rsqrt(jnp.mean(x * x, axis=-1, keepdims=True) + EPS)
    return (y * g) * (1.0 + scale) + shift


def _mod_rows(mod_ref, i):
    d = mod_ref.shape[1] // N_MOD
    return mod_ref[:, i * d:(i + 1) * d]


def _log_sigmoid(z):
    return jnp.minimum(z, 0.0) - jnp.log1p(jnp.exp(-jnp.abs(z)))


def _group_ln(sv, gmat, ln_g, ln_b):
    mu = jnp.dot(sv.astype(BF16), gmat, preferred_element_type=F32)
    d = sv - mu
    var = jnp.dot((d * d).astype(BF16), gmat, preferred_element_type=F32)
    return d * lax.rsqrt(var + EPS) * ln_g + ln_b


def _sgu_pairs(wm, svn_b, u, bias):
    lane = lax.broadcasted_iota(jnp.int32, (CHUNK, LANES), 1)
    outs = []
    for p in range(N_GROUPS // 2):
        rhs = svn_b[:, p * LANES:(p + 1) * LANES]
        lo = jnp.dot(wm[2 * p], rhs, preferred_element_type=F32)
        hi = jnp.dot(wm[2 * p + 1], rhs, preferred_element_type=F32)
        mixed = jnp.where(lane < SGU_DIM, lo, hi) + bias[:, p * LANES:(p + 1) * LANES]
        outs.append(u[:, p * LANES:(p + 1) * LANES] * mixed)
    return jnp.concatenate(outs, axis=1)


def _ada_kernel(c_ref, w_ref, b_ref, o_ref):
    a = jax.nn.silu(c_ref[...])
    o_ref[...] = jnp.dot(a.astype(BF16), w_ref[...].astype(BF16),
                         preferred_element_type=F32) + b_ref[...]


def _ada(c_all, w_ada, b_ada):
    n, d = c_all.shape
    cols = w_ada.shape[1]
    bn = d
    return pl.pallas_call(
        _ada_kernel,
        grid=(cols // bn,),
        in_specs=[pl.BlockSpec((n, d), lambda j: (0, 0)),
                  pl.BlockSpec((d, bn), lambda j: (0, j)),
                  pl.BlockSpec((1, bn), lambda j: (0, j))],
        out_specs=pl.BlockSpec((n, bn), lambda j: (0, j)),
        out_shape=jax.ShapeDtypeStruct((n, cols), F32),
        compiler_params=_cparams(("arbitrary",)),
        name="ada_ln",
    )(c_all, w_ada, b_ada)


def _in_prompt_kernel(x_ref, mod_ref, g1_ref, wq_ref, wkvf_ref, wus_ref, bf_ref, lng_ref, lnb_ref,
                      gmat_ref, ws_ref, bs_ref, tri_ref,
                      q_ref, kt_ref, vt_ref, ktb_ref, vtb_ref, lft_ref, cumt_ref, cum_ref, sgu_ref,
                      carry_ref):
    s = pl.program_id(1)
    tm = x_ref.shape[1]

    @pl.when(s == 0)
    def _():
        carry_ref[...] = jnp.zeros_like(carry_ref)

    h = _rms_mod(x_ref[0], g1_ref[...], mod_ref[0, 0:1, :], mod_ref[0, 1:2, :]).astype(BF16)

    q = jnp.dot(h, wq_ref[...], preferred_element_type=F32)
    q_ref[0] = (q * (ATT_SCALE * LOG2E)).astype(BF16)

    kvf = lax.dot_general(wkvf_ref[...], h, (((1,), (1,)), ((), ())), preferred_element_type=F32)
    kt = kvf[0:ATT_W]
    vt = kvf[ATT_W:2 * ATT_W]
    kt_ref[0] = kt
    vt_ref[0] = vt
    ktb_ref[0, 0] = kt.astype(BF16)
    vtb_ref[0, 0] = vt.astype(BF16)
    lft = _log_sigmoid(kvf[2 * ATT_W:2 * ATT_W + N_HEADS] + bf_ref[...])
    lft_ref[0] = lft

    r = _dot3(lft, tri_ref[...])
    carry = carry_ref[...]
    cumt = r[:, 0:tm] + jnp.concatenate([carry] * (tm // LANES), axis=1)
    carry_ref[...] = carry + r[:, tm:tm + LANES]
    cumt = cumt * LOG2E
    cumt_ref[0, 0] = cumt
    pad = jnp.zeros((LANES - N_HEADS, LANES), F32)
    for c in range(tm // LANES):
        blk = jnp.concatenate([cumt[:, c * LANES:(c + 1) * LANES], pad], axis=0)
        cum_ref[0, c * LANES:(c + 1) * LANES, :] = blk.T[:, 0:N_HEADS]

    us = jnp.dot(h, wus_ref[...], preferred_element_type=F32)
    u = jax.nn.gelu(us[:, 0:SGU_W])
    sv = jax.nn.gelu(us[:, SGU_W:2 * SGU_W])
    svn = _group_ln(sv, gmat_ref[...], lng_ref[...], lnb_ref[...]).astype(BF16)

    row = lax.broadcasted_iota(jnp.int32, (CHUNK, CHUNK), 0)
    col = lax.broadcasted_iota(jnp.int32, (CHUNK, CHUNK), 1)
    wm = [jnp.where(row >= col, ws_ref[g], 0.0).astype(BF16) for g in range(N_GROUPS)]
    bias = bs_ref[...]
    for c in range(tm // CHUNK):
        sl = slice(c * CHUNK, (c + 1) * CHUNK)
        sgu_ref[0, sl, :] = _sgu_pairs(wm, svn[sl], u[sl], bias).astype(BF16)


def _in_prompt(x, mod3, g1, wq, wkvf, wus, bf_col, lng, lnb, gmat, ws_b, bs_full, tri):
    b, s, d = x.shape
    tm = TM_IN
    ns = s // tm
    grid = (b, ns)
    row_blk = lambda w: pl.BlockSpec((1, tm, w), lambda i, j: (i, j, 0))
    colT_blk = lambda r: pl.BlockSpec((1, r, tm), lambda i, j: (i, 0, j))
    tiledT_blk = lambda r: pl.BlockSpec((1, 1, r, tm), lambda i, j: (i, j, 0, 0))
    out_shape = (
        jax.ShapeDtypeStruct((b, s, ATT_W), BF16),
        jax.ShapeDtypeStruct((b, ATT_W, s), F32),
        jax.ShapeDtypeStruct((b, ATT_W, s), F32),
        jax.ShapeDtypeStruct((b, ns, ATT_W, tm), BF16),
        jax.ShapeDtypeStruct((b, ns, ATT_W, tm), BF16),
        jax.ShapeDtypeStruct((b, N_HEADS, s), F32),
        jax.ShapeDtypeStruct((b, ns, N_HEADS, tm), F32),
        jax.ShapeDtypeStruct((b, s, N_HEADS), F32),
        jax.ShapeDtypeStruct((b, s, SGU_W), BF16),
    )
    out_specs = (row_blk(ATT_W), colT_blk(ATT_W), colT_blk(ATT_W), tiledT_blk(ATT_W), tiledT_blk(ATT_W),
                 colT_blk(N_HEADS), tiledT_blk(N_HEADS), row_blk(N_HEADS), row_blk(SGU_W))
    in_specs = [row_blk(d),
                pl.BlockSpec((1, N_MOD, d), lambda i, j: (i, 0, 0)),
                _const_spec(g1.shape), _const_spec(wq.shape), _const_spec(wkvf.shape), _const_spec(wus.shape),
                _const_spec(bf_col.shape), _const_spec(lng.shape), _const_spec(lnb.shape),
                _const_spec(gmat.shape), _const_spec(ws_b.shape), _const_spec(bs_full.shape),
                _const_spec(tri.shape)]
    return pl.pallas_call(
        _in_prompt_kernel, grid=grid, in_specs=in_specs, out_specs=out_specs, out_shape=out_shape,
        scratch_shapes=[pltpu.VMEM((N_HEADS, LANES), F32)],
        compiler_params=_cparams(("arbitrary", "arbitrary")),
        name="prompt_in_proj",
    )(x, mod3, g1, wq, wkvf, wus, bf_col, lng, lnb, gmat, ws_b, bs_full, tri)


def _attn_kernel(q_ref, kt_ref, vt_ref, cumt_ref, cum_ref, o_ref, qm_ref, s_ref, cq_ref, m_ref, acc_ref):
    qi = pl.program_id(1)
    tq = q_ref.shape[1]
    tk = kt_ref.shape[3]
    qs = qi * tq
    j_diag = (qs + tq - 1) // tk
    n_slots = s_ref.shape[0]
    ahead = n_slots - 1
    lane = lax.broadcasted_iota(jnp.int32, (tq, LANES), 1)
    qpos = qs + lax.broadcasted_iota(jnp.int32, (tq, tk), 0)
    kcol = lax.broadcasted_iota(jnp.int32, (tq, tk), 1)
    ones_half = jnp.ones((HEAD_DIM, tk), BF16)
    nt = (((1,), (1,)), ((), ()))

    for p in range(N_HEADS // 2):
        qp = q_ref[0, :, p * PAIR_W:(p + 1) * PAIR_W].astype(F32)
        qm_ref[2 * p] = jnp.where(lane < HEAD_DIM, qp, 0.0).astype(BF16)
        qm_ref[2 * p + 1] = jnp.where(lane >= HEAD_DIM, qp, 0.0).astype(BF16)
    for h in range(N_HEADS):
        cq_ref[h] = jnp.broadcast_to(cum_ref[0, :, h:h + 1], (tq, LANES))
    m_ref[...] = jnp.full(m_ref.shape, -jnp.inf, F32)
    acc_ref[...] = jnp.zeros(acc_ref.shape, F32)

    def scores(j, h):
        p = h // 2
        kt = kt_ref[0, j, p * PAIR_W:(p + 1) * PAIR_W, :]
        s_ref[h % n_slots] = jnp.dot(qm_ref[h], kt, preferred_element_type=F32)

    def softmax_pv(j, h, masked):
        p, half = divmod(h, 2)
        t = s_ref[h % n_slots] - cumt_ref[0, j, h:h + 1, :]
        if masked:
            t = jnp.where(j * tk + kcol <= qpos, t, -jnp.inf)
        cq = cq_ref[h]
        m = m_ref[h]
        m_new = jnp.maximum(m, jnp.max(t, axis=1, keepdims=True) + cq)
        alpha = jnp.exp2(m - m_new)
        d = cq - m_new
        pr = jnp.exp2(t + jnp.concatenate([d] * (tk // LANES), axis=1)).astype(BF16)
        vt = vt_ref[0, j, p * PAIR_W:(p + 1) * PAIR_W, :]
        if half == 0:
            vaug = jnp.concatenate([vt[0:HEAD_DIM], ones_half], axis=0)
        else:
            vaug = jnp.concatenate([ones_half, vt[HEAD_DIM:PAIR_W]], axis=0)
        acc_ref[h] = alpha * acc_ref[h] + lax.dot_general(pr, vaug, nt, preferred_element_type=F32)
        m_ref[h] = m_new

    def key_tile(j, masked):
        for h in range(N_HEADS):
            softmax_pv(j, h, masked)
            nh = h + ahead
            if nh < N_HEADS:
                scores(j, nh)
            elif not masked:
                scores(j + 1, nh - N_HEADS)

    for h in range(ahead):
        scores(0, h)

    def loop_body(j, carry):
        key_tile(j, False)
        return carry

    lax.fori_loop(0, j_diag, loop_body, 0)
    key_tile(j_diag, True)

    for p in range(N_HEADS // 2):
        a0 = acc_ref[2 * p]
        a1 = acc_ref[2 * p + 1]
        o0 = a0 / pltpu.roll(a0, HEAD_DIM, axis=1)
        o1 = a1 / pltpu.roll(a1, HEAD_DIM, axis=1)
        o_ref[0, :, p * PAIR_W:(p + 1) * PAIR_W] = jnp.where(lane < HEAD_DIM, o0, o1).astype(BF16)


def _attn_prompt(q, ktb, vtb, cumt, cum):
    b, s, _ = q.shape
    nk, tk = ktb.shape[1], ktb.shape[3]
    tq = TQ
    return pl.pallas_call(
        _attn_kernel,
        grid=(b, s // tq),
        in_specs=[pl.BlockSpec((1, tq, ATT_W), lambda i, j: (i, j, 0)),
                  pl.BlockSpec((1, nk, ATT_W, tk), lambda i, j: (i, 0, 0, 0)),
                  pl.BlockSpec((1, nk, ATT_W, tk), lambda i, j: (i, 0, 0, 0)),
                  pl.BlockSpec((1, nk, N_HEADS, tk), lambda i, j: (i, 0, 0, 0)),
                  pl.BlockSpec((1, tq, N_HEADS), lambda i, j: (i, j, 0))],
        out_specs=pl.BlockSpec((1, tq, ATT_W), lambda i, j: (i, j, 0)),
        out_shape=jax.ShapeDtypeStruct((b, s, ATT_W), BF16),
        scratch_shapes=[pltpu.VMEM((N_HEADS, tq, PAIR_W), BF16),
                        pltpu.VMEM((ATTN_SCORE_SLOTS, tq, tk), F32),
                        pltpu.VMEM((N_HEADS, tq, LANES), F32),
                        pltpu.VMEM((N_HEADS, tq, LANES), F32),
                        pltpu.VMEM((N_HEADS, tq, PAIR_W), F32)],
        compiler_params=_cparams(("arbitrary", "arbitrary")),
        name="prompt_attention",
    )(q, ktb, vtb, cumt, cum)


def _transpose_rows8(blocks):
    sub = lax.broadcasted_iota(jnp.int32, blocks[0].shape, 0)
    a = list(blocks)
    for s in (4, 2, 1):
        keep = (sub & s) == 0
        b = list(a)
        for i in range(8):
            if i & s == 0:
                lo, hi = a[i], a[i + s]
                b[i] = jnp.where(keep, lo, pltpu.roll(hi, s, axis=0))
                b[i + s] = jnp.where(keep, pltpu.roll(lo, 8 - s, axis=0), hi)
        a = b
    return a


def _interleave_rows(x, nv):
    out = [None] * nv
    for jh in range(nv // 8):
        t = _transpose_rows8([x[r * nv + jh * 8:r * nv + jh * 8 + 8, :] for r in range(8)])
        for jl in range(8):
            out[8 * jh + jl] = t[jl]
    return jnp.concatenate(out, axis=0)


def _deinterleave_rows(y, nv):
    nb = nv // 8
    out = [None] * nv
    for jh in range(nb):
        t = _transpose_rows8([y[(8 * jh + jl) * 8:(8 * jh + jl) * 8 + 8, :] for jl in range(8)])
        for r in range(8):
            out[r * nb + jh] = t[r]
    return jnp.concatenate(out, axis=0)


def _causal_conv3(up, prev_ref, c, wc_ref, bc_ref):
    th = up.shape[0]
    first_row = lax.broadcasted_iota(jnp.int32, (8, up.shape[1]), 0) == 0

    def wrap(prev_blk, cur_blk):
        return jnp.where(first_row, pltpu.roll(prev_blk, 1, axis=0), pltpu.roll(cur_blk, 1, axis=0))

    w1 = wrap(prev_ref[c, 8:16, :], up[th - 8:th])
    w2 = wrap(prev_ref[c, 0:8, :], up[th - 16:th - 8])
    s1 = jnp.concatenate([w1, up[0:th - 8]], axis=0)
    s2 = jnp.concatenate([w2, w1, up[0:th - 16]], axis=0)
    prev_ref[c] = up[th - 16:th]
    return bc_ref[c] + s2 * wc_ref[c, 0:1, :] + s1 * wc_ref[c, 1:2, :] + up * wc_ref[c, 2:3, :]


def _in_sample_kernel(x_ref, mod_ref, g1_ref, wq_ref, wkv_ref, wf_ref, wus_ref, bf_ref, lng_ref, lnb_ref,
                      gmat_ref, wst_ref, bs_ref, same_ref,
                      q_ref, k_ref, v_ref, lf_ref, cq_ref, svn_ref, sgu_ref):
    n = x_ref.shape[0]
    h = _rms_mod(x_ref[...], g1_ref[...], _mod_rows(mod_ref, 0), _mod_rows(mod_ref, 1)).astype(BF16)
    q_ref[...] = jnp.dot(h, wq_ref[...], preferred_element_type=F32) * ATT_SCALE
    kv = jnp.dot(h, wkv_ref[...], preferred_element_type=F32)
    k_ref[...] = kv[:, 0:ATT_W]
    v_ref[...] = kv[:, ATT_W:2 * ATT_W]
    lf = _log_sigmoid(jnp.dot(h, wf_ref[...], preferred_element_type=F32) + bf_ref[...])
    lf_ref[...] = lf

    same = same_ref[...]
    hi, mid, lo = _split3(lf)
    tri = same.astype(BF16)
    d = functools.partial(jnp.dot, preferred_element_type=F32)
    cq_ref[...] = d(tri, hi) + d(tri, mid) + d(tri, lo)

    us = jnp.dot(h, wus_ref[...], preferred_element_type=F32)
    u = jax.nn.gelu(us[:, 0:SGU_W])
    sv = jax.nn.gelu(us[:, SGU_W:2 * SGU_W])
    svn = _group_ln(sv, gmat_ref[...], lng_ref[...], lnb_ref[...])
    svn_ref[...] = svn
    r_i = lax.broadcasted_iota(jnp.int32, (n, n), 0)
    c_i = lax.broadcasted_iota(jnp.int32, (n, n), 1)
    pick_rows = jnp.where(c_i == r_i % DEC_T, 1.0, 0.0).astype(BF16)
    pick_cols = jnp.where(r_i == c_i % DEC_T, 1.0, 0.0).astype(BF16)
    wm = []
    for g in range(N_GROUPS):
        rows = jnp.dot(pick_rows, wst_ref[g].astype(BF16), preferred_element_type=F32)
        tiled = jnp.dot(rows.astype(BF16), pick_cols, preferred_element_type=F32)
        wm.append(jnp.where(same > 0, tiled, 0.0).astype(BF16))
    sgu_ref[...] = _sgu_pairs(wm, svn.astype(BF16), u, bs_ref[...]).astype(BF16)


def _in_sample(x2, mod_rows, g1, wq, wkv, wf, wus, bf_row, lng, lnb, gmat, wst, bs_full, same):
    n, d = x2.shape
    args = (x2, mod_rows, g1, wq, wkv, wf, wus, bf_row, lng, lnb, gmat, wst, bs_full, same)
    out_shape = (jax.ShapeDtypeStruct((n, ATT_W), F32), jax.ShapeDtypeStruct((n, ATT_W), F32),
                 jax.ShapeDtypeStruct((n, ATT_W), F32), jax.ShapeDtypeStruct((n, LANES), F32),
                 jax.ShapeDtypeStruct((n, LANES), F32), jax.ShapeDtypeStruct((n, SGU_W), F32),
                 jax.ShapeDtypeStruct((n, SGU_W), BF16))
    return pl.pallas_call(
        _in_sample_kernel, grid=(1,),
        in_specs=[_const_spec((n,) + a.shape[1:] if a is mod_rows else a.shape) for a in args],
        out_specs=tuple(pl.BlockSpec(o.shape, lambda i: (0, 0)) for o in out_shape),
        out_shape=out_shape,
        compiler_params=_cparams(("arbitrary",)),
        name="sample_in_proj",
    )(*args)


def _decode_begin(q, kn, vn, cqcol, cqmat_ref, bb):
    t_new = q.shape[0]
    rows = t_new * N_HEADS
    sub = lax.broadcasted_iota(jnp.int32, (N_HEADS, ATT_W), 0)
    lane_head = lax.broadcasted_iota(jnp.int32, (N_HEADS, ATT_W), 1) // HEAD_DIM
    own = sub == lane_head
    qbd = jnp.concatenate(
        [jnp.where(own, jnp.broadcast_to(q[t:t + 1, :], (N_HEADS, ATT_W)), 0.0) for t in range(t_new)], axis=0)
    rt = lax.broadcasted_iota(jnp.int32, (rows, 1), 0) // N_HEADS
    s_new = []
    for t2 in range(t_new):
        sc = jnp.sum(qbd * kn[t2:t2 + 1, :], axis=1, keepdims=True) + cqcol - cqmat_ref[bb, :, t2:t2 + 1]
        s_new.append(jnp.where(rt >= t2, sc, -jnp.inf))
    m = s_new[0]
    for t2 in range(1, t_new):
        m = jnp.maximum(m, s_new[t2])
    l = jnp.zeros((rows, 1), F32)
    acc = jnp.zeros((rows, ATT_W), F32)
    for t2 in range(t_new):
        pr = jnp.exp(s_new[t2] - m)
        l = l + pr
        acc = acc + pr * vn[t2:t2 + 1, :]
    return dict(qbd_b=qbd.astype(BF16), cqcol=cqcol, own=own, t_new=t_new,
                m=m, l=l, acc=acc, c_run=jnp.zeros((N_HEADS, PAGE), F32))


def _decode_scores(st, kbuf, lbuf, slot, uo, valid):
    g_pages = kbuf.shape[1]
    t_new = st["t_new"]
    c_run = st["c_run"]
    lp = lbuf[slot].reshape(g_pages * N_HEADS, PAGE)
    r = _dot3(lp, uo)
    sufs = [None] * g_pages
    for i in reversed(range(g_pages)):
        sufs[i] = jnp.where(valid[i], r[i * N_HEADS:(i + 1) * N_HEADS, 0:PAGE] + c_run, -jnp.inf)
        c_run = c_run + r[i * N_HEADS:(i + 1) * N_HEADS, PAGE:2 * PAGE]
    scs = []
    for i in range(0, g_pages, 2):
        kt2 = jnp.concatenate([kbuf[slot, i].astype(BF16), kbuf[slot, i + 1].astype(BF16)], axis=1)
        sc = jnp.dot(st["qbd_b"], kt2, preferred_element_type=F32)
        bias = jnp.concatenate([jnp.concatenate([sufs[i]] * t_new, axis=0),
                                jnp.concatenate([sufs[i + 1]] * t_new, axis=0)], axis=1)
        scs.append(sc + bias + st["cqcol"])
    return scs, dict(st, c_run=c_run)


def _decode_update(st, scs, vbuf, slot):
    nt = (((1,), (1,)), ((), ()))
    m, l, acc = st["m"], st["l"], st["acc"]
    m_new = m
    for sc in scs:
        m_new = jnp.maximum(m_new, jnp.max(sc, axis=1, keepdims=True))
    alpha = jnp.exp(m - m_new)
    l = alpha * l
    acc = alpha * acc
    for idx, sc in enumerate(scs):
        i = 2 * idx
        pr = jnp.exp(sc - m_new)
        l = l + jnp.sum(pr, axis=1, keepdims=True)
        vt2 = jnp.concatenate([vbuf[slot, i].astype(BF16), vbuf[slot, i + 1].astype(BF16)], axis=1)
        acc = acc + lax.dot_general(pr.astype(BF16), vt2, nt, preferred_element_type=F32)
    return dict(st, m=m_new, l=l, acc=acc)


def _decode_end(st, o_ref, bb):
    o = st["acc"] / st["l"]
    for t in range(st["t_new"]):
        blk = jnp.where(st["own"], o[t * N_HEADS:(t + 1) * N_HEADS, :], 0.0)
        o_ref[bb, t:t + 1, :] = jnp.sum(blk, axis=0, keepdims=True)


def _ffn_decode_kernel(pt_ref, x_ref, att_ref, sgu_ref, mod_ref, wo_ref, g2_ref, wu_ref, wcg_ref,
                       wcv_ref, bcg_ref, bcv_ref, wd_ref, gf_ref,
                       q_ref, kn_ref, vn_ref, cqcol_ref, cqmat_ref, uo_ref, kc_hbm, vc_hbm, lc_hbm,
                       y_ref, conv_ref, o_ref,
                       x1_ref, h2_ref, f_ref, pg_ref, pv_ref, kbuf, vbuf, lbuf, sem):
    s = pl.program_id(1)
    step = pl.program_id(0) * pl.num_programs(1) + s
    n_steps = pl.num_programs(0) * pl.num_programs(1)
    tm = x_ref.shape[1]
    n_chunks = wd_ref.shape[0]
    th = tm // 2
    nv = th // 8
    halves = (slice(0, th), slice(th, tm))
    n_pages = pt_ref.shape[1]
    g_pages = kbuf.shape[1]
    n_regions = 2 * n_chunks
    n_slots = kbuf.shape[0]
    ahead = n_slots - 2
    total = n_steps * n_regions

    def page_of(r, i):
        return n_pages - g_pages * (r + 1) + i

    def copies(n, slot):
        bb = n // n_regions
        r = n % n_regions
        out = []
        for i in range(g_pages):
            phys = pt_ref[bb, jnp.maximum(page_of(r, i), 0)]
            out.append(pltpu.make_async_copy(kc_hbm.at[phys], kbuf.at[slot, i], sem.at[0, slot]))
            out.append(pltpu.make_async_copy(vc_hbm.at[phys], vbuf.at[slot, i], sem.at[1, slot]))
            out.append(pltpu.make_async_copy(lc_hbm.at[phys], lbuf.at[slot, i], sem.at[2, slot]))
        return out

    @pl.when(step == 0)
    def _():
        for n0 in range(ahead):
            for c in copies(n0, n0):
                c.start()

    @pl.when(s == 0)
    def _():
        pg_ref[...] = jnp.zeros(pg_ref.shape, F32)
        pv_ref[...] = jnp.zeros(pv_ref.shape, F32)

    for i, rows in enumerate(halves):
        mix = (jnp.dot(att_ref[0, rows, :], wo_ref[0:ATT_W, :], preferred_element_type=F32)
               + jnp.dot(sgu_ref[0, rows, :], wo_ref[ATT_W:ATT_W + SGU_W, :], preferred_element_type=F32))
        x1 = _interleave_rows(x_ref[0, rows, :] + mod_ref[0, 2:3, :] * mix, nv)
        x1_ref[rows, :] = x1
        h2_ref[i] = _rms_mod(x1, g2_ref[...], mod_ref[0, 3:4, :], mod_ref[0, 4:5, :]).astype(BF16)
    f_ref[...] = jnp.zeros(f_ref.shape, F32)

    uo = uo_ref[...]
    st0 = _decode_begin(q_ref[0], kn_ref[0], vn_ref[0], cqcol_ref[0], cqmat_ref, 0)

    def trip(c, carry):
        st = dict(st0, m=carry[0], l=carry[1], acc=carry[2], c_run=carry[3])
        regions = [2 * c, 2 * c + 1]
        ns_ = [step * n_regions + r for r in regions]
        slots = [n % n_slots for n in ns_]
        for n in ns_:
            @pl.when(n + ahead < total)
            def _(n=n):
                for cp in copies(n + ahead, (n + ahead) % n_slots):
                    cp.start()
        for n, slot in zip(ns_, slots):
            for cp in copies(n, slot):
                cp.wait()
        gate = pl.ds(pl.multiple_of(c * FC, FC), FC)
        value = pl.ds(pl.multiple_of((n_chunks + c) * FC, FC), FC)
        ups = [(jnp.dot(h2_ref[i], wu_ref[:, gate], preferred_element_type=F32),
                jnp.dot(h2_ref[i], wu_ref[:, value], preferred_element_type=F32)) for i in range(2)]
        scs = []
        for r, slot in zip(regions, slots):
            valid = [page_of(r, k) >= 0 for k in range(g_pages)]
            sc, st = _decode_scores(st, kbuf, lbuf, slot, uo, valid)
            scs.append(sc)

        def down(i):
            cg = _causal_conv3(ups[i][0], pg_ref, c, wcg_ref, bcg_ref)
            cv = _causal_conv3(ups[i][1], pv_ref, c, wcv_ref, bcv_ref)
            act = (jax.nn.silu(cg) * cv).astype(BF16)
            f_ref[i] += jnp.dot(act, wd_ref[c], preferred_element_type=F32)

        down(0)
        st = _decode_update(st, scs[0], vbuf, slots[0])
        down(1)
        st = _decode_update(st, scs[1], vbuf, slots[1])
        return st["m"], st["l"], st["acc"], st["c_run"]

    m, l, acc, c_run = lax.fori_loop(0, n_chunks, trip, (st0["m"], st0["l"], st0["acc"], st0["c_run"]))
    _decode_end(dict(st0, m=m, l=l, acc=acc, c_run=c_run), o_ref, 0)

    for i, rows in enumerate(halves):
        x2 = x1_ref[rows, :] + mod_ref[0, 5:6, :] * f_ref[i]
        y = x2 * lax.rsqrt(jnp.mean(x2 * x2, axis=-1, keepdims=True) + EPS) * gf_ref[...]
        y_ref[0, rows, :] = _deinterleave_rows(y, nv)
    fc = pg_ref.shape[2]
    for c in range(n_chunks):
        for k, row in enumerate((7, 15)):
            conv_ref[0, k:k + 1, c * fc:(c + 1) * fc] = pg_ref[c, row:row + 1, :]
            conv_ref[0, k:k + 1, (n_chunks + c) * fc:(n_chunks + c + 1) * fc] = pv_ref[c, row:row + 1, :]


def _ffn_prompt_and_decode(page_table, x, att, sgu, mod3, wo, g2, wu, wcg, wcv, bcg, bcv, wd, gf,
                           q3, k3, v3, cqcol, cqmat, uo, kc, vc, lc):
    b, s, d = x.shape
    tm = TM_FFN
    ns = s // tm
    n_chunks, fc, _ = wd.shape
    dff = n_chunks * fc
    nb, t_new, _ = q3.shape
    g = PAGES_PER_GROUP
    assert nb == b * ns and -(-page_table.shape[1] // g) == 2 * n_chunks and g % 2 == 0
    rows = t_new * N_HEADS
    consts = (wo, g2, wu, wcg, wcv, bcg, bcv, wd, gf)
    row_blk = lambda w: pl.BlockSpec((1, tm, w), lambda i, j, pt: (i, j, 0))
    smp_blk = lambda r, w: pl.BlockSpec((1, r, w), lambda i, j, pt: (i * ns + j, 0, 0))
    const_blk = lambda a: pl.BlockSpec(a.shape, lambda i, j, pt: (0,) * a.ndim, pipeline_mode=pl.Buffered(1))
    grid_spec = pltpu.PrefetchScalarGridSpec(
        num_scalar_prefetch=1,
        grid=(b, ns),
        in_specs=[row_blk(d), row_blk(ATT_W), row_blk(SGU_W),
                  pl.BlockSpec((1, N_MOD, d), lambda i, j, pt: (i, 0, 0))]
                 + [const_blk(a) for a in consts]
                 + [smp_blk(t_new, ATT_W), smp_blk(t_new, ATT_W), smp_blk(t_new, ATT_W),
                    smp_blk(rows, 1), smp_blk(rows, t_new), const_blk(uo),
                    pl.BlockSpec(memory_space=pl.ANY), pl.BlockSpec(memory_space=pl.ANY),
                    pl.BlockSpec(memory_space=pl.ANY)],
        out_specs=(row_blk(d), pl.BlockSpec((1, 2, 2 * dff), lambda i, j, pt: (i, 0, 0)),
                   smp_blk(t_new, ATT_W)),
        scratch_shapes=[pltpu.VMEM((tm, d), F32),
                        pltpu.VMEM((2, tm // 2, d), BF16),
                        pltpu.VMEM((2, tm // 2, d), F32),
                        pltpu.VMEM((n_chunks, 16, fc), F32),
                        pltpu.VMEM((n_chunks, 16, fc), F32),
                        pltpu.VMEM((DECODE_SLOTS, g, ATT_W, PAGE), F32),
                        pltpu.VMEM((DECODE_SLOTS, g, ATT_W, PAGE), F32),
                        pltpu.VMEM((DECODE_SLOTS, g, N_HEADS, PAGE), F32),
                        pltpu.SemaphoreType.DMA((3, DECODE_SLOTS))],
    )
    return pl.pallas_call(
        _ffn_decode_kernel, grid_spec=grid_spec,
        out_shape=(jax.ShapeDtypeStruct((b, s, d), F32), jax.ShapeDtypeStruct((b, 2, 2 * dff), F32),
                   jax.ShapeDtypeStruct((nb, t_new, ATT_W), F32)),
        compiler_params=_cparams(("arbitrary", "arbitrary")),
        name="prompt_ffn_decode",
    )(page_table, x, att, sgu, mod3, *consts, q3, k3, v3, cqcol, cqmat, uo, kc, vc, lc)


def _ffn_sample_kernel(x_ref, att_ref, sgu_ref, mod_ref, wo_ref, g2_ref, wug_ref, wuv_ref, wcg_ref, wcv_ref,
                       bcg_ref, bcv_ref, stg_ref, stv_ref, e1_ref, e2_ref, wd_ref, gf_ref,
                       y_ref, upg_ref, upv_ref,
                       x1_ref, h2_ref, acc_ref):
    j = pl.program_id(0)
    n = x_ref.shape[0]
    t_new = 4

    @pl.when(j == 0)
    def _():
        mix = (jnp.dot(att_ref[...].astype(BF16), wo_ref[0:ATT_W, :], preferred_element_type=F32)
               + jnp.dot(sgu_ref[...], wo_ref[ATT_W:ATT_W + SGU_W, :], preferred_element_type=F32))
        x1 = x_ref[...] + _mod_rows(mod_ref, 2) * mix
        x1_ref[...] = x1
        h2_ref[...] = _rms_mod(x1, g2_ref[...], _mod_rows(mod_ref, 3), _mod_rows(mod_ref, 4)).astype(BF16)
        acc_ref[...] = jnp.zeros_like(acc_ref)

    h2 = h2_ref[...]
    tpos = lax.broadcasted_iota(jnp.int32, (n, FC), 0) % t_new

    def place(e_ref, parts):
        d = functools.partial(jnp.dot, preferred_element_type=F32)
        return d(e_ref[...], parts[0]) + d(e_ref[...], parts[1]) + d(e_ref[...], parts[2])

    def conv(up, wc_ref, bc_ref, st_ref):
        parts = _split3(st_ref[...])
        s1 = jnp.where(tpos >= 1, pltpu.roll(up, 1, axis=0), place(e1_ref, parts))
        s2 = jnp.where(tpos >= 2, pltpu.roll(up, 2, axis=0), place(e2_ref, parts))
        return bc_ref[0] + s2 * wc_ref[0, 0:1, :] + s1 * wc_ref[0, 1:2, :] + up * wc_ref[0, 2:3, :]

    upg = jnp.dot(h2, wug_ref[...], preferred_element_type=F32)
    upv = jnp.dot(h2, wuv_ref[...], preferred_element_type=F32)
    upg_ref[...] = upg
    upv_ref[...] = upv
    cg = conv(upg, wcg_ref, bcg_ref, stg_ref)
    cv = conv(upv, wcv_ref, bcv_ref, stv_ref)
    act = (jax.nn.silu(cg) * cv).astype(BF16)
    acc_ref[...] += jnp.dot(act, wd_ref[0], preferred_element_type=F32)

    @pl.when(j == pl.num_programs(0) - 1)
    def _():
        x2 = x1_ref[...] + _mod_rows(mod_ref, 5) * acc_ref[...]
        y_ref[...] = x2 * lax.rsqrt(jnp.mean(x2 * x2, axis=-1, keepdims=True) + EPS) * gf_ref[...]


def _ffn_sample(x2, att, sgu, mod_rows, wo, g2, wu, wcg, wcv, bcg, bcv, st2, e1, e2, wd, gf):
    n, d = x2.shape
    nf = wd.shape[0]
    dff = nf * FC
    full = lambda a: _const_spec(a.shape)
    colc = lambda r: pl.BlockSpec((r, FC), lambda j: (0, j))
    chunk = lambda a: pl.BlockSpec((1,) + a.shape[1:], lambda j: (j, 0, 0))
    in_specs = [full(x2), full(att), full(sgu), _const_spec((n, mod_rows.shape[1])), full(wo), full(g2),
                colc(d), pl.BlockSpec((d, FC), lambda j: (0, nf + j)),
                chunk(wcg), chunk(wcv), chunk(bcg), chunk(bcv),
                pl.BlockSpec((st2.shape[0], FC), lambda j: (0, j)),
                pl.BlockSpec((st2.shape[0], FC), lambda j: (0, nf + j)),
                full(e1), full(e2),
                chunk(wd), full(gf)]
    return pl.pallas_call(
        _ffn_sample_kernel, grid=(nf,),
        in_specs=in_specs,
        out_specs=(pl.BlockSpec((n, d), lambda j: (0, 0)), colc(n), colc(n)),
        out_shape=(jax.ShapeDtypeStruct((n, d), F32), jax.ShapeDtypeStruct((n, dff), F32),
                   jax.ShapeDtypeStruct((n, dff), F32)),
        scratch_shapes=[pltpu.VMEM((n, d), F32), pltpu.VMEM((n, d), BF16), pltpu.VMEM((n, d), F32)],
        compiler_params=_cparams(("arbitrary",)),
        name="sample_ffn",
    )(x2, att, sgu, mod_rows, wo, g2, wu, wu, wcg, wcv, bcg, bcv, st2, st2, e1, e2, wd, gf)


def kernel(x_prompt, x_sample, c_prompt, c_sample, cache_k, cache_v, cache_logf, state_conv, page_table,
           w_ada, b_ada, norm1_g, w_in, b_f, ln_v_g, ln_v_b, w_s, b_s, w_o, norm2_g, w_up, w_conv, b_conv,
           w_down, final_g):
    bp, s, d = x_prompt.shape
    bs, t_new, _ = x_sample.shape
    n_s = bs * t_new
    dff = w_down.shape[1]
    n_phys = cache_k.shape[1]
    assert w_ada.shape[0] == 1, "single layer"
    assert s % TM_IN == 0 and s % TQ == 0 and s % TM_FFN == 0 and TM_IN % TQ == 0
    assert dff % FC == 0 and n_s == CHUNK and t_new == 4
    assert bs == bp * (s // TM_FFN), "one decode sample rides along with each FFN grid step"

    wi = w_in[0]
    k0, v0, f0 = ATT_W, 2 * ATT_W, 3 * ATT_W
    u0 = f0 + N_HEADS
    wq = wi[:, 0:k0].astype(BF16)
    wkv = wi[:, k0:f0].astype(BF16)
    wf = wi[:, f0:u0]
    wus = wi[:, u0:].astype(BF16)
    wkvf_t = jnp.concatenate([wi[:, k0:f0], wf], axis=1).T.astype(BF16)
    wf_pad = jnp.pad(wf, ((0, 0), (0, LANES - N_HEADS))).astype(BF16)
    bf_col = b_f[0].reshape(N_HEADS, 1)
    bf_row = jnp.pad(b_f[0].reshape(1, N_HEADS), ((0, 0), (0, LANES - N_HEADS)))
    lng = ln_v_g[0].reshape(1, SGU_W)
    lnb = ln_v_b[0].reshape(1, SGU_W)
    gidx = np.arange(SGU_W) // SGU_DIM
    gmat = jnp.asarray(np.where(gidx[:, None] == gidx[None, :], 1.0 / SGU_DIM, 0.0), BF16)
    ws_b = w_s[0]
    bs_full = jnp.repeat(b_s[0].T, SGU_DIM, axis=1)
    wst = w_s[0]
    bs_full_s = jnp.tile(jnp.repeat(b_s[0][:, :t_new].T, SGU_DIM, axis=1), (bs, 1))
    r_idx = np.arange(n_s)
    same = jnp.asarray((r_idx[:, None] // t_new == r_idx[None, :] // t_new)
                       & (r_idx[:, None] >= r_idx[None, :]), F32)
    pos = np.arange(TM_IN)
    tri = jnp.asarray(np.concatenate([pos[:, None] <= pos[None, :],
                                      np.ones((TM_IN, LANES), bool)], axis=1), BF16)
    pp = np.arange(PAGE)
    uo = jnp.asarray(np.concatenate([pp[:, None] > pp[None, :],
                                     np.ones((PAGE, PAGE), bool)], axis=1), BF16)
    wo = w_o[0].astype(BF16)
    nfc = dff // FC
    chunked = lambda a: a.reshape(a.shape[0], nfc, FC).transpose(1, 0, 2)
    wu = w_up[0].astype(BF16)
    wcg, wcv = chunked(w_conv[0][:, :dff]), chunked(w_conv[0][:, dff:])
    bcg, bcv = chunked(b_conv[:, :dff]), chunked(b_conv[:, dff:])
    wd = w_down[0].astype(BF16).reshape(nfc, FC, d)
    g1 = norm1_g
    g2 = norm2_g
    gf = final_g.reshape(1, d)

    c_all = jnp.concatenate([jnp.repeat(c_sample, t_new, axis=0), c_prompt], axis=0)
    mod = _ada(c_all, w_ada[0], b_ada)
    mod_p = mod[n_s:].reshape(bp, N_MOD, d)
    mod_s = mod

    q, kt, vt, ktb, vtb, lft, cumt, cum, sgu = _in_prompt(
        x_prompt, mod_p, g1, wq, wkvf_t, wus, bf_col, lng, lnb, gmat, ws_b, bs_full, tri)
    att = _attn_prompt(q, ktb, vtb, cumt, cum)
    new_k_p = kt.reshape(1, bp, N_HEADS, HEAD_DIM, s).transpose(0, 1, 4, 2, 3)
    new_v_p = vt.reshape(1, bp, N_HEADS, HEAD_DIM, s).transpose(0, 1, 4, 2, 3)
    new_lf_p = lft.transpose(0, 2, 1)[None]

    x2 = x_sample.reshape(n_s, d)
    q_s, k_s, v_s, lf_s, cq_s, svn_s, sgu_s = _in_sample(
        x2, mod_s, g1, wq, wkv, wf_pad, wus, bf_row, lng, lnb, gmat, wst, bs_full_s, same)
    cq3 = cq_s[:, :N_HEADS].reshape(bs, t_new, N_HEADS)
    cqcol = cq3.reshape(bs, t_new * N_HEADS, 1)
    cqmat = jnp.tile(cq3.transpose(0, 2, 1), (1, t_new, 1))
    kc = cache_k[0].transpose(0, 2, 3, 1).reshape(n_phys, ATT_W, PAGE)
    vc = cache_v[0].transpose(0, 2, 3, 1).reshape(n_phys, ATT_W, PAGE)
    lc = cache_logf[0].transpose(0, 2, 1)
    y_prompt, conv_p, att_s = _ffn_prompt_and_decode(
        page_table, x_prompt, att, sgu, mod_p, wo, g2, wu, wcg, wcv, bcg, bcv, wd, gf,
        q_s.reshape(bs, t_new, ATT_W), k_s.reshape(bs, t_new, ATT_W), v_s.reshape(bs, t_new, ATT_W),
        cqcol, cqmat, uo, kc, vc, lc)
    new_conv_p = conv_p[None]

    st2 = state_conv[0].reshape(2 * bs, 2 * dff)
    tok, src = np.arange(n_s)[:, None], np.arange(2 * bs)[None, :]
    smp, t_in = tok // t_new, tok % t_new
    e1 = jnp.asarray((t_in == 0) & (src == 2 * smp + 1), BF16)
    e2 = jnp.asarray((t_in <= 1) & (src == 2 * smp + t_in), BF16)
    y_s, upg, upv = _ffn_sample(x2, att_s.reshape(n_s, ATT_W), sgu_s, mod_s, wo, g2, wu, wcg, wcv,
                                bcg, bcv, st2, e1, e2, wd, gf)
    up = jnp.concatenate([upg, upv], axis=1).reshape(bs, t_new, 2 * dff)

    return (y_prompt, y_s.reshape(bs, t_new, d),
            new_k_p, new_v_p, new_lf_p, new_conv_p,
            k_s.reshape(1, bs, t_new, N_HEADS, HEAD_DIM), v_s.reshape(1, bs, t_new, N_HEADS, HEAD_DIM),
            lf_s[:, :N_HEADS].reshape(1, bs, t_new, N_HEADS),
            svn_s.reshape(1, bs, t_new, N_GROUPS, SGU_DIM),
            up[:, t_new - 2:, :][None])
```
